```python
import math
import jax, jax.numpy as jnp
from jax import lax
import numpy as np

D_MODEL = 2048
BATCH = 16
SEQ = 2048
DEPTH = 1

MEM_LEN = 256
EPS = 1e-6
NSA_HEADS = 8
NSA_KV_GROUPS = 2
NSA_HPG = NSA_HEADS // NSA_KV_GROUPS
HEAD_DIM = 128
CMP_BLOCK = 32
CMP_STRIDE = 16
CMP_HIDDEN = 256
SEL_BLOCK = 64
SEL_TOPK = 16
SEL_LOCAL = 2
FORCE_SCORE = 1e4
WINDOW = 512
WIN_QBLOCK = 128
SEL_QCHUNK = 32
RET_HEADS = 4
RET_QK_DIM = 128
RET_V_DIM = 256
RET_CHUNK = 128
MEM_HEADS = 4
MEM_HEAD_DIM = 256
REL_BUCKETS = 32
REL_MAX_DIST = 128
D_FF = 5632
CONV_WIDTH = 3
NSA_Q = NSA_HEADS * HEAD_DIM
NSA_KV = 3 * 2 * NSA_KV_GROUPS * HEAD_DIM
NSA_GATES = 3 * NSA_HEADS
RET_QK = RET_HEADS * RET_QK_DIM
RET_V = RET_HEADS * RET_V_DIM
MEM_Q = MEM_HEADS * MEM_HEAD_DIM
N_BRANCH = 3
BRANCH_GATES = N_BRANCH * D_MODEL
SPLITS = (NSA_Q, NSA_KV, NSA_GATES, RET_QK, RET_QK, RET_V, RET_V, MEM_Q, BRANCH_GATES)
IN_COLS = NSA_Q + NSA_KV + NSA_GATES + 2 * RET_QK + 2 * RET_V + MEM_Q + BRANCH_GATES

kernel_name = 'hybrid_nsa_retention_memory_block'


def rmsnorm(x, g):
    xf = x.astype(jnp.float32)
    y = xf * lax.rsqrt(jnp.mean(xf * xf, axis=-1, keepdims=True) + EPS)
    return (y * g.astype(jnp.float32)).astype(x.dtype)


def t5_bucket(rel):
    n = jnp.maximum(rel, 0)
    max_exact = REL_BUCKETS // 2
    nf = jnp.maximum(n, 1).astype(jnp.float32)
    large = max_exact + (jnp.log(nf / max_exact) / math.log(REL_MAX_DIST / max_exact)
                         * (REL_BUCKETS - max_exact)).astype(jnp.int32)
    large = jnp.minimum(large, REL_BUCKETS - 1)
    return jnp.where(n < max_exact, n, large)


def masked_softmax(s, mask):
    s = jnp.where(mask, s.astype(jnp.float32), -1e30)
    m = jnp.max(s, axis=-1, keepdims=True)
    p = jnp.where(mask, jnp.exp(s - m), 0.0)
    return p / jnp.maximum(jnp.sum(p, axis=-1, keepdims=True), 1e-30)


def nsa_attention(q, kv, gates, tbl, pe_k, w1_k, w2_k, pe_v, w1_v, w2_v):
    B, G, Hg, S, dh = q.shape
    scale = dh ** -0.5
    t_pos = jnp.arange(S)
    k_c, v_c = kv[0, 0], kv[0, 1]
    k_s, v_s = kv[1, 0], kv[1, 1]
    k_w, v_w = kv[2, 0], kv[2, 1]

    n_cmp = (S - CMP_BLOCK) // CMP_STRIDE + 1
    starts = jnp.arange(n_cmp) * CMP_STRIDE
    tok = starts[:, None] + jnp.arange(CMP_BLOCK)[None, :]

    def compress(t, pe, w1, w2):
        blocks = t[:, :, tok] + pe
        flat = blocks.reshape(B, G, n_cmp, CMP_BLOCK * dh)
        return jax.nn.silu(flat @ w1) @ w2

    kc = compress(k_c, pe_k, w1_k, w2_k)
    vc = compress(v_c, pe_v, w1_v, w2_v)
    rel_c = t_pos[:, None] - (starts + CMP_BLOCK - 1)[None, :]
    s_cmp = jnp.einsum('bghsd,bgnd->bghsn', q, kc) * scale + tbl[:, :, t5_bucket(rel_c)]
    p_cmp = masked_softmax(s_cmp, rel_c >= 0)
    o_cmp = jnp.einsum('bghsn,bgnd->bghsd', p_cmp.astype(vc.dtype), vc)

    n_sel = S // SEL_BLOCK
    top_k = min(SEL_TOPK, n_sel)
    blk_start = jnp.arange(n_sel) * SEL_BLOCK
    overlap = ((starts[:, None] < blk_start[None, :] + SEL_BLOCK)
               & (starts[:, None] + CMP_BLOCK > blk_start[None, :])).astype(jnp.float32)
    imp = jnp.einsum('bghsn,nj->bgsj', p_cmp, overlap)
    cur = (t_pos // SEL_BLOCK)[:, None]
    jb = jnp.arange(n_sel)[None, :]
    forced = (jb == 0) | ((cur - jb >= 0) & (cur - jb < SEL_LOCAL))
    imp = jnp.where(forced, FORCE_SCORE, imp)
    imp = jnp.where(jb <= cur, imp, -1.0)
    _, sel_idx = lax.top_k(imp, top_k)

    n_ch = S // SEL_QCHUNK
    L = top_k * SEL_BLOCK
    q_ch = jnp.moveaxis(q.reshape(B, G, Hg, n_ch, SEL_QCHUNK, dh), 3, 0)
    idx_ch = jnp.moveaxis(sel_idx.reshape(B, G, n_ch, SEL_QCHUNK, top_k), 2, 0)
    offs = jnp.arange(SEL_BLOCK)
    g_i = jnp.arange(G)[None, :, None, None, None]
    h_i = jnp.arange(Hg)[None, None, :, None, None]
    gather = jax.vmap(jax.vmap(lambda src, ix: src[ix]))

    def sel_chunk(args):
        qc, ic, c0 = args
        t = c0 * SEL_QCHUNK + jnp.arange(SEL_QCHUNK)
        pos = (ic[..., None] * SEL_BLOCK + offs).reshape(B, G, SEL_QCHUNK, L)
        ks = gather(k_s, pos)
        vs = gather(v_s, pos)
        rel = t[None, None, :, None] - pos
        bias = tbl[g_i, h_i, t5_bucket(rel)[:, :, None]]
        s = jnp.einsum('bghcd,bgcld->bghcl', qc, ks) * scale + bias
        p = masked_softmax(s, (rel >= 0)[:, :, None])
        return jnp.einsum('bghcl,bgcld->bghcd', p.astype(vs.dtype), vs)

    o_sel = lax.map(sel_chunk, (q_ch, idx_ch, jnp.arange(n_ch)))
    o_sel = jnp.moveaxis(o_sel, 0, 3).reshape(B, G, Hg, S, dh)

    nqb = S // WIN_QBLOCK
    span = WINDOW + WIN_QBLOCK
    kpad = jnp.pad(k_w, ((0, 0), (0, 0), (WINDOW, 0), (0, 0)))
    vpad = jnp.pad(v_w, ((0, 0), (0, 0), (WINDOW, 0), (0, 0)))
    q_wb = jnp.moveaxis(q.reshape(B, G, Hg, nqb, WIN_QBLOCK, dh), 3, 0)
    a = jnp.arange(WIN_QBLOCK)[:, None]
    c = jnp.arange(span)[None, :]
    rel_w = WINDOW + a - c
    bias_w = tbl[:, :, t5_bucket(rel_w)]
    band = (rel_w >= 0) & (rel_w < WINDOW)

    def win_block(args):
        qb, i = args
        kb = lax.dynamic_slice_in_dim(kpad, i * WIN_QBLOCK, span, axis=2)
        vb = lax.dynamic_slice_in_dim(vpad, i * WIN_QBLOCK, span, axis=2)
        valid = band & (i * WIN_QBLOCK - WINDOW + c >= 0)
        s = jnp.einsum('bghqd,bgkd->bghqk', qb, kb) * scale + bias_w
        p = masked_softmax(s, valid)
        return jnp.einsum('bghqk,bgkd->bghqd', p.astype(vb.dtype), vb)

    o_win = lax.map(win_block, (q_wb, jnp.arange(nqb)))
    o_win = jnp.moveaxis(o_win, 0, 3).reshape(B, G, Hg, S, dh)

    o = gates[0] * o_cmp + gates[1] * o_sel + gates[2] * o_win
    return o.transpose(0, 3, 1, 2, 4).reshape(B, S, G * Hg * dh)


def rotate_half(x, cos, sin):
    x1, x2 = jnp.split(x, 2, axis=-1)
    return jnp.concatenate([x1 * cos - x2 * sin, x2 * cos + x1 * sin], axis=-1)


def retention(q, k, v):
    B, H, S, dk = q.shape
    dv = v.shape[-1]
    q = q.astype(jnp.float32)
    k = k.astype(jnp.float32)
    v = v.astype(jnp.float32)
    theta = 1.0 / (10000.0 ** jnp.linspace(0.0, 1.0, dk // 2, dtype=jnp.float32))
    ang = jnp.arange(S, dtype=jnp.float32)[:, None] * theta[None, :]
    cos, sin = jnp.cos(ang), jnp.sin(ang)
    q = rotate_half(q, cos, sin)
    k = rotate_half(k, cos, sin) * (dk ** -0.5)
    gamma = 1.0 - jnp.exp2(-5.0 - jnp.arange(H, dtype=jnp.float32))
    log_g = jnp.log(gamma)
    C = RET_CHUNK
    n = S // C
    j = jnp.arange(C, dtype=jnp.float32)
    diff = j[:, None] - j[None, :]
    dmat = jnp.where(diff >= 0, jnp.exp(jnp.maximum(diff, 0.0)[None] * log_g[:, None, None]), 0.0)
    xi = jnp.exp((j + 1.0)[None, :] * log_g[:, None])
    zeta = jnp.exp((C - 1.0 - j)[None, :] * log_g[:, None])
    g_chunk = jnp.exp(C * log_g)

    def to_chunks(t):
        return jnp.moveaxis(t.reshape(B, H, n, C, t.shape[-1]), 2, 0)

    def step(R, inp):
        qi, ki, vi = inp
        inner = jnp.einsum('bhid,bhmd->bhim', qi, ki) * dmat
        o = (jnp.einsum('bhim,bhme->bhie', inner, vi)
             + jnp.einsum('bhid,bhde->bhie', qi, R) * xi[:, :, None])
        R = R * g_chunk[:, None, None] + jnp.einsum('bhmd,bhme->bhde', ki * zeta[:, :, None], vi)
        return R, o

    R0 = jnp.zeros((B, H, dk, dv), jnp.float32)
    _, o = lax.scan(step, R0, (to_chunks(q), to_chunks(k), to_chunks(v)))
    o = jnp.moveaxis(o, 0, 2).reshape(B, H, S, dv)
    mu = jnp.mean(o, axis=-1, keepdims=True)
    var = jnp.mean(jnp.square(o - mu), axis=-1, keepdims=True)
    o = (o - mu) * lax.rsqrt(var + EPS)
    return o.transpose(0, 2, 1, 3).reshape(B, S, H * dv)


def memory_attention(mq, mk, mv):
    scale = mq.shape[-1] ** -0.5
    s = jnp.einsum('bshd,bmhd->bhsm', mq, mk).astype(jnp.float32) * scale
    p = jax.nn.softmax(s, axis=-1)
    o = jnp.einsum('bhsm,bmhd->bshd', p.astype(mv.dtype), mv)
    return o.reshape(mq.shape[0], mq.shape[1], -1)


def causal_dwconv(u, w, b):
    S = u.shape[1]
    up = jnp.pad(u, ((0, 0), (CONV_WIDTH - 1, 0), (0, 0)))
    y = b
    for kk in range(CONV_WIDTH):
        y = y + w[kk] * up[:, kk:kk + S]
    return y


def setup_inputs(seed: int = 0) -> dict:
    key = jax.random.key(seed)
    ks = jax.random.split(key, 26)
    f32 = jnp.float32

    def nrm(k, shape, scale):
        return jax.random.normal(k, shape, f32) * scale

    def gain(k):
        return 1.0 + 0.05 * jax.random.normal(k, (DEPTH, D_MODEL), f32)

    flat_cmp = CMP_BLOCK * HEAD_DIM
    return {
        'x': nrm(ks[0], (BATCH, SEQ, D_MODEL), 1.0),
        'mem': nrm(ks[1], (BATCH, MEM_LEN, D_MODEL), 1.0),
        'w_in': nrm(ks[2], (DEPTH, D_MODEL, IN_COLS), D_MODEL ** -0.5),
        'cmp_pe_k': nrm(ks[3], (DEPTH, CMP_BLOCK, HEAD_DIM), 0.5),
        'cmp_w1_k': nrm(ks[4], (DEPTH, flat_cmp, CMP_HIDDEN), flat_cmp ** -0.5),
        'cmp_w2_k': nrm(ks[5], (DEPTH, CMP_HIDDEN, HEAD_DIM), CMP_HIDDEN ** -0.5),
        'cmp_pe_v': nrm(ks[6], (DEPTH, CMP_BLOCK, HEAD_DIM), 0.5),
        'cmp_w1_v': nrm(ks[7], (DEPTH, flat_cmp, CMP_HIDDEN), flat_cmp ** -0.5),
        'cmp_w2_v': nrm(ks[8], (DEPTH, CMP_HIDDEN, HEAD_DIM), CMP_HIDDEN ** -0.5),
        'rel_bias': nrm(ks[9], (REL_BUCKETS, NSA_HEADS), 0.5),
        'w_mem_kv': nrm(ks[10], (DEPTH, D_MODEL, 2 * MEM_Q), D_MODEL ** -0.5),
        'w_br_nsa': nrm(ks[11], (DEPTH, NSA_Q, D_MODEL), NSA_Q ** -0.5),
        'w_br_ret': nrm(ks[12], (DEPTH, RET_V, D_MODEL), RET_V ** -0.5),
        'w_br_mem': nrm(ks[13], (DEPTH, MEM_Q, D_MODEL), MEM_Q ** -0.5),
        'w_o': nrm(ks[14], (DEPTH, D_MODEL, D_MODEL), D_MODEL ** -0.5),
        'w_up': nrm(ks[15], (DEPTH, D_MODEL, 2 * D_FF), D_MODEL ** -0.5),
        'conv_w': nrm(ks[16], (DEPTH, CONV_WIDTH, 2 * D_FF), CONV_WIDTH ** -0.5),
        'conv_b': nrm(ks[17], (DEPTH, 2 * D_FF), 0.01),
        'w_down': nrm(ks[18], (DEPTH, D_FF, D_MODEL), D_FF ** -0.5),
        'g_pre_mix': gain(ks[19]),
        'g_post_mix': gain(ks[20]),
        'g_mem': gain(ks[21]),
        'g_pre_ffn': gain(ks[22]),
        'g_post_ffn': gain(ks[23]),
    }


def reference(x, mem, w_in, cmp_pe_k, cmp_w1_k, cmp_w2_k, cmp_pe_v, cmp_w1_v, cmp_w2_v,
              rel_bias, w_mem_kv, w_br_nsa, w_br_ret, w_br_mem, w_o, w_up, conv_w, conv_b,
              w_down, g_pre_mix, g_post_mix, g_mem, g_pre_ffn, g_post_ffn):
    B, S, D = x.shape
    M = mem.shape[1]
    G, Hg, dh = NSA_KV_GROUPS, NSA_HPG, HEAD_DIM
    pts = np.cumsum(np.array(SPLITS))[:-1].tolist()
    tbl = rel_bias.T.reshape(G, Hg, REL_BUCKETS)
    for l in range(DEPTH):
        h = rmsnorm(x, g_pre_mix[l])
        proj = h @ w_in[l]
        nq, nkv, ngate, rq, rk, rv, rg, mq, bg = jnp.split(proj, pts, axis=-1)

        q_a = nq.reshape(B, S, G, Hg, dh).transpose(0, 2, 3, 1, 4)
        kv_a = nkv.reshape(B, S, 3, 2, G, dh).transpose(2, 3, 0, 4, 1, 5)
        gate_a = jax.nn.sigmoid(ngate.reshape(B, S, 3, G, Hg)).transpose(2, 0, 3, 4, 1)[..., None]
        y_a = nsa_attention(q_a, kv_a, gate_a.astype(x.dtype), tbl,
                            cmp_pe_k[l], cmp_w1_k[l], cmp_w2_k[l],
                            cmp_pe_v[l], cmp_w1_v[l], cmp_w2_v[l])

        q_r = rq.reshape(B, S, RET_HEADS, RET_QK_DIM).transpose(0, 2, 1, 3)
        k_r = rk.reshape(B, S, RET_HEADS, RET_QK_DIM).transpose(0, 2, 1, 3)
        v_r = rv.reshape(B, S, RET_HEADS, RET_V_DIM).transpose(0, 2, 1, 3)
        y_b = (retention(q_r, k_r, v_r).astype(x.dtype) * jax.nn.silu(rg))

        mem_n = rmsnorm(mem, g_mem[l])
        mkv = (mem_n @ w_mem_kv[l]).reshape(B, M, 2, MEM_HEADS, MEM_HEAD_DIM)
        y_c = memory_attention(mq.reshape(B, S, MEM_HEADS, MEM_HEAD_DIM),
                               mkv[:, :, 0], mkv[:, :, 1])

        bgate = jax.nn.sigmoid(bg.reshape(B, S, N_BRANCH, D))
        merged = (bgate[:, :, 0] * (y_a @ w_br_nsa[l])
                  + bgate[:, :, 1] * (y_b @ w_br_ret[l])
                  + bgate[:, :, 2] * (y_c @ w_br_mem[l]))
        x = x + rmsnorm(merged @ w_o[l], g_post_mix[l])

        h2 = rmsnorm(x, g_pre_ffn[l])
        u = causal_dwconv(h2 @ w_up[l], conv_w[l], conv_b[l])
        u_gate, u_val = jnp.split(u, 2, axis=-1)
        f = (jax.nn.gelu(u_gate, approximate=True) * u_val) @ w_down[l]
        x = x + rmsnorm(f, g_post_ffn[l])
    return x
```

```python
import functools
import math

import jax
import jax.numpy as jnp
import numpy as np
from jax import lax
from jax.experimental import pallas as pl
from jax.experimental.pallas import tpu as pltpu

F32 = jnp.float32
BF16 = jnp.bfloat16

LANES = 128
EPS = 1e-6
NEG = -1e30

NSA_HEADS = 8
NSA_GROUPS = 2
NSA_HPG = NSA_HEADS // NSA_GROUPS
HEAD_DIM = 128
CMP_BLOCK = 32
CMP_STRIDE = 16
CMP_HIDDEN = 256
SEL_BLOCK = 64
SEL_TOPK = 16
SEL_LOCAL = 2
FORCE_SCORE = 1e4
WINDOW = 512
RET_HEADS = 4
RET_QK_DIM = 128
RET_V_DIM = 256
RET_CHUNK = 128
MEM_HEADS = 4
MEM_HEAD_DIM = 256
REL_BUCKETS = 32
REL_MAX_DIST = 128
CONV_WIDTH = 3

QT = 128
HALO = 16

VMEM_LIMIT = 56 * 1024 * 1024


def _cparams(sem, vmem=VMEM_LIMIT):
    return pltpu.CompilerParams(dimension_semantics=sem, vmem_limit_bytes=vmem)


def _dot(a, b):
    return jnp.dot(a, b, preferred_element_type=F32)


def _dot_nt(a, b):
    return lax.dot_general(a, b, (((1,), (1,)), ((), ())), preferred_element_type=F32)


def _rms(x, g):
    ms = jnp.mean(x * x, axis=-1, keepdims=True)
    return x * lax.rsqrt(ms + EPS) * g


def _norm_proj_kernel(x_ref, g_ref, w_ref, o_ref, h_ref, *, head_major):
    @pl.when(pl.program_id(1) == 0)
    def _():
        h_ref[...] = _rms(x_ref[...], g_ref[...]).astype(BF16)

    acc = _dot(h_ref[...], w_ref[...])
    if head_major:
        for c in range(acc.shape[1] // LANES):
            o_ref[c] = acc[:, c * LANES:(c + 1) * LANES].astype(o_ref.dtype)
    else:
        o_ref[...] = acc.astype(o_ref.dtype)


def _norm_proj(x2d, g, w, *, head_major, tm, tn, name):
    m, d = x2d.shape
    n = w.shape[1]
    grid = (m // tm, n // tn)
    if head_major:
        out_shape = jax.ShapeDtypeStruct((n // LANES, m, LANES), BF16)
        out_spec = pl.BlockSpec((tn // LANES, tm, LANES), lambda i, j: (j, i, 0))
    else:
        out_shape = jax.ShapeDtypeStruct((m, n), BF16)
        out_spec = pl.BlockSpec((tm, tn), lambda i, j: (i, j))
    return pl.pallas_call(
        functools.partial(_norm_proj_kernel, head_major=head_major),
        out_shape=out_shape,
        grid=grid,
        in_specs=[
            pl.BlockSpec((tm, d), lambda i, j: (i, 0)),
            pl.BlockSpec((1, d), lambda i, j: (0, 0)),
            pl.BlockSpec((d, tn), lambda i, j: (0, j)),
        ],
        out_specs=out_spec,
        scratch_shapes=[pltpu.VMEM((tm, d), BF16)],
        compiler_params=_cparams(("parallel", "arbitrary")),
        name=name,
    )(x2d, g, w)


def _compress_kernel(x_ref, pe_ref, w1_ref, w2_ref, o_ref):
    half = CMP_STRIDE * HEAD_DIM
    x = x_ref[0, 0].astype(F32)
    pe = pe_ref[0]
    a = _dot((x + pe[0:1]).astype(BF16), w1_ref[0, :half, :])
    b = _dot((x + pe[1:2]).astype(BF16), w1_ref[0, half:, :])
    rows = x.shape[0]
    hid = a + pltpu.roll(b, rows - 1, 0)
    hid = hid * (1.0 / (1.0 + jnp.exp(-hid)))
    o_ref[0, 0] = _dot(hid.astype(BF16), w2_ref[0]).astype(o_ref.dtype)


def _compress(proj_rows, kv_chunk0, pe, w1, w2, batch):
    _, _, rows, width = proj_rows.shape
    n_out = 2 * NSA_GROUPS
    return pl.pallas_call(
        _compress_kernel,
        out_shape=jax.ShapeDtypeStruct((batch, n_out, rows, HEAD_DIM), BF16),
        grid=(batch, n_out),
        in_specs=[
            pl.BlockSpec((1, 1, rows, width), lambda b, c: (kv_chunk0 + c, b, 0, 0)),
            pl.BlockSpec((1, 2, width), lambda b, c: (c // NSA_GROUPS, 0, 0)),
            pl.BlockSpec((1, 2 * width, CMP_HIDDEN), lambda b, c: (c // NSA_GROUPS, 0, 0)),
            pl.BlockSpec((1, CMP_HIDDEN, HEAD_DIM), lambda b, c: (c // NSA_GROUPS, 0, 0)),
        ],
        out_specs=pl.BlockSpec((1, 1, rows, HEAD_DIM), lambda b, c: (b, c, 0, 0)),
        compiler_params=_cparams(("parallel", "parallel")),
        name="nsa_compress",
    )(proj_rows, pe, w1, w2)


def _bias_from_rel(rel, tbl_ref, head):
    max_exact = REL_BUCKETS // 2
    n = jnp.maximum(rel, 0)
    nf = jnp.maximum(n, 1).astype(F32)
    large = max_exact + (jnp.log(nf / max_exact) / math.log(REL_MAX_DIST / max_exact)
                         * (REL_BUCKETS - max_exact)).astype(jnp.int32)
    large = jnp.minimum(large, REL_BUCKETS - 1)
    bucket = jnp.where(n < max_exact, n, large)
    out = jnp.zeros(rel.shape, F32)
    for b in range(REL_BUCKETS):
        out = jnp.where(bucket == b, tbl_ref[b * NSA_HEADS + head], out)
    return out


def _cmp_bias_kernel(tbl_ref, o_ref):
    head, blk = pl.program_id(0), pl.program_id(1)
    rows, cols = o_ref.shape[1], o_ref.shape[2]
    t = blk * rows + lax.broadcasted_iota(jnp.int32, (rows, cols), 0)
    n = lax.broadcasted_iota(jnp.int32, (rows, cols), 1)
    o_ref[0] = _bias_from_rel(t - (n * CMP_STRIDE + CMP_BLOCK - 1), tbl_ref, head)


def _band_bias_kernel(tbl_ref, o_ref):
    head = pl.program_id(0)
    rows, cols = o_ref.shape[1], o_ref.shape[2]
    i = lax.broadcasted_iota(jnp.int32, (rows, cols), 0)
    c = lax.broadcasted_iota(jnp.int32, (rows, cols), 1)
    o_ref[0] = _bias_from_rel(i - c + QT, tbl_ref, head)


def _bias_tables(tbl_flat, seq):
    rows = 256
    smem = pl.BlockSpec(memory_space=pltpu.SMEM)
    cmp_bias = pl.pallas_call(
        _cmp_bias_kernel,
        out_shape=jax.ShapeDtypeStruct((NSA_HEADS, seq, LANES), F32),
        grid=(NSA_HEADS, seq // rows),
        in_specs=[smem],
        out_specs=pl.BlockSpec((1, rows, LANES), lambda h, r: (h, r, 0)),
        compiler_params=_cparams(("parallel", "parallel")),
        name="cmp_bias",
    )(tbl_flat)
    band = pl.pallas_call(
        _band_bias_kernel,
        out_shape=jax.ShapeDtypeStruct((NSA_HEADS, QT, 2 * QT), F32),
        grid=(NSA_HEADS,),
        in_specs=[smem],
        out_specs=pl.BlockSpec((1, QT, 2 * QT), lambda h: (h, 0, 0)),
        compiler_params=_cparams(("parallel",)),
        name="band_bias",
    )(tbl_flat)
    return cmp_bias, band


def _split3(x):
    hi = x.astype(BF16)
    r1 = x - hi.astype(F32)
    mid = r1.astype(BF16)
    lo = (r1 - mid.astype(F32)).astype(BF16)
    return hi, mid, lo


def _nsa_kernel(tbl_ref, q_ref, kc_ref, vc_ref, ks_ref, vs_ref, kw_ref, vw_ref, gate_ref,
                cbias_ref, band_ref, o_ref, m_ref, l_ref, acc_ref, mask_ref):
    grp, qi = pl.program_id(1), pl.program_id(2)
    hpg = NSA_HPG
    scale = HEAD_DIM ** -0.5
    n_sel = mask_ref.shape[0] * QT // SEL_BLOCK
    q = q_ref[...].reshape(hpg * QT, HEAD_DIM)
    row = lax.broadcasted_iota(jnp.int32, (QT, QT), 0)
    col = lax.broadcasted_iota(jnp.int32, (QT, QT), 1)

    t_abs = qi * QT + row
    valid_c = (t_abs - (col * CMP_STRIDE + CMP_BLOCK - 1)) >= 0
    s = _dot_nt(q, kc_ref[0, 0]).reshape(hpg, QT, QT) * scale + cbias_ref[...]
    s = jnp.where(valid_c[None], s, NEG)
    mx = jnp.max(s, axis=-1, keepdims=True)
    p = jnp.where(valid_c[None], jnp.exp(s - mx), 0.0)
    p = p / jnp.maximum(jnp.sum(p, axis=-1, keepdims=True), 1e-30)
    o_cmp = _dot(p.reshape(hpg * QT, QT).astype(BF16), vc_ref[0, 0]).reshape(hpg, QT, HEAD_DIM)

    p_sum = p[0]
    for h in range(1, hpg):
        p_sum = p_sum + p[h]
    jb = lax.broadcasted_iota(jnp.int32, (n_sel, QT), 0)
    nn = lax.broadcasted_iota(jnp.int32, (n_sel, QT), 1)
    n_cmp = (mask_ref.shape[0] * QT - CMP_BLOCK) // CMP_STRIDE + 1
    overlap_t = jnp.where((nn * CMP_STRIDE < jb * SEL_BLOCK + SEL_BLOCK)
                          & (nn * CMP_STRIDE + CMP_BLOCK > jb * SEL_BLOCK)
                          & (nn < n_cmp), 1.0, 0.0).astype(BF16)
    hi, mid, lo = _split3(p_sum)
    imp = _dot_nt(overlap_t, hi) + _dot_nt(overlap_t, mid) + _dot_nt(overlap_t, lo)
    cur = (qi * QT + nn) // SEL_BLOCK
    forced = (jb == 0) | ((cur - jb >= 0) & (cur - jb < SEL_LOCAL))
    imp = jnp.where(forced, FORCE_SCORE, imp)
    imp = jnp.where(jb <= cur, imp, -1.0)
    rank = jnp.zeros((n_sel, QT), F32)
    for i in range(n_sel):
        other = jnp.broadcast_to(imp[i:i + 1, :], (n_sel, QT))
        rank = rank + jnp.where(jb > i, jnp.where(other >= imp, 1.0, 0.0), jnp.where(other > imp, 1.0, 0.0))
    sel_t = jnp.where(rank < min(SEL_TOPK, n_sel), 1.0, 0.0).astype(BF16)
    eye = jnp.where(row == col, 1.0, 0.0).astype(BF16)
    sel = _dot_nt(eye, sel_t).astype(BF16)
    blk = lax.broadcasted_iota(jnp.int32, (n_sel, QT), 0)
    key_in_tile = lax.broadcasted_iota(jnp.int32, (n_sel, QT), 1)
    for k in range(mask_ref.shape[0]):
        expand = jnp.where(blk == (k * QT + key_in_tile) // SEL_BLOCK, 1.0, 0.0).astype(BF16)
        mask_ref[k] = _dot(sel, expand)

    def tile_bias(dist):
        near = jnp.where(dist == 0, band_ref[:, :, QT:], band_ref[:, :, :QT])
        far = [jnp.full((1, QT, QT), tbl_ref[(REL_BUCKETS - 1) * NSA_HEADS + grp * hpg + h], F32)
               for h in range(hpg)]
        return jnp.where(dist <= 1, near, jnp.concatenate(far, axis=0))

    def flash_init():
        m_ref[...] = jnp.full(m_ref.shape, NEG, F32)
        l_ref[...] = jnp.zeros(l_ref.shape, F32)
        acc_ref[...] = jnp.zeros(acc_ref.shape, F32)

    def flash_step(k_tile, v_tile, valid, dist):
        s = _dot_nt(q, k_tile).reshape(hpg, QT, QT) * scale + tile_bias(dist)
        s = jnp.where(valid[None], s, NEG)
        m_old = m_ref[...]
        m_new = jnp.maximum(m_old, jnp.max(s, axis=-1, keepdims=True))
        alpha = jnp.exp(m_old - m_new)
        pt = jnp.where(valid[None], jnp.exp(s - m_new), 0.0)
        l_ref[...] = alpha * l_ref[...] + jnp.sum(pt, axis=-1, keepdims=True)
        pv = _dot(pt.reshape(hpg * QT, QT).astype(BF16), v_tile).reshape(hpg, QT, HEAD_DIM)
        acc_ref[...] = alpha * acc_ref[...] + pv
        m_ref[...] = m_new

    def flash_out():
        return acc_ref[...] / jnp.maximum(l_ref[...], 1e-30)

    rel0 = row - col

    flash_init()

    def sel_body(k, carry):
        start = pl.multiple_of(k * QT, QT)
        dist = qi - k
        valid = (mask_ref[k] > 0.5) & (rel0 + dist * QT >= 0)
        flash_step(ks_ref[0, pl.ds(start, QT), :], vs_ref[0, pl.ds(start, QT), :], valid, dist)
        return carry

    lax.fori_loop(0, qi + 1, sel_body, 0)
    o_sel = flash_out()

    flash_init()

    def win_body(k, carry):
        start = pl.multiple_of(k * QT, QT)
        dist = qi - k
        rel = rel0 + dist * QT
        valid = (rel >= 0) & (rel < WINDOW)
        flash_step(kw_ref[0, pl.ds(start, QT), :], vw_ref[0, pl.ds(start, QT), :], valid, dist)
        return carry

    lax.fori_loop(jnp.maximum(qi - WINDOW // QT, 0), qi + 1, win_body, 0)
    o_win = flash_out()

    gate = gate_ref[...].astype(F32)
    gate = 1.0 / (1.0 + jnp.exp(-gate))
    for h in range(hpg):
        o = (gate[:, h:h + 1] * o_cmp[h]
             + gate[:, hpg + h:hpg + h + 1] * o_sel[h]
             + gate[:, 2 * hpg + h:2 * hpg + h + 1] * o_win[h])
        o_ref[:, h * HEAD_DIM:(h + 1) * HEAD_DIM] = o.astype(o_ref.dtype)


def _nsa_attention(tbl_flat, proj, cmp_kv, gates, cmp_bias, band, *, batch, seq, kv_chunk0, gate_blk0):
    nq = seq // QT
    hpg = NSA_HPG
    grps = NSA_GROUPS

    def kv_spec(branch, kv):
        base = kv_chunk0 + (branch * 2 + kv) * grps
        return pl.BlockSpec((1, seq, HEAD_DIM), lambda b, g, i: (base + g, b, 0))

    return pl.pallas_call(
        _nsa_kernel,
        out_shape=jax.ShapeDtypeStruct((batch * seq, NSA_HEADS * HEAD_DIM), BF16),
        grid=(batch, grps, nq),
        in_specs=[
            pl.BlockSpec(memory_space=pltpu.SMEM),
            pl.BlockSpec((hpg, QT, HEAD_DIM), lambda b, g, i: (g, b * nq + i, 0)),
            pl.BlockSpec((1, 1, QT, HEAD_DIM), lambda b, g, i: (b, g, 0, 0)),
            pl.BlockSpec((1, 1, QT, HEAD_DIM), lambda b, g, i: (b, grps + g, 0, 0)),
            kv_spec(1, 0), kv_spec(1, 1), kv_spec(2, 0), kv_spec(2, 1),
            pl.BlockSpec((QT, LANES), lambda b, g, i: (b * nq + i, gate_blk0 + g)),
            pl.BlockSpec((hpg, QT, LANES), lambda b, g, i: (g, i, 0)),
            pl.BlockSpec((hpg, QT, 2 * QT), lambda b, g, i: (g, 0, 0)),
        ],
        out_specs=pl.BlockSpec((QT, hpg * HEAD_DIM), lambda b, g, i: (b * nq + i, g)),
        scratch_shapes=[
            pltpu.VMEM((hpg, QT, 1), F32),
            pltpu.VMEM((hpg, QT, 1), F32),
            pltpu.VMEM((hpg, QT, HEAD_DIM), F32),
            pltpu.VMEM((nq, QT, QT), F32),
        ],
        compiler_params=_cparams(("parallel", "parallel", "arbitrary")),
        name="nsa_attention",
    )(tbl_flat, proj, cmp_kv, cmp_kv, proj, proj, proj, proj, gates, cmp_bias, band)


def _retention_kernel(q_ref, k_ref, v_ref, g_ref, cos_ref, sin_ref, dmat_ref, xi_ref, zeta_ref,
                      o_ref, state_ref, *, g_chunk):
    @pl.when(pl.program_id(1) == 0)
    def _():
        state_ref[...] = jnp.zeros(state_ref.shape, F32)

    cos, sin = cos_ref[...], sin_ref[...]
    half = RET_QK_DIM // 2
    for h in range(RET_HEADS):
        q = q_ref[h].astype(F32)
        k = k_ref[h].astype(F32)
        qr = q * cos + pltpu.roll(q, half, 1) * sin
        kr = (k * cos + pltpu.roll(k, half, 1) * sin) * (RET_QK_DIM ** -0.5)
        v = jnp.concatenate([v_ref[2 * h], v_ref[2 * h + 1]], axis=1)
        state = state_ref[h]
        qb = qr.astype(BF16)
        inner = _dot_nt(qb, kr.astype(BF16)) * dmat_ref[h]
        o = _dot(inner.astype(BF16), v) + _dot(qb, state.astype(BF16)) * xi_ref[h][:, 0:1]
        kz = (kr * zeta_ref[h]).T.astype(BF16)
        state_ref[h] = state * g_chunk[h] + _dot(kz, v)
        mu = jnp.mean(o, axis=-1, keepdims=True)
        var = jnp.mean(jnp.square(o - mu), axis=-1, keepdims=True)
        on = (o - mu) * lax.rsqrt(var + EPS)
        gate = g_ref[:, h * RET_V_DIM:(h + 1) * RET_V_DIM].astype(F32)
        gate = gate * (1.0 / (1.0 + jnp.exp(-gate)))
        o_ref[:, h * RET_V_DIM:(h + 1) * RET_V_DIM] = (on * gate).astype(o_ref.dtype)


def _retention(proj, gates, *, batch, seq, q_blk, k_blk, v_blk):
    c = RET_CHUNK
    n = seq // c
    heads = RET_HEADS
    dk = RET_QK_DIM
    theta = 1.0 / (10000.0 ** np.linspace(0.0, 1.0, dk // 2))
    ang = np.arange(seq)[:, None] * theta[None, :]
    cos = np.concatenate([np.cos(ang), np.cos(ang)], axis=1)
    sin = np.concatenate([-np.sin(ang), np.sin(ang)], axis=1)
    log_g = np.log(1.0 - np.exp2(-5.0 - np.arange(heads)))
    j = np.arange(c)
    diff = j[:, None] - j[None, :]
    dmat = np.where(diff >= 0, np.exp(np.maximum(diff, 0)[None] * log_g[:, None, None]), 0.0)
    xi = np.broadcast_to(np.exp((j + 1.0)[None, :] * log_g[:, None])[:, :, None], (heads, c, LANES))
    zeta = np.broadcast_to(np.exp((c - 1.0 - j)[None, :] * log_g[:, None])[:, :, None], (heads, c, LANES))
    g_chunk = tuple(float(v) for v in np.exp(c * log_g))
    consts = [jnp.asarray(a, F32) for a in (cos, sin, dmat, xi, zeta)]

    full3 = lambda shape: pl.BlockSpec(shape, lambda b, i: (0, 0, 0))
    return pl.pallas_call(
        functools.partial(_retention_kernel, g_chunk=g_chunk),
        out_shape=jax.ShapeDtypeStruct((batch * seq, heads * RET_V_DIM), BF16),
        grid=(batch, n),
        in_specs=[
            pl.BlockSpec((heads, c, LANES), lambda b, i: (q_blk, b * n + i, 0)),
            pl.BlockSpec((heads, c, LANES), lambda b, i: (k_blk, b * n + i, 0)),
            pl.BlockSpec((2 * heads, c, LANES), lambda b, i: (v_blk, b * n + i, 0)),
            pl.BlockSpec((c, heads * RET_V_DIM), lambda b, i: (b * n + i, 0)),
            pl.BlockSpec((c, dk), lambda b, i: (i, 0)),
            pl.BlockSpec((c, dk), lambda b, i: (i, 0)),
            full3((heads, c, c)), full3((heads, c, LANES)), full3((heads, c, LANES)),
        ],
        out_specs=pl.BlockSpec((c, heads * RET_V_DIM), lambda b, i: (b * n + i, 0)),
        scratch_shapes=[pltpu.VMEM((heads, dk, RET_V_DIM), F32)],
        compiler_params=_cparams(("parallel", "arbitrary")),
        name="retention",
    )(proj, proj, proj, gates, *consts)


def _mem_attn_kernel(q_ref, k_ref, v_ref, o_ref):
    scale = MEM_HEAD_DIM ** -0.5
    for h in range(MEM_HEADS):
        s = (_dot_nt(q_ref[2 * h], k_ref[2 * h]) + _dot_nt(q_ref[2 * h + 1], k_ref[2 * h + 1])) * scale
        mx = jnp.max(s, axis=-1, keepdims=True)
        p = jnp.exp(s - mx)
        p = (p / jnp.sum(p, axis=-1, keepdims=True)).astype(BF16)
        for half in range(2):
            c0 = h * MEM_HEAD_DIM + half * LANES
            o_ref[:, c0:c0 + LANES] = _dot(p, v_ref[2 * h + half]).astype(o_ref.dtype)


def _mem_attention(proj, mkv, *, batch, seq, mem_len, q_blk, tq):
    nq = seq // tq
    chunks = MEM_HEADS * MEM_HEAD_DIM // LANES
    return pl.pallas_call(
        _mem_attn_kernel,
        out_shape=jax.ShapeDtypeStruct((batch * seq, MEM_HEADS * MEM_HEAD_DIM), BF16),
        grid=(batch, nq),
        in_specs=[
            pl.BlockSpec((chunks, tq, LANES), lambda b, i: (q_blk, b * nq + i, 0)),
            pl.BlockSpec((chunks, mem_len, LANES), lambda b, i: (0, b, 0)),
            pl.BlockSpec((chunks, mem_len, LANES), lambda b, i: (1, b, 0)),
        ],
        out_specs=pl.BlockSpec((tq, MEM_HEADS * MEM_HEAD_DIM), lambda b, i: (b * nq + i, 0)),
        compiler_params=_cparams(("parallel", "parallel")),
        name="mem_attention",
    )(proj, mkv, mkv)


def _sigmoid(x):
    return 1.0 / (1.0 + jnp.exp(-x))


def _merge_kernel(ya_ref, yb_ref, yc_ref, wa_ref, wb_ref, wc_ref, ga_ref, gb_ref, gc_ref, o_ref):
    out = _sigmoid(ga_ref[...].astype(F32)) * _dot(ya_ref[...], wa_ref[...])
    out = out + _sigmoid(gb_ref[...].astype(F32)) * _dot(yb_ref[...], wb_ref[...])
    out = out + _sigmoid(gc_ref[...].astype(F32)) * _dot(yc_ref[...], wc_ref[...])
    o_ref[...] = out.astype(o_ref.dtype)


def _merge(ya, yb, yc, wa, wb, wc, gates, *, d_model, gate_col0, tm, tn):
    m, kdim = ya.shape
    y_spec = pl.BlockSpec((tm, kdim), lambda i, j: (i, 0))
    w_spec = pl.BlockSpec((kdim, tn), lambda i, j: (0, j))

    def g_spec(branch):
        base = (gate_col0 + branch * d_model) // tn
        return pl.BlockSpec((tm, tn), lambda i, j: (i, base + j))

    return pl.pallas_call(
        _merge_kernel,
        out_shape=jax.ShapeDtypeStruct((m, d_model), BF16),
        grid=(m // tm, d_model // tn),
        in_specs=[y_spec, y_spec, y_spec, w_spec, w_spec, w_spec, g_spec(0), g_spec(1), g_spec(2)],
        out_specs=pl.BlockSpec((tm, tn), lambda i, j: (i, j)),
        compiler_params=_cparams(("parallel", "parallel")),
        name="branch_merge",
    )(ya, yb, yc, wa, wb, wc, gates, gates, gates)


def _out_proj_kernel(y_ref, w_ref, g_ref, x_ref, o_ref):
    o_ref[...] = x_ref[...] + _rms(_dot(y_ref[...], w_ref[...]), g_ref[...])


def _out_proj(y, w, g, x2d, *, tm):
    m, d = x2d.shape
    return pl.pallas_call(
        _out_proj_kernel,
        out_shape=jax.ShapeDtypeStruct((m, d), F32),
        grid=(m // tm,),
        in_specs=[
            pl.BlockSpec((tm, d), lambda i: (i, 0)),
            pl.BlockSpec((d, d), lambda i: (0, 0)),
            pl.BlockSpec((1, d), lambda i: (0, 0)),
            pl.BlockSpec((tm, d), lambda i: (i, 0)),
        ],
        out_specs=pl.BlockSpec((tm, d), lambda i: (i, 0)),
        compiler_params=_cparams(("parallel",)),
        name="out_proj",
    )(y, w, g, x2d)


def _gelu_tanh(x):
    return 0.5 * x * (1.0 + jnp.tanh(math.sqrt(2.0 / math.pi) * (x + 0.044715 * (x * x * x))))


def _ffn_kernel(x_ref, halo_ref, gpre_ref, wg_ref, wv_ref, cwg_ref, cwv_ref, cbg_ref, cbv_ref,
                wd_ref, gpost_ref, o_ref, h_ref, acc_ref, *, seq):
    i, j = pl.program_id(0), pl.program_id(1)
    tm = x_ref.shape[0]

    @pl.when(j == 0)
    def _():
        h_ref[HALO:, :] = _rms(x_ref[...], gpre_ref[...]).astype(BF16)
        keep = jnp.where((i * tm) % seq == 0, 0.0, 1.0)
        h_ref[:HALO, :] = (_rms(halo_ref[...], gpre_ref[...]) * keep).astype(BF16)
        acc_ref[...] = jnp.zeros(acc_ref.shape, F32)

    h = h_ref[...]

    def conv(u, w_ref, b_ref):
        w = w_ref[...]
        y = b_ref[...] + w[2:3] * u[HALO:]
        y = y + w[1:2] * pltpu.roll(u, 1, 0)[HALO:]
        return y + w[0:1] * pltpu.roll(u, 2, 0)[HALO:]

    yg = conv(_dot(h, wg_ref[...]), cwg_ref, cbg_ref)
    yv = conv(_dot(h, wv_ref[...]), cwv_ref, cbv_ref)
    act = (_gelu_tanh(yg) * yv).astype(BF16)
    acc_ref[...] += _dot(act, wd_ref[...])

    @pl.when(j == pl.num_programs(1) - 1)
    def _():
        o_ref[...] = x_ref[...] + _rms(acc_ref[...], gpost_ref[...])


def _ffn(x2d, gpre, w_up, conv_w, conv_b, w_down, gpost, *, seq, tm, tn):
    m, d = x2d.shape
    d_ff = w_down.shape[0]
    nj = d_ff // tn
    halo_blocks = tm // HALO
    return pl.pallas_call(
        functools.partial(_ffn_kernel, seq=seq),
        out_shape=jax.ShapeDtypeStruct((m, d), F32),
        grid=(m // tm, nj),
        in_specs=[
            pl.BlockSpec((tm, d), lambda i, j: (i, 0)),
            pl.BlockSpec((HALO, d), lambda i, j: (jnp.maximum(i * halo_blocks - 1, 0), 0)),
            pl.BlockSpec((1, d), lambda i, j: (0, 0)),
            pl.BlockSpec((d, tn), lambda i, j: (0, j)),
            pl.BlockSpec((d, tn), lambda i, j: (0, nj + j)),
            pl.BlockSpec((CONV_WIDTH, tn), lambda i, j: (0, j)),
            pl.BlockSpec((CONV_WIDTH, tn), lambda i, j: (0, nj + j)),
            pl.BlockSpec((1, tn), lambda i, j: (0, j)),
            pl.BlockSpec((1, tn), lambda i, j: (0, nj + j)),
            pl.BlockSpec((tn, d), lambda i, j: (j, 0)),
            pl.BlockSpec((1, d), lambda i, j: (0, 0)),
        ],
        out_specs=pl.BlockSpec((tm, d), lambda i, j: (i, 0)),
        scratch_shapes=[pltpu.VMEM((tm + HALO, d), BF16), pltpu.VMEM((tm, d), F32)],
        compiler_params=_cparams(("parallel", "arbitrary")),
        name="conv_ffn",
    )(x2d, x2d, gpre, w_up, w_up, conv_w, conv_w, conv_b, conv_b, w_down, gpost)


def _pick(n, prefs):
    for t in prefs:
        if n % t == 0:
            return t
    raise ValueError(f"no tile in {prefs} divides {n}")


def kernel(x, mem, w_in, cmp_pe_k, cmp_w1_k, cmp_w2_k, cmp_pe_v, cmp_w1_v, cmp_w2_v, rel_bias, w_mem_kv,
           w_br_nsa, w_br_ret, w_br_mem, w_o, w_up, conv_w, conv_b, w_down, g_pre_mix, g_post_mix, g_mem,
           g_pre_ffn, g_post_ffn):
    batch, seq, d_model = x.shape
    mem_len = mem.shape[1]
    depth = w_in.shape[0]
    m = batch * seq
    grps, hpg, dh = NSA_GROUPS, NSA_HPG, HEAD_DIM

    nsa_q = NSA_HEADS * dh
    nsa_kv = 3 * 2 * grps * dh
    nsa_gates = 3 * NSA_HEADS
    ret_qk = RET_HEADS * RET_QK_DIM
    ret_v = RET_HEADS * RET_V_DIM
    mem_q = MEM_HEADS * MEM_HEAD_DIM
    splits = (nsa_q, nsa_kv, nsa_gates, ret_qk, ret_qk, ret_v, ret_v, mem_q, 3 * d_model)
    offs = np.concatenate([[0], np.cumsum(splits)])
    o_nq, o_nkv, o_ng, o_rq, o_rk, o_rv, o_rg, o_mq, o_bg = (int(v) for v in offs[:-1])

    q_chunk0 = 0
    rv_chunk0 = q_chunk0 + nsa_q // LANES
    mq_chunk0 = rv_chunk0 + ret_v // LANES
    rq_chunk0 = mq_chunk0 + mem_q // LANES
    rk_chunk0 = rq_chunk0 + ret_qk // LANES
    kv_chunk0 = rk_chunk0 + ret_qk // LANES
    bg_col0 = ret_v
    ng_col0 = bg_col0 + 3 * d_model
    gate_cols = -(-(ng_col0 + grps * LANES) // 512) * 512

    tbl_flat = rel_bias.reshape(-1)
    cmp_bias, band = _bias_tables(tbl_flat, seq)
    x2d = x.reshape(m, d_model)

    for l in range(depth):
        w = w_in[l]
        cols = lambda o, n: w[:, o:o + n]
        w_attn = jnp.concatenate([cols(o_nq, nsa_q), cols(o_rv, ret_v), cols(o_mq, mem_q), cols(o_rq, ret_qk),
                                  cols(o_rk, ret_qk), cols(o_nkv, nsa_kv)], axis=1).astype(BF16)
        ng = cols(o_ng, nsa_gates).reshape(d_model, 3, grps, hpg).transpose(0, 2, 1, 3).reshape(d_model, grps, 3 * hpg)
        ng = jnp.pad(ng, ((0, 0), (0, 0), (0, LANES - 3 * hpg))).reshape(d_model, grps * LANES)
        w_gate = jnp.concatenate([cols(o_rg, ret_v), cols(o_bg, 3 * d_model), ng], axis=1)
        w_gate = jnp.pad(w_gate, ((0, 0), (0, gate_cols - w_gate.shape[1]))).astype(BF16)

        tm_proj = _pick(m, (1024, 512, 256, 128))
        g_pre = g_pre_mix[l].reshape(1, d_model)
        proj = _norm_proj(x2d, g_pre, w_attn, head_major=True, tm=tm_proj, tn=512, name="in_proj_heads")
        gates = _norm_proj(x2d, g_pre, w_gate, head_major=False, tm=tm_proj, tn=512, name="in_proj_gates")

        rows = seq // CMP_STRIDE
        proj_rows = proj.reshape(proj.shape[0], batch, rows, CMP_STRIDE * dh)
        pe = jnp.stack([cmp_pe_k[l], cmp_pe_v[l]]).reshape(2, 2, CMP_STRIDE * dh)
        w1 = jnp.stack([cmp_w1_k[l], cmp_w1_v[l]]).astype(BF16)
        w2 = jnp.stack([cmp_w2_k[l], cmp_w2_v[l]]).astype(BF16)
        cmp_kv = _compress(proj_rows, kv_chunk0, pe, w1, w2, batch)
        y_a = _nsa_attention(tbl_flat, proj, cmp_kv, gates, cmp_bias, band, batch=batch, seq=seq,
                             kv_chunk0=kv_chunk0, gate_blk0=ng_col0 // LANES)

        y_b = _retention(proj, gates, batch=batch, seq=seq, q_blk=rq_chunk0 // RET_HEADS,
                         k_blk=rk_chunk0 // RET_HEADS, v_blk=rv_chunk0 // (2 * RET_HEADS))

        mem2d = mem.reshape(batch * mem_len, d_model)
        mkv = _norm_proj(mem2d, g_mem[l].reshape(1, d_model), w_mem_kv[l].astype(BF16), head_major=True,
                         tm=_pick(batch * mem_len, (1024, 512, 256)), tn=512, name="mem_kv_proj")
        y_c = _mem_attention(proj, mkv, batch=batch, seq=seq, mem_len=mem_len,
                             q_blk=mq_chunk0 // (mem_q // LANES), tq=512)

        merged = _merge(y_a, y_b, y_c, w_br_nsa[l].astype(BF16), w_br_ret[l].astype(BF16),
                        w_br_mem[l].astype(BF16), gates, d_model=d_model, gate_col0=bg_col0,
                        tm=tm_proj, tn=512)
        x2d = _out_proj(merged, w_o[l].astype(BF16), g_post_mix[l].reshape(1, d_model), x2d, tm=256)

        x2d = _ffn(x2d, g_pre_ffn[l].reshape(1, d_model), w_up[l].astype(BF16), conv_w[l],
                   conv_b[l].reshape(1, -1), w_down[l].astype(BF16), g_post_ffn[l].reshape(1, d_model),
                   seq=seq, tm=512, tn=512)
    return x2d.reshape(batch, seq, d_model)
```

```python
import functools
import math

import jax
import jax.numpy as jnp
import numpy as np
from jax import lax
from jax.experimental import pallas as pl
from jax.experimental.pallas import tpu as pltpu

F32 = jnp.float32
BF16 = jnp.bfloat16

LANES = 128
EPS = 1e-6
NEG = -1e30

NSA_HEADS = 8
NSA_GROUPS = 2
NSA_HPG = NSA_HEADS // NSA_GROUPS
HEAD_DIM = 128
CMP_BLOCK = 32
CMP_STRIDE = 16
CMP_HIDDEN = 256
SEL_BLOCK = 64
SEL_TOPK = 16
SEL_LOCAL = 2
FORCE_SCORE = 1e4
WINDOW = 512
RET_HEADS = 4
RET_QK_DIM = 128
RET_V_DIM = 256
RET_CHUNK = 128
MEM_HEADS = 4
MEM_HEAD_DIM = 256
REL_BUCKETS = 32
REL_MAX_DIST = 128
CONV_WIDTH = 3

QT = 128
KCH = 2 * QT
PAD_BLOCKS = 8
HALO = 16

VMEM_LIMIT = 56 * 1024 * 1024


def _cparams(sem, vmem=VMEM_LIMIT):
    return pltpu.CompilerParams(dimension_semantics=sem, vmem_limit_bytes=vmem)


def _dot(a, b):
    return jnp.dot(a, b, preferred_element_type=F32)


def _dot_nt(a, b):
    return lax.dot_general(a, b, (((1,), (1,)), ((), ())), preferred_element_type=F32)


def _rms(x, g):
    ms = jnp.mean(x * x, axis=-1, keepdims=True)
    return x * lax.rsqrt(ms + EPS) * g


def _norm_proj_kernel(x_ref, g_ref, w_ref, o_ref, h_ref, *, head_major):
    @pl.when(pl.program_id(1) == 0)
    def _():
        h_ref[...] = _rms(x_ref[...], g_ref[...]).astype(BF16)

    acc = _dot(h_ref[...], w_ref[...])
    if head_major:
        for c in range(acc.shape[1] // LANES):
            o_ref[c] = acc[:, c * LANES:(c + 1) * LANES].astype(o_ref.dtype)
    else:
        o_ref[...] = acc.astype(o_ref.dtype)


def _norm_proj(x2d, g, w, *, head_major, tm, tn, name):
    m, d = x2d.shape
    n = w.shape[1]
    grid = (m // tm, n // tn)
    if head_major:
        out_shape = jax.ShapeDtypeStruct((n // LANES, m, LANES), BF16)
        out_spec = pl.BlockSpec((tn // LANES, tm, LANES), lambda i, j: (j, i, 0))
    else:
        out_shape = jax.ShapeDtypeStruct((m, n), BF16)
        out_spec = pl.BlockSpec((tm, tn), lambda i, j: (i, j))
    return pl.pallas_call(
        functools.partial(_norm_proj_kernel, head_major=head_major),
        out_shape=out_shape,
        grid=grid,
        in_specs=[
            pl.BlockSpec((tm, d), lambda i, j: (i, 0)),
            pl.BlockSpec((1, d), lambda i, j: (0, 0)),
            pl.BlockSpec((d, tn), lambda i, j: (0, j)),
        ],
        out_specs=out_spec,
        scratch_shapes=[pltpu.VMEM((tm, d), BF16)],
        compiler_params=_cparams(("parallel", "arbitrary")),
        name=name,
    )(x2d, g, w)


def _compress_kernel(x_ref, pe_ref, w1_ref, w2_ref, o_ref, ot_ref):
    half = CMP_STRIDE * HEAD_DIM
    x = x_ref[0, 0].astype(F32)
    pe = pe_ref[0]
    a = _dot((x + pe[0:1]).astype(BF16), w1_ref[0, :half, :])
    b = _dot((x + pe[1:2]).astype(BF16), w1_ref[0, half:, :])
    rows = x.shape[0]
    hid = a + pltpu.roll(b, rows - 1, 0)
    hid = hid * (1.0 / (1.0 + jnp.exp(-hid)))
    out = _dot(hid.astype(BF16), w2_ref[0])
    o_ref[0, 0] = out.astype(o_ref.dtype)
    ot_ref[0, 0] = out.T.astype(ot_ref.dtype)


def _compress(cmp_rows, pe, w1, w2, batch):
    _, _, rows, width = cmp_rows.shape
    n_out = 2 * NSA_GROUPS
    return pl.pallas_call(
        _compress_kernel,
        out_shape=(jax.ShapeDtypeStruct((batch, n_out, rows, HEAD_DIM), BF16),
                   jax.ShapeDtypeStruct((batch, n_out, HEAD_DIM, rows), BF16)),
        grid=(batch, n_out),
        in_specs=[
            pl.BlockSpec((1, 1, rows, width), lambda b, c: (c, b, 0, 0)),
            pl.BlockSpec((1, 2, width), lambda b, c: (c // NSA_GROUPS, 0, 0)),
            pl.BlockSpec((1, 2 * width, CMP_HIDDEN), lambda b, c: (c // NSA_GROUPS, 0, 0)),
            pl.BlockSpec((1, CMP_HIDDEN, HEAD_DIM), lambda b, c: (c // NSA_GROUPS, 0, 0)),
        ],
        out_specs=(pl.BlockSpec((1, 1, rows, HEAD_DIM), lambda b, c: (b, c, 0, 0)),
                   pl.BlockSpec((1, 1, HEAD_DIM, rows), lambda b, c: (b, c, 0, 0))),
        compiler_params=_cparams(("parallel", "parallel")),
        name="nsa_compress",
    )(cmp_rows, pe, w1, w2)


def _bias_from_rel(rel, tbl_ref, head):
    max_exact = REL_BUCKETS // 2
    n = jnp.maximum(rel, 0)
    nf = jnp.maximum(n, 1).astype(F32)
    large = max_exact + (jnp.log(nf / max_exact) / math.log(REL_MAX_DIST / max_exact)
                         * (REL_BUCKETS - max_exact)).astype(jnp.int32)
    large = jnp.minimum(large, REL_BUCKETS - 1)
    bucket = jnp.where(n < max_exact, n, large)
    out = jnp.zeros(rel.shape, F32)
    for b in range(REL_BUCKETS):
        out = jnp.where(bucket == b, tbl_ref[b * NSA_HEADS + head], out)
    return out


def _cmp_bias_kernel(tbl_ref, o_ref):
    head, blk = pl.program_id(0), pl.program_id(1)
    rows, cols = o_ref.shape[1], o_ref.shape[2]
    n = lax.broadcasted_iota(jnp.int32, (rows, cols), 0)
    t = blk * cols + lax.broadcasted_iota(jnp.int32, (rows, cols), 1)
    o_ref[0] = _bias_from_rel(t - (n * CMP_STRIDE + CMP_BLOCK - 1), tbl_ref, head)


def _band_bias_kernel(tbl_ref, o_ref):
    head = pl.program_id(0)
    rows, cols = o_ref.shape[1], o_ref.shape[2]
    c = lax.broadcasted_iota(jnp.int32, (rows, cols), 0)
    i = lax.broadcasted_iota(jnp.int32, (rows, cols), 1)
    rel = i - c + QT
    far = tbl_ref[(REL_BUCKETS - 1) * NSA_HEADS + head]
    o_ref[0] = jnp.where(rel >= 0, _bias_from_rel(rel, tbl_ref, head) - far, NEG)


def _bias_tables(tbl_flat, seq):
    cols = 256
    smem = pl.BlockSpec(memory_space=pltpu.SMEM)
    cmp_bias = pl.pallas_call(
        _cmp_bias_kernel,
        out_shape=jax.ShapeDtypeStruct((NSA_HEADS, LANES, seq), F32),
        grid=(NSA_HEADS, seq // cols),
        in_specs=[smem],
        out_specs=pl.BlockSpec((1, LANES, cols), lambda h, r: (h, 0, r)),
        compiler_params=_cparams(("parallel", "parallel")),
        name="cmp_bias",
    )(tbl_flat)
    band = pl.pallas_call(
        _band_bias_kernel,
        out_shape=jax.ShapeDtypeStruct((NSA_HEADS, 2 * QT, QT), F32),
        grid=(NSA_HEADS,),
        in_specs=[smem],
        out_specs=pl.BlockSpec((1, 2 * QT, QT), lambda h: (h, 0, 0)),
        compiler_params=_cparams(("parallel",)),
        name="band_bias",
    )(tbl_flat)
    return cmp_bias, band


def _split3(x):
    hi = x.astype(BF16)
    r1 = x - hi.astype(F32)
    mid = r1.astype(BF16)
    lo = (r1 - mid.astype(F32)).astype(BF16)
    return hi, mid, lo


def _nsa_kernel(q_ref, kc_ref, vct_ref, ks_ref, vs_ref, kw_ref, vw_ref, gate_ref, cbias_ref, band_ref,
                o_ref, kpad_s, vt_s, kpad_w, vt_w, acc_s, acc_w, selb_ref, posb_ref):
    qi = pl.program_id(2)
    hpg = NSA_HPG
    scale = HEAD_DIM ** -0.5
    n_tiles = vt_s.shape[0] - 1
    seq = n_tiles * QT
    n_sel = seq // SEL_BLOCK
    n_cmp = (seq - CMP_BLOCK) // CMP_STRIDE + 1
    blocks_per_tile = QT // SEL_BLOCK

    @pl.when(qi == 0)
    def _():
        for kpad, vt, k_ref, v_ref in ((kpad_s, vt_s, ks_ref, vs_ref), (kpad_w, vt_w, kw_ref, vw_ref)):
            kpad[:QT, :] = jnp.zeros((QT, HEAD_DIM), BF16)
            kpad[QT:, :] = k_ref[0]
            vt[0] = jnp.zeros((HEAD_DIM, QT), BF16)
            for t in range(n_tiles):
                vt[t + 1] = v_ref[0, t * QT:(t + 1) * QT, :].astype(F32).T.astype(BF16)
        neg_rows = jnp.full((PAD_BLOCKS, QT), NEG, F32)
        selb_ref[:PAD_BLOCKS, :] = neg_rows
        posb_ref[:PAD_BLOCKS, :] = neg_rows
        posb_ref[PAD_BLOCKS:, :] = jnp.zeros((n_sel, QT), F32)

    q = q_ref[...].reshape(hpg * QT, HEAD_DIM)
    sub = lax.broadcasted_iota(jnp.int32, (QT, QT), 0)
    lane = lax.broadcasted_iota(jnp.int32, (QT, QT), 1)

    valid_c = (qi * QT + lane) - (sub * CMP_STRIDE + CMP_BLOCK - 1) >= 0
    sc = _dot_nt(kc_ref[0, 0], q) * scale
    p_heads = []
    p_sum = None
    for h in range(hpg):
        s = jnp.where(valid_c, sc[:, h * QT:(h + 1) * QT] + cbias_ref[h], NEG)
        mx = jnp.max(s, axis=0, keepdims=True)
        p = jnp.where(valid_c, jnp.exp(s - mx), 0.0)
        p = p / jnp.maximum(jnp.sum(p, axis=0, keepdims=True), 1e-30)
        p_heads.append(p.astype(BF16))
        p_sum = p if p_sum is None else p_sum + p
    o_cmp = _dot(vct_ref[0, 0], jnp.concatenate(p_heads, axis=1))

    jb = lax.broadcasted_iota(jnp.int32, (n_sel, QT), 0)
    nn = lax.broadcasted_iota(jnp.int32, (n_sel, QT), 1)
    overlap_t = jnp.where((nn * CMP_STRIDE < jb * SEL_BLOCK + SEL_BLOCK)
                          & (nn * CMP_STRIDE + CMP_BLOCK > jb * SEL_BLOCK)
                          & (nn < n_cmp), 1.0, 0.0).astype(BF16)
    hi, mid, lo = _split3(p_sum)
    imp = _dot(overlap_t, hi) + _dot(overlap_t, mid) + _dot(overlap_t, lo)
    cur = (qi * QT + nn) // SEL_BLOCK
    forced = (jb == 0) | ((cur - jb >= 0) & (cur - jb < SEL_LOCAL))
    imp = jnp.where(forced, FORCE_SCORE, imp)
    imp = jnp.where(jb <= cur, imp, -1.0)
    rank = jnp.zeros((n_sel, QT), F32)
    for i in range(n_sel):
        other = jnp.broadcast_to(imp[i:i + 1, :], (n_sel, QT))
        rank = rank + jnp.where(jb > i, jnp.where(other >= imp, 1.0, 0.0), jnp.where(other > imp, 1.0, 0.0))
    selb_ref[PAD_BLOCKS:, :] = jnp.where(rank < min(SEL_TOPK, n_sel), 0.0, NEG)

    def row_bias(ref, tile0, n_keys):
        r0 = PAD_BLOCKS + tile0 * blocks_per_tile
        return jnp.concatenate([jnp.broadcast_to(ref[pl.ds(r0 + b, 1), :], (SEL_BLOCK, QT))
                                for b in range(n_keys // SEL_BLOCK)], axis=0)

    def add_shared(s, bias):
        return jnp.concatenate([s[:, h * QT:(h + 1) * QT] + bias for h in range(hpg)], axis=1)

    def add_band(s, bias):
        return jnp.concatenate([s[:, h * QT:(h + 1) * QT] + (band_ref[h] + bias) for h in range(hpg)], axis=1)

    def scores(kpad, tile0, n_keys):
        start = pl.multiple_of((jnp.maximum(tile0, -1) + 1) * QT, QT)
        return _dot_nt(kpad[pl.ds(start, n_keys), :], q) * scale

    def weighted_values(vt, tile0, p):
        t0 = jnp.maximum(tile0, -1) + 1
        pb = p.astype(BF16)
        pv = _dot(vt[t0], pb[:QT])
        for t in range(1, p.shape[0] // QT):
            pv = pv + _dot(vt[t0 + t], pb[t * QT:(t + 1) * QT])
        return pv

    def flash_first(s, vt, tile0, acc_ref):
        m = jnp.max(s, axis=0, keepdims=True)
        p = jnp.exp(s - m)
        acc_ref[...] = weighted_values(vt, tile0, p)
        return m, jnp.sum(p, axis=0, keepdims=True)

    def flash_next(s, vt, tile0, acc_ref, m, l):
        m_new = jnp.maximum(m, jnp.max(s, axis=0, keepdims=True))
        alpha = jnp.exp(m - m_new)
        p = jnp.exp(s - m_new)
        acc_ref[...] = acc_ref[...] * alpha + weighted_values(vt, tile0, p)
        return m_new, alpha * l + jnp.sum(p, axis=0, keepdims=True)

    near = qi - 1

    s = add_band(scores(kpad_s, near, KCH), row_bias(selb_ref, near, KCH))
    m_s, l_s = flash_first(s, vt_s, near, acc_s)

    def sel_body(e, carry):
        tile0 = qi + 1 - 2 * (e + 1)
        s = add_shared(scores(kpad_s, tile0, KCH), row_bias(selb_ref, tile0, KCH))
        return flash_next(s, vt_s, tile0, acc_s, *carry)

    m_s, l_s = lax.fori_loop(1, (qi + 2) // 2, sel_body, (m_s, l_s))

    s = add_band(scores(kpad_w, near, KCH), row_bias(posb_ref, near, KCH))
    m_w, l_w = flash_first(s, vt_w, near, acc_w)
    mid_tile = qi - 3
    s = add_shared(scores(kpad_w, mid_tile, KCH), row_bias(posb_ref, mid_tile, KCH))
    m_w, l_w = flash_next(s, vt_w, mid_tile, acc_w, m_w, l_w)
    far_tile = qi - WINDOW // QT
    edge = jnp.where(sub > lane, 0.0, NEG)
    s = add_shared(scores(kpad_w, far_tile, QT), row_bias(posb_ref, far_tile, QT) + edge)
    m_w, l_w = flash_next(s, vt_w, far_tile, acc_w, m_w, l_w)

    gate = _sigmoid(gate_ref[...].astype(F32)).T
    inv_s = 1.0 / jnp.maximum(l_s, 1e-30)
    inv_w = 1.0 / jnp.maximum(l_w, 1e-30)
    for h in range(hpg):
        sl = slice(h * QT, (h + 1) * QT)
        o_t = (gate[h:h + 1] * o_cmp[:, sl]
               + gate[hpg + h:hpg + h + 1] * (acc_s[:, sl] * inv_s[:, sl])
               + gate[2 * hpg + h:2 * hpg + h + 1] * (acc_w[:, sl] * inv_w[:, sl]))
        o_ref[:, h * HEAD_DIM:(h + 1) * HEAD_DIM] = o_t.T.astype(o_ref.dtype)


def _nsa_attention(proj, cmp_kv, cmp_kv_t, gates, cmp_bias, band, *, batch, seq, kv_chunk0, gate_blk0):
    nq = seq // QT
    hpg = NSA_HPG
    grps = NSA_GROUPS
    n_sel = seq // SEL_BLOCK
    assert WINDOW // QT <= PAD_BLOCKS // (QT // SEL_BLOCK) and WINDOW == 4 * QT and KCH == 2 * QT

    def kv_spec(branch, kv):
        base = kv_chunk0 + (branch * 2 + kv) * grps
        return pl.BlockSpec((1, seq, HEAD_DIM), lambda b, g, i: (base + g, b, 0))

    return pl.pallas_call(
        _nsa_kernel,
        out_shape=jax.ShapeDtypeStruct((batch * seq, NSA_HEADS * HEAD_DIM), BF16),
        grid=(batch, grps, nq),
        in_specs=[
            pl.BlockSpec((hpg, QT, HEAD_DIM), lambda b, g, i: (g, b * nq + i, 0)),
            pl.BlockSpec((1, 1, seq // CMP_STRIDE, HEAD_DIM), lambda b, g, i: (b, g, 0, 0)),
            pl.BlockSpec((1, 1, HEAD_DIM, seq // CMP_STRIDE), lambda b, g, i: (b, grps + g, 0, 0)),
            kv_spec(1, 0), kv_spec(1, 1), kv_spec(2, 0), kv_spec(2, 1),
            pl.BlockSpec((QT, LANES), lambda b, g, i: (b * nq + i, gate_blk0 + g)),
            pl.BlockSpec((hpg, LANES, QT), lambda b, g, i: (g, 0, i)),
            pl.BlockSpec((hpg, KCH, QT), lambda b, g, i: (g, 0, 0)),
        ],
        out_specs=pl.BlockSpec((QT, hpg * HEAD_DIM), lambda b, g, i: (b * nq + i, g)),
        scratch_shapes=[
            pltpu.VMEM((seq + QT, HEAD_DIM), BF16),
            pltpu.VMEM((nq + 1, HEAD_DIM, QT), BF16),
            pltpu.VMEM((seq + QT, HEAD_DIM), BF16),
            pltpu.VMEM((nq + 1, HEAD_DIM, QT), BF16),
            pltpu.VMEM((HEAD_DIM, hpg * QT), F32),
            pltpu.VMEM((HEAD_DIM, hpg * QT), F32),
            pltpu.VMEM((PAD_BLOCKS + n_sel, QT), F32),
            pltpu.VMEM((PAD_BLOCKS + n_sel, QT), F32),
        ],
        compiler_params=_cparams(("arbitrary", "arbitrary", "arbitrary")),
        name="nsa_attention",
    )(proj, cmp_kv, cmp_kv_t, proj, proj, proj, proj, gates, cmp_bias, band)


def _retention_kernel(q_ref, k_ref, v_ref, g_ref, cos_ref, sin_ref, dmat_ref, xi_ref, zeta_ref,
                      o_ref, state_ref, *, g_chunk):
    @pl.when(pl.program_id(1) == 0)
    def _():
        state_ref[...] = jnp.zeros(state_ref.shape, F32)

    cos, sin = cos_ref[...], sin_ref[...]
    half = RET_QK_DIM // 2
    for h in range(RET_HEADS):
        q = q_ref[h].astype(F32)
        k = k_ref[h].astype(F32)
        qr = q * cos + pltpu.roll(q, half, 1) * sin
        kr = (k * cos + pltpu.roll(k, half, 1) * sin) * (RET_QK_DIM ** -0.5)
        v = jnp.concatenate([v_ref[2 * h], v_ref[2 * h + 1]], axis=1)
        state = state_ref[h]
        qb = qr.astype(BF16)
        inner = _dot_nt(qb, kr.astype(BF16)) * dmat_ref[h]
        o = _dot(inner.astype(BF16), v) + _dot(qb, state.astype(BF16)) * xi_ref[h][:, 0:1]
        kz = (kr * zeta_ref[h]).T.astype(BF16)
        state_ref[h] = state * g_chunk[h] + _dot(kz, v)
        mu = jnp.mean(o, axis=-1, keepdims=True)
        var = jnp.mean(jnp.square(o - mu), axis=-1, keepdims=True)
        on = (o - mu) * lax.rsqrt(var + EPS)
        gate = g_ref[:, h * RET_V_DIM:(h + 1) * RET_V_DIM].astype(F32)
        gate = gate * (1.0 / (1.0 + jnp.exp(-gate)))
        o_ref[:, h * RET_V_DIM:(h + 1) * RET_V_DIM] = (on * gate).astype(o_ref.dtype)


def _retention(proj, gates, *, batch, seq, q_blk, k_blk, v_blk):
    c = RET_CHUNK
    n = seq // c
    heads = RET_HEADS
    dk = RET_QK_DIM
    theta = 1.0 / (10000.0 ** np.linspace(0.0, 1.0, dk // 2))
    ang = np.arange(seq)[:, None] * theta[None, :]
    cos = np.concatenate([np.cos(ang), np.cos(ang)], axis=1)
    sin = np.concatenate([-np.sin(ang), np.sin(ang)], axis=1)
    log_g = np.log(1.0 - np.exp2(-5.0 - np.arange(heads)))
    j = np.arange(c)
    diff = j[:, None] - j[None, :]
    dmat = np.where(diff >= 0, np.exp(np.maximum(diff, 0)[None] * log_g[:, None, None]), 0.0)
    xi = np.broadcast_to(np.exp((j + 1.0)[None, :] * log_g[:, None])[:, :, None], (heads, c, LANES))
    zeta = np.broadcast_to(np.exp((c - 1.0 - j)[None, :] * log_g[:, None])[:, :, None], (heads, c, LANES))
    g_chunk = tuple(float(v) for v in np.exp(c * log_g))
    consts = [jnp.asarray(a, F32) for a in (cos, sin, dmat, xi, zeta)]

    full3 = lambda shape: pl.BlockSpec(shape, lambda b, i: (0, 0, 0))
    return pl.pallas_call(
        functools.partial(_retention_kernel, g_chunk=g_chunk),
        out_shape=jax.ShapeDtypeStruct((batch * seq, heads * RET_V_DIM), BF16),
        grid=(batch, n),
        in_specs=[
            pl.BlockSpec((heads, c, LANES), lambda b, i: (q_blk, b * n + i, 0)),
            pl.BlockSpec((heads, c, LANES), lambda b, i: (k_blk, b * n + i, 0)),
            pl.BlockSpec((2 * heads, c, LANES), lambda b, i: (v_blk, b * n + i, 0)),
            pl.BlockSpec((c, heads * RET_V_DIM), lambda b, i: (b * n + i, 0)),
            pl.BlockSpec((c, dk), lambda b, i: (i, 0)),
            pl.BlockSpec((c, dk), lambda b, i: (i, 0)),
            full3((heads, c, c)), full3((heads, c, LANES)), full3((heads, c, LANES)),
        ],
        out_specs=pl.BlockSpec((c, heads * RET_V_DIM), lambda b, i: (b * n + i, 0)),
        scratch_shapes=[pltpu.VMEM((heads, dk, RET_V_DIM), F32)],
        compiler_params=_cparams(("parallel", "arbitrary")),
        name="retention",
    )(proj, proj, proj, gates, *consts)


def _mem_attn_kernel(q_ref, k_ref, v_ref, o_ref):
    scale = MEM_HEAD_DIM ** -0.5
    for h in range(MEM_HEADS):
        s = (_dot_nt(q_ref[2 * h], k_ref[2 * h]) + _dot_nt(q_ref[2 * h + 1], k_ref[2 * h + 1])) * scale
        mx = jnp.max(s, axis=-1, keepdims=True)
        p = jnp.exp(s - mx)
        p = (p / jnp.sum(p, axis=-1, keepdims=True)).astype(BF16)
        for half in range(2):
            c0 = h * MEM_HEAD_DIM + half * LANES
            o_ref[:, c0:c0 + LANES] = _dot(p, v_ref[2 * h + half]).astype(o_ref.dtype)


def _mem_attention(proj, mkv, *, batch, seq, mem_len, q_blk, tq):
    nq = seq // tq
    chunks = MEM_HEADS * MEM_HEAD_DIM // LANES
    return pl.pallas_call(
        _mem_attn_kernel,
        out_shape=jax.ShapeDtypeStruct((batch * seq, MEM_HEADS * MEM_HEAD_DIM), BF16),
        grid=(batch, nq),
        in_specs=[
            pl.BlockSpec((chunks, tq, LANES), lambda b, i: (q_blk, b * nq + i, 0)),
            pl.BlockSpec((chunks, mem_len, LANES), lambda b, i: (0, b, 0)),
            pl.BlockSpec((chunks, mem_len, LANES), lambda b, i: (1, b, 0)),
        ],
        out_specs=pl.BlockSpec((tq, MEM_HEADS * MEM_HEAD_DIM), lambda b, i: (b * nq + i, 0)),
        compiler_params=_cparams(("parallel", "parallel")),
        name="mem_attention",
    )(proj, mkv, mkv)


def _sigmoid(x):
    return 1.0 / (1.0 + jnp.exp(-x))


def _merge_kernel(ya_ref, yb_ref, yc_ref, wa_ref, wb_ref, wc_ref, ga_ref, gb_ref, gc_ref, o_ref):
    out = _sigmoid(ga_ref[...].astype(F32)) * _dot(ya_ref[...], wa_ref[...])
    out = out + _sigmoid(gb_ref[...].astype(F32)) * _dot(yb_ref[...], wb_ref[...])
    out = out + _sigmoid(gc_ref[...].astype(F32)) * _dot(yc_ref[...], wc_ref[...])
    o_ref[...] = out.astype(o_ref.dtype)


def _merge(ya, yb, yc, wa, wb, wc, gates, *, d_model, gate_col0, tm, tn):
    m, kdim = ya.shape
    y_spec = pl.BlockSpec((tm, kdim), lambda i, j: (i, 0))
    w_spec = pl.BlockSpec((kdim, tn), lambda i, j: (0, j))

    def g_spec(branch):
        base = (gate_col0 + branch * d_model) // tn
        return pl.BlockSpec((tm, tn), lambda i, j: (i, base + j))

    return pl.pallas_call(
        _merge_kernel,
        out_shape=jax.ShapeDtypeStruct((m, d_model), BF16),
        grid=(m // tm, d_model // tn),
        in_specs=[y_spec, y_spec, y_spec, w_spec, w_spec, w_spec, g_spec(0), g_spec(1), g_spec(2)],
        out_specs=pl.BlockSpec((tm, tn), lambda i, j: (i, j)),
        compiler_params=_cparams(("parallel", "parallel")),
        name="branch_merge",
    )(ya, yb, yc, wa, wb, wc, gates, gates, gates)


def _out_proj_kernel(y_ref, w_ref, g_ref, x_ref, o_ref):
    o_ref[...] = x_ref[...] + _rms(_dot(y_ref[...], w_ref[...]), g_ref[...])


def _out_proj(y, w, g, x2d, *, tm):
    m, d = x2d.shape
    return pl.pallas_call(
        _out_proj_kernel,
        out_shape=jax.ShapeDtypeStruct((m, d), F32),
        grid=(m // tm,),
        in_specs=[
            pl.BlockSpec((tm, d), lambda i: (i, 0)),
            pl.BlockSpec((d, d), lambda i: (0, 0)),
            pl.BlockSpec((1, d), lambda i: (0, 0)),
            pl.BlockSpec((tm, d), lambda i: (i, 0)),
        ],
        out_specs=pl.BlockSpec((tm, d), lambda i: (i, 0)),
        compiler_params=_cparams(("parallel",)),
        name="out_proj",
    )(y, w, g, x2d)


def _gelu_tanh(x):
    return 0.5 * x * (1.0 + jnp.tanh(math.sqrt(2.0 / math.pi) * (x + 0.044715 * (x * x * x))))


def _ffn_kernel(x_ref, halo_ref, gpre_ref, wg_ref, wv_ref, cwg_ref, cwv_ref, cbg_ref, cbv_ref,
                wd_ref, gpost_ref, o_ref, h_ref, acc_ref, *, seq):
    i, j = pl.program_id(0), pl.program_id(1)
    tm = x_ref.shape[0]

    @pl.when(j == 0)
    def _():
        h_ref[HALO:, :] = _rms(x_ref[...], gpre_ref[...]).astype(BF16)
        keep = jnp.where((i * tm) % seq == 0, 0.0, 1.0)
        h_ref[:HALO, :] = (_rms(halo_ref[...], gpre_ref[...]) * keep).astype(BF16)
        acc_ref[...] = jnp.zeros(acc_ref.shape, F32)

    h = h_ref[...]

    def conv(u, w_ref, b_ref):
        w = w_ref[...]
        y = b_ref[...] + w[2:3] * u[HALO:]
        y = y + w[1:2] * pltpu.roll(u, 1, 0)[HALO:]
        return y + w[0:1] * pltpu.roll(u, 2, 0)[HALO:]

    yg = conv(_dot(h, wg_ref[...]), cwg_ref, cbg_ref)
    yv = conv(_dot(h, wv_ref[...]), cwv_ref, cbv_ref)
    act = (_gelu_tanh(yg) * yv).astype(BF16)
    acc_ref[...] += _dot(act, wd_ref[...])

    @pl.when(j == pl.num_programs(1) - 1)
    def _():
        o_ref[...] = x_ref[...] + _rms(acc_ref[...], gpost_ref[...])


def _ffn(x2d, gpre, w_up, conv_w, conv_b, w_down, gpost, *, seq, tm, tn):
    m, d = x2d.shape
    d_ff = w_down.shape[0]
    nj = d_ff // tn
    halo_blocks = tm // HALO
    return pl.pallas_call(
        functools.partial(_ffn_kernel, seq=seq),
        out_shape=jax.ShapeDtypeStruct((m, d), F32),
        grid=(m // tm, nj),
        in_specs=[
            pl.BlockSpec((tm, d), lambda i, j: (i, 0)),
            pl.BlockSpec((HALO, d), lambda i, j: (jnp.maximum(i * halo_blocks - 1, 0), 0)),
            pl.BlockSpec((1, d), lambda i, j: (0, 0)),
            pl.BlockSpec((d, tn), lambda i, j: (0, j)),
            pl.BlockSpec((d, tn), lambda i, j: (0, nj + j)),
            pl.BlockSpec((CONV_WIDTH, tn), lambda i, j: (0, j)),
            pl.BlockSpec((CONV_WIDTH, tn), lambda i, j: (0, nj + j)),
            pl.BlockSpec((1, tn), lambda i, j: (0, j)),
            pl.BlockSpec((1, tn), lambda i, j: (0, nj + j)),
            pl.BlockSpec((tn, d), lambda i, j: (j, 0)),
            pl.BlockSpec((1, d), lambda i, j: (0, 0)),
        ],
        out_specs=pl.BlockSpec((tm, d), lambda i, j: (i, 0)),
        scratch_shapes=[pltpu.VMEM((tm + HALO, d), BF16), pltpu.VMEM((tm, d), F32)],
        compiler_params=_cparams(("parallel", "arbitrary")),
        name="conv_ffn",
    )(x2d, x2d, gpre, w_up, w_up, conv_w, conv_w, conv_b, conv_b, w_down, gpost)


def _pick(n, prefs):
    for t in prefs:
        if n % t == 0:
            return t
    raise ValueError(f"no tile in {prefs} divides {n}")


def kernel(x, mem, w_in, cmp_pe_k, cmp_w1_k, cmp_w2_k, cmp_pe_v, cmp_w1_v, cmp_w2_v, rel_bias, w_mem_kv,
           w_br_nsa, w_br_ret, w_br_mem, w_o, w_up, conv_w, conv_b, w_down, g_pre_mix, g_post_mix, g_mem,
           g_pre_ffn, g_post_ffn):
    batch, seq, d_model = x.shape
    mem_len = mem.shape[1]
    depth = w_in.shape[0]
    m = batch * seq
    grps, hpg, dh = NSA_GROUPS, NSA_HPG, HEAD_DIM

    nsa_q = NSA_HEADS * dh
    nsa_kv = 3 * 2 * grps * dh
    nsa_gates = 3 * NSA_HEADS
    ret_qk = RET_HEADS * RET_QK_DIM
    ret_v = RET_HEADS * RET_V_DIM
    mem_q = MEM_HEADS * MEM_HEAD_DIM
    splits = (nsa_q, nsa_kv, nsa_gates, ret_qk, ret_qk, ret_v, ret_v, mem_q, 3 * d_model)
    offs = np.concatenate([[0], np.cumsum(splits)])
    o_nq, o_nkv, o_ng, o_rq, o_rk, o_rv, o_rg, o_mq, o_bg = (int(v) for v in offs[:-1])

    q_chunk0 = 0
    rv_chunk0 = q_chunk0 + nsa_q // LANES
    mq_chunk0 = rv_chunk0 + ret_v // LANES
    rq_chunk0 = mq_chunk0 + mem_q // LANES
    rk_chunk0 = rq_chunk0 + ret_qk // LANES
    kv_chunk0 = rk_chunk0 + ret_qk // LANES
    bg_col0 = ret_v
    ng_col0 = bg_col0 + 3 * d_model
    gate_cols = -(-(ng_col0 + grps * LANES) // 512) * 512

    tbl_flat = rel_bias.reshape(-1)
    cmp_bias, band = _bias_tables(tbl_flat, seq)
    x2d = x.reshape(m, d_model)

    for l in range(depth):
        w = w_in[l]
        cols = lambda o, n: w[:, o:o + n]
        w_attn = jnp.concatenate([cols(o_nq, nsa_q), cols(o_rv, ret_v), cols(o_mq, mem_q), cols(o_rq, ret_qk),
                                  cols(o_rk, ret_qk), cols(o_nkv, nsa_kv)], axis=1).astype(BF16)
        ng = cols(o_ng, nsa_gates).reshape(d_model, 3, grps, hpg).transpose(0, 2, 1, 3).reshape(d_model, grps, 3 * hpg)
        ng = jnp.pad(ng, ((0, 0), (0, 0), (0, LANES - 3 * hpg))).reshape(d_model, grps * LANES)
        w_gate = jnp.concatenate([cols(o_rg, ret_v), cols(o_bg, 3 * d_model), ng], axis=1)
        w_gate = jnp.pad(w_gate, ((0, 0), (0, gate_cols - w_gate.shape[1]))).astype(BF16)

        tm_proj = _pick(m, (1024, 512, 256, 128))
        g_pre = g_pre_mix[l].reshape(1, d_model)
        proj = _norm_proj(x2d, g_pre, w_attn, head_major=True, tm=tm_proj, tn=512, name="in_proj_heads")
        gates = _norm_proj(x2d, g_pre, w_gate, head_major=False, tm=tm_proj, tn=512, name="in_proj_gates")

        rows = seq // CMP_STRIDE
        cmp_rows = proj[kv_chunk0:kv_chunk0 + 2 * grps].reshape(2 * grps, batch, rows, CMP_STRIDE * dh)
        pe = jnp.stack([cmp_pe_k[l], cmp_pe_v[l]]).reshape(2, 2, CMP_STRIDE * dh)
        w1 = jnp.stack([cmp_w1_k[l], cmp_w1_v[l]]).astype(BF16)
        w2 = jnp.stack([cmp_w2_k[l], cmp_w2_v[l]]).astype(BF16)
        cmp_kv, cmp_kv_t = _compress(cmp_rows, pe, w1, w2, batch)
        y_a = _nsa_attention(proj, cmp_kv, cmp_kv_t, gates, cmp_bias, band, batch=batch, seq=seq,
                             kv_chunk0=kv_chunk0, gate_blk0=ng_col0 // LANES)

        y_b = _retention(proj, gates, batch=batch, seq=seq, q_blk=rq_chunk0 // RET_HEADS,
                         k_blk=rk_chunk0 // RET_HEADS, v_blk=rv_chunk0 // (2 * RET_HEADS))

        mem2d = mem.reshape(batch * mem_len, d_model)
        mkv = _norm_proj(mem2d, g_mem[l].reshape(1, d_model), w_mem_kv[l].astype(BF16), head_major=True,
                         tm=_pick(batch * mem_len, (1024, 512, 256)), tn=512, name="mem_kv_proj")
        y_c = _mem_attention(proj, mkv, batch=batch, seq=seq, mem_len=mem_len,
                             q_blk=mq_chunk0 // (mem_q // LANES), tq=512)

        merged = _merge(y_a, y_b, y_c, w_br_nsa[l].astype(BF16), w_br_ret[l].astype(BF16),
                        w_br_mem[l].astype(BF16), gates, d_model=d_model, gate_col0=bg_col0,
                        tm=tm_proj, tn=512)
        x2d = _out_proj(merged, w_o[l].astype(BF16), g_post_mix[l].reshape(1, d_model), x2d, tm=256)

        x2d = _ffn(x2d, g_pre_ffn[l].reshape(1, d_model), w_up[l].astype(BF16), conv_w[l],
                   conv_b[l].reshape(1, -1), w_down[l].astype(BF16), g_post_ffn[l].reshape(1, d_model),
                   seq=seq, tm=512, tn=512)
    return x2d.reshape(batch, seq, d_model)
```

```python
import functools
import math

import jax
import jax.numpy as jnp
import numpy as np
from jax import lax
from jax.experimental import pallas as pl
from jax.experimental.pallas import tpu as pltpu

F32 = jnp.float32
BF16 = jnp.bfloat16

LANES = 128
EPS = 1e-6
NEG = -1e30

NSA_HEADS = 8
NSA_GROUPS = 2
NSA_HPG = NSA_HEADS // NSA_GROUPS
HEAD_DIM = 128
CMP_BLOCK = 32
CMP_STRIDE = 16
CMP_HIDDEN = 256
SEL_BLOCK = 64
SEL_TOPK = 16
SEL_LOCAL = 2
FORCE_SCORE = 1e4
WINDOW = 512
RET_HEADS = 4
RET_QK_DIM = 128
RET_V_DIM = 256
RET_CHUNK = 128
MEM_HEADS = 4
MEM_HEAD_DIM = 256
REL_BUCKETS = 32
REL_MAX_DIST = 128
CONV_WIDTH = 3

QT = 128
KCH = 2 * QT
PAD_BLOCKS = 8
HALO = 16

VMEM_LIMIT = 56 * 1024 * 1024


def _cparams(sem, vmem=VMEM_LIMIT):
    return pltpu.CompilerParams(dimension_semantics=sem, vmem_limit_bytes=vmem)


def _dot(a, b):
    return jnp.dot(a, b, preferred_element_type=F32)


def _dot_nt(a, b):
    return lax.dot_general(a, b, (((1,), (1,)), ((), ())), preferred_element_type=F32)


def _rms(x, g):
    ms = jnp.mean(x * x, axis=-1, keepdims=True)
    return x * lax.rsqrt(ms + EPS) * g


def _norm_proj_kernel(x_ref, g_ref, w_ref, o_ref, h_ref, *, head_major):
    @pl.when(pl.program_id(1) == 0)
    def _():
        h_ref[...] = _rms(x_ref[...], g_ref[...]).astype(BF16)

    acc = _dot(h_ref[...], w_ref[...])
    if head_major:
        for c in range(acc.shape[1] // LANES):
            o_ref[c] = acc[:, c * LANES:(c + 1) * LANES].astype(o_ref.dtype)
    else:
        o_ref[...] = acc.astype(o_ref.dtype)


def _norm_proj(x2d, g, w, *, head_major, tm, tn, name):
    m, d = x2d.shape
    n = w.shape[1]
    grid = (m // tm, n // tn)
    if head_major:
        out_shape = jax.ShapeDtypeStruct((n // LANES, m, LANES), BF16)
        out_spec = pl.BlockSpec((tn // LANES, tm, LANES), lambda i, j: (j, i, 0))
    else:
        out_shape = jax.ShapeDtypeStruct((m, n), BF16)
        out_spec = pl.BlockSpec((tm, tn), lambda i, j: (i, j))
    return pl.pallas_call(
        functools.partial(_norm_proj_kernel, head_major=head_major),
        out_shape=out_shape,
        grid=grid,
        in_specs=[
            pl.BlockSpec((tm, d), lambda i, j: (i, 0)),
            pl.BlockSpec((1, d), lambda i, j: (0, 0)),
            pl.BlockSpec((d, tn), lambda i, j: (0, j)),
        ],
        out_specs=out_spec,
        scratch_shapes=[pltpu.VMEM((tm, d), BF16)],
        compiler_params=_cparams(("parallel", "arbitrary")),
        name=name,
    )(x2d, g, w)


def _compress_kernel(x_ref, pe_ref, w1_ref, w2_ref, o_ref, ot_ref):
    half = CMP_STRIDE * HEAD_DIM
    x = x_ref[0, 0].astype(F32)
    pe = pe_ref[0]
    a = _dot((x + pe[0:1]).astype(BF16), w1_ref[0, :half, :])
    b = _dot((x + pe[1:2]).astype(BF16), w1_ref[0, half:, :])
    rows = x.shape[0]
    hid = a + pltpu.roll(b, rows - 1, 0)
    hid = hid * (1.0 / (1.0 + jnp.exp(-hid)))
    out = _dot(hid.astype(BF16), w2_ref[0])
    o_ref[0, 0] = out.astype(o_ref.dtype)
    ot_ref[0, 0] = out.T.astype(ot_ref.dtype)


def _compress(cmp_rows, pe, w1, w2, batch):
    _, _, rows, width = cmp_rows.shape
    n_out = 2 * NSA_GROUPS
    return pl.pallas_call(
        _compress_kernel,
        out_shape=(jax.ShapeDtypeStruct((batch, n_out, rows, HEAD_DIM), BF16),
                   jax.ShapeDtypeStruct((batch, n_out, HEAD_DIM, rows), BF16)),
        grid=(batch, n_out),
        in_specs=[
            pl.BlockSpec((1, 1, rows, width), lambda b, c: (c, b, 0, 0)),
            pl.BlockSpec((1, 2, width), lambda b, c: (c // NSA_GROUPS, 0, 0)),
            pl.BlockSpec((1, 2 * width, CMP_HIDDEN), lambda b, c: (c // NSA_GROUPS, 0, 0)),
            pl.BlockSpec((1, CMP_HIDDEN, HEAD_DIM), lambda b, c: (c // NSA_GROUPS, 0, 0)),
        ],
        out_specs=(pl.BlockSpec((1, 1, rows, HEAD_DIM), lambda b, c: (b, c, 0, 0)),
                   pl.BlockSpec((1, 1, HEAD_DIM, rows), lambda b, c: (b, c, 0, 0))),
        compiler_params=_cparams(("parallel", "parallel")),
        name="nsa_compress",
    )(cmp_rows, pe, w1, w2)


def _bias_from_rel(rel, tbl_ref, head):
    max_exact = REL_BUCKETS // 2
    n = jnp.maximum(rel, 0)
    nf = jnp.maximum(n, 1).astype(F32)
    large = max_exact + (jnp.log(nf / max_exact) / math.log(REL_MAX_DIST / max_exact)
                         * (REL_BUCKETS - max_exact)).astype(jnp.int32)
    large = jnp.minimum(large, REL_BUCKETS - 1)
    bucket = jnp.where(n < max_exact, n, large)
    out = jnp.zeros(rel.shape, F32)
    for b in range(REL_BUCKETS):
        out = jnp.where(bucket == b, tbl_ref[b * NSA_HEADS + head], out)
    return out


def _cmp_bias_kernel(tbl_ref, o_ref):
    head, blk = pl.program_id(0), pl.program_id(1)
    rows, cols = o_ref.shape[1], o_ref.shape[2]
    n = lax.broadcasted_iota(jnp.int32, (rows, cols), 0)
    t = blk * cols + lax.broadcasted_iota(jnp.int32, (rows, cols), 1)
    o_ref[0] = _bias_from_rel(t - (n * CMP_STRIDE + CMP_BLOCK - 1), tbl_ref, head)


def _band_bias_kernel(tbl_ref, o_ref):
    head = pl.program_id(0)
    rows, cols = o_ref.shape[1], o_ref.shape[2]
    c = lax.broadcasted_iota(jnp.int32, (rows, cols), 0)
    i = lax.broadcasted_iota(jnp.int32, (rows, cols), 1)
    rel = i - c + QT
    far = tbl_ref[(REL_BUCKETS - 1) * NSA_HEADS + head]
    o_ref[0] = jnp.where(rel >= 0, _bias_from_rel(rel, tbl_ref, head) - far, NEG)


def _bias_tables(tbl_flat, seq):
    cols = 256
    smem = pl.BlockSpec(memory_space=pltpu.SMEM)
    cmp_bias = pl.pallas_call(
        _cmp_bias_kernel,
        out_shape=jax.ShapeDtypeStruct((NSA_HEADS, LANES, seq), F32),
        grid=(NSA_HEADS, seq // cols),
        in_specs=[smem],
        out_specs=pl.BlockSpec((1, LANES, cols), lambda h, r: (h, 0, r)),
        compiler_params=_cparams(("parallel", "parallel")),
        name="cmp_bias",
    )(tbl_flat)
    band = pl.pallas_call(
        _band_bias_kernel,
        out_shape=jax.ShapeDtypeStruct((NSA_HEADS, 2 * QT, QT), F32),
        grid=(NSA_HEADS,),
        in_specs=[smem],
        out_specs=pl.BlockSpec((1, 2 * QT, QT), lambda h: (h, 0, 0)),
        compiler_params=_cparams(("parallel",)),
        name="band_bias",
    )(tbl_flat)
    return cmp_bias, band


def _split3(x):
    hi = x.astype(BF16)
    r1 = x - hi.astype(F32)
    mid = r1.astype(BF16)
    lo = (r1 - mid.astype(F32)).astype(BF16)
    return hi, mid, lo


def _nsa_kernel(q_ref, kc_ref, vct_ref, ks_ref, vs_ref, kw_ref, vw_ref, gate_ref, cbias_ref, band_ref,
                o_ref, kpad_s, vt_s, kpad_w, vt_w, acc_s, acc_w, selb_ref, posb_ref):
    qi = pl.program_id(1)
    hpg, grps = NSA_HPG, NSA_GROUPS
    scale = HEAD_DIM ** -0.5
    n_tiles = vt_s.shape[1] - 1
    seq = n_tiles * QT
    n_sel = seq // SEL_BLOCK
    n_cmp = (seq - CMP_BLOCK) // CMP_STRIDE + 1
    blocks_per_tile = QT // SEL_BLOCK

    @pl.when(qi == 0)
    def _():
        for kpad, vt, k_ref, v_ref in ((kpad_s, vt_s, ks_ref, vs_ref), (kpad_w, vt_w, kw_ref, vw_ref)):
            for g in range(grps):
                kpad[g, :QT, :] = jnp.zeros((QT, HEAD_DIM), BF16)
                kpad[g, QT:, :] = k_ref[g]
                vt[g, 0] = jnp.zeros((HEAD_DIM, QT), BF16)
                for t in range(n_tiles):
                    vt[g, t + 1] = v_ref[g, t * QT:(t + 1) * QT, :].astype(F32).T.astype(BF16)
        neg_rows = jnp.full((PAD_BLOCKS, QT), NEG, F32)
        for g in range(grps):
            selb_ref[g, :PAD_BLOCKS, :] = neg_rows
        posb_ref[:PAD_BLOCKS, :] = neg_rows
        posb_ref[PAD_BLOCKS:, :] = jnp.zeros((n_sel, QT), F32)

    qs = [q_ref[g * hpg:(g + 1) * hpg].reshape(hpg * QT, HEAD_DIM) for g in range(grps)]
    sub = lax.broadcasted_iota(jnp.int32, (QT, QT), 0)
    lane = lax.broadcasted_iota(jnp.int32, (QT, QT), 1)
    valid_c = (qi * QT + lane) - (sub * CMP_STRIDE + CMP_BLOCK - 1) >= 0
    jb = lax.broadcasted_iota(jnp.int32, (n_sel, QT), 0)
    nn = lax.broadcasted_iota(jnp.int32, (n_sel, QT), 1)
    overlap_t = jnp.where((nn * CMP_STRIDE < jb * SEL_BLOCK + SEL_BLOCK)
                          & (nn * CMP_STRIDE + CMP_BLOCK > jb * SEL_BLOCK)
                          & (nn < n_cmp), 1.0, 0.0).astype(BF16)
    cur = (qi * QT + nn) // SEL_BLOCK
    forced = (jb == 0) | ((cur - jb >= 0) & (cur - jb < SEL_LOCAL))

    def scores(kpad, g, tile0, n_keys):
        start = pl.multiple_of((jnp.maximum(tile0, -1) + 1) * QT, QT)
        return _dot_nt(kpad[g, pl.ds(start, n_keys), :], qs[g]) * scale

    near = qi - 1
    mid_tile = qi - 3
    far_tile = qi - WINDOW // QT
    raw_w = [(scores(kpad_w, g, near, KCH), scores(kpad_w, g, mid_tile, KCH), scores(kpad_w, g, far_tile, QT))
             for g in range(grps)]
    raw_s = [scores(kpad_s, g, near, KCH) for g in range(grps)]

    o_cmp = []
    for g in range(grps):
        sc = _dot_nt(kc_ref[0, g], qs[g]) * scale
        p_heads = []
        p_sum = None
        for h in range(hpg):
            s = jnp.where(valid_c, sc[:, h * QT:(h + 1) * QT] + cbias_ref[g * hpg + h], NEG)
            mx = jnp.max(s, axis=0, keepdims=True)
            p = jnp.where(valid_c, jnp.exp(s - mx), 0.0)
            p = p / jnp.maximum(jnp.sum(p, axis=0, keepdims=True), 1e-30)
            p_heads.append(p.astype(BF16))
            p_sum = p if p_sum is None else p_sum + p
        o_cmp.append(_dot(vct_ref[0, g], jnp.concatenate(p_heads, axis=1)))

        hi, mid, lo = _split3(p_sum)
        imp = _dot(overlap_t, hi) + _dot(overlap_t, mid) + _dot(overlap_t, lo)
        imp = jnp.where(forced, FORCE_SCORE, imp)
        imp = jnp.where(jb <= cur, imp, -1.0)
        rank = jnp.zeros((n_sel, QT), F32)
        for i in range(n_sel):
            other = jnp.broadcast_to(imp[i:i + 1, :], (n_sel, QT))
            rank = rank + jnp.where(jb > i, jnp.where(other >= imp, 1.0, 0.0), jnp.where(other > imp, 1.0, 0.0))
        selb_ref[g, PAD_BLOCKS:, :] = jnp.where(rank < min(SEL_TOPK, n_sel), 0.0, NEG)

    def row_bias(ref, tile0, n_keys):
        r0 = PAD_BLOCKS + tile0 * blocks_per_tile
        return jnp.concatenate([jnp.broadcast_to(ref[pl.ds(r0 + b, 1), :], (SEL_BLOCK, QT))
                                for b in range(n_keys // SEL_BLOCK)], axis=0)

    def add_shared(s, bias):
        return jnp.concatenate([s[:, h * QT:(h + 1) * QT] + bias for h in range(hpg)], axis=1)

    def add_band(s, bias, g):
        return jnp.concatenate([s[:, h * QT:(h + 1) * QT] + (band_ref[g * hpg + h] + bias) for h in range(hpg)],
                               axis=1)

    def weighted_values(vt, g, tile0, p):
        t0 = jnp.maximum(tile0, -1) + 1
        pb = p.astype(BF16)
        pv = _dot(vt[g, t0], pb[:QT])
        for t in range(1, p.shape[0] // QT):
            pv = pv + _dot(vt[g, t0 + t], pb[t * QT:(t + 1) * QT])
        return pv

    def flash_first(s, vt, g, tile0, acc_ref):
        m = jnp.max(s, axis=0, keepdims=True)
        p = jnp.exp(s - m)
        acc_ref[g] = weighted_values(vt, g, tile0, p)
        return m, jnp.sum(p, axis=0, keepdims=True)

    def flash_next(s, vt, g, tile0, acc_ref, m, l):
        m_new = jnp.maximum(m, jnp.max(s, axis=0, keepdims=True))
        alpha = jnp.exp(m - m_new)
        p = jnp.exp(s - m_new)
        acc_ref[g] = acc_ref[g] * alpha + weighted_values(vt, g, tile0, p)
        return m_new, alpha * l + jnp.sum(p, axis=0, keepdims=True)


    edge = jnp.where(sub > lane, 0.0, NEG)
    stats_w = [None] * grps
    for g in range(grps):
        s = add_band(raw_w[g][0], row_bias(posb_ref, near, KCH), g)
        stats_w[g] = flash_first(s, vt_w, g, near, acc_w)
    for g in range(grps):
        s = add_shared(raw_w[g][1], row_bias(posb_ref, mid_tile, KCH))
        stats_w[g] = flash_next(s, vt_w, g, mid_tile, acc_w, *stats_w[g])
    for g in range(grps):
        s = add_shared(raw_w[g][2], row_bias(posb_ref, far_tile, QT) + edge)
        stats_w[g] = flash_next(s, vt_w, g, far_tile, acc_w, *stats_w[g])

    stats_s = []
    for g in range(grps):
        s = add_band(raw_s[g], row_bias(selb_ref.at[g], near, KCH), g)
        stats_s.extend(flash_first(s, vt_s, g, near, acc_s))

    def sel_body(e, carry):
        tile0 = qi + 1 - 2 * (e + 1)
        raw = [scores(kpad_s, g, tile0, KCH) for g in range(grps)]
        out = []
        for g in range(grps):
            s = add_shared(raw[g], row_bias(selb_ref.at[g], tile0, KCH))
            out.extend(flash_next(s, vt_s, g, tile0, acc_s, carry[2 * g], carry[2 * g + 1]))
        return tuple(out)

    stats_s = lax.fori_loop(1, (qi + 2) // 2, sel_body, tuple(stats_s))

    for g in range(grps):
        gate = _sigmoid(gate_ref[:, g * LANES:(g + 1) * LANES].astype(F32)).T
        inv_s = 1.0 / jnp.maximum(stats_s[2 * g + 1], 1e-30)
        inv_w = 1.0 / jnp.maximum(stats_w[g][1], 1e-30)
        for h in range(hpg):
            sl = slice(h * QT, (h + 1) * QT)
            o_t = (gate[h:h + 1] * o_cmp[g][:, sl]
                   + gate[hpg + h:hpg + h + 1] * (acc_s[g, :, sl] * inv_s[:, sl])
                   + gate[2 * hpg + h:2 * hpg + h + 1] * (acc_w[g, :, sl] * inv_w[:, sl]))
            c0 = (g * hpg + h) * HEAD_DIM
            o_ref[:, c0:c0 + HEAD_DIM] = o_t.T.astype(o_ref.dtype)


def _nsa_attention(proj, cmp_kv, cmp_kv_t, gates, cmp_bias, band, *, batch, seq, kv_chunk0, gate_blk0):
    nq = seq // QT
    grps = NSA_GROUPS
    n_sel = seq // SEL_BLOCK
    rows_c = seq // CMP_STRIDE
    assert WINDOW // QT <= PAD_BLOCKS // (QT // SEL_BLOCK) and WINDOW == 4 * QT and KCH == 2 * QT
    assert kv_chunk0 % grps == 0 and gate_blk0 % grps == 0

    def kv_spec(branch, kv):
        blk = (kv_chunk0 + (branch * 2 + kv) * grps) // grps
        return pl.BlockSpec((grps, seq, HEAD_DIM), lambda b, i: (blk, b, 0))

    return pl.pallas_call(
        _nsa_kernel,
        out_shape=jax.ShapeDtypeStruct((batch * seq, NSA_HEADS * HEAD_DIM), BF16),
        grid=(batch, nq),
        in_specs=[
            pl.BlockSpec((NSA_HEADS, QT, HEAD_DIM), lambda b, i: (0, b * nq + i, 0)),
            pl.BlockSpec((1, grps, rows_c, HEAD_DIM), lambda b, i: (b, 0, 0, 0)),
            pl.BlockSpec((1, grps, HEAD_DIM, rows_c), lambda b, i: (b, 1, 0, 0)),
            kv_spec(1, 0), kv_spec(1, 1), kv_spec(2, 0), kv_spec(2, 1),
            pl.BlockSpec((QT, grps * LANES), lambda b, i: (b * nq + i, gate_blk0 // grps)),
            pl.BlockSpec((NSA_HEADS, LANES, QT), lambda b, i: (0, 0, i)),
            pl.BlockSpec((NSA_HEADS, KCH, QT), lambda b, i: (0, 0, 0)),
        ],
        out_specs=pl.BlockSpec((QT, NSA_HEADS * HEAD_DIM), lambda b, i: (b * nq + i, 0)),
        scratch_shapes=[
            pltpu.VMEM((grps, seq + QT, HEAD_DIM), BF16),
            pltpu.VMEM((grps, nq + 1, HEAD_DIM, QT), BF16),
            pltpu.VMEM((grps, seq + QT, HEAD_DIM), BF16),
            pltpu.VMEM((grps, nq + 1, HEAD_DIM, QT), BF16),
            pltpu.VMEM((grps, HEAD_DIM, NSA_HPG * QT), F32),
            pltpu.VMEM((grps, HEAD_DIM, NSA_HPG * QT), F32),
            pltpu.VMEM((grps, PAD_BLOCKS + n_sel, QT), F32),
            pltpu.VMEM((PAD_BLOCKS + n_sel, QT), F32),
        ],
        compiler_params=_cparams(("arbitrary", "arbitrary")),
        name="nsa_attention",
    )(proj, cmp_kv, cmp_kv_t, proj, proj, proj, proj, gates, cmp_bias, band)


def _retention_kernel(q_ref, k_ref, v_ref, g_ref, cos_ref, sin_ref, dmat_ref, xi_ref, zeta_ref,
                      o_ref, state_ref, *, g_chunk):
    @pl.when(pl.program_id(1) == 0)
    def _():
        state_ref[...] = jnp.zeros(state_ref.shape, F32)

    cos, sin = cos_ref[...], sin_ref[...]
    half = RET_QK_DIM // 2
    for h in range(RET_HEADS):
        q = q_ref[h].astype(F32)
        k = k_ref[h].astype(F32)
        qr = q * cos + pltpu.roll(q, half, 1) * sin
        kr = (k * cos + pltpu.roll(k, half, 1) * sin) * (RET_QK_DIM ** -0.5)
        v = jnp.concatenate([v_ref[2 * h], v_ref[2 * h + 1]], axis=1)
        state = state_ref[h]
        qb = qr.astype(BF16)
        inner = _dot_nt(qb, kr.astype(BF16)) * dmat_ref[h]
        o = _dot(inner.astype(BF16), v) + _dot(qb, state.astype(BF16)) * xi_ref[h][:, 0:1]
        kz = (kr * zeta_ref[h]).T.astype(BF16)
        state_ref[h] = state * g_chunk[h] + _dot(kz, v)
        mu = jnp.mean(o, axis=-1, keepdims=True)
        var = jnp.mean(jnp.square(o - mu), axis=-1, keepdims=True)
        on = (o - mu) * lax.rsqrt(var + EPS)
        gate = g_ref[:, h * RET_V_DIM:(h + 1) * RET_V_DIM].astype(F32)
        gate = gate * (1.0 / (1.0 + jnp.exp(-gate)))
        o_ref[:, h * RET_V_DIM:(h + 1) * RET_V_DIM] = (on * gate).astype(o_ref.dtype)


def _retention(proj, gates, *, batch, seq, q_blk, k_blk, v_blk):
    c = RET_CHUNK
    n = seq // c
    heads = RET_HEADS
    dk = RET_QK_DIM
    theta = 1.0 / (10000.0 ** np.linspace(0.0, 1.0, dk // 2))
    ang = np.arange(seq)[:, None] * theta[None, :]
    cos = np.concatenate([np.cos(ang), np.cos(ang)], axis=1)
    sin = np.concatenate([-np.sin(ang), np.sin(ang)], axis=1)
    log_g = np.log(1.0 - np.exp2(-5.0 - np.arange(heads)))
    j = np.arange(c)
    diff = j[:, None] - j[None, :]
    dmat = np.where(diff >= 0, np.exp(np.maximum(diff, 0)[None] * log_g[:, None, None]), 0.0)
    xi = np.broadcast_to(np.exp((j + 1.0)[None, :] * log_g[:, None])[:, :, None], (heads, c, LANES))
    zeta = np.broadcast_to(np.exp((c - 1.0 - j)[None, :] * log_g[:, None])[:, :, None], (heads, c, LANES))
    g_chunk = tuple(float(v) for v in np.exp(c * log_g))
    consts = [jnp.asarray(a, F32) for a in (cos, sin, dmat, xi, zeta)]

    full3 = lambda shape: pl.BlockSpec(shape, lambda b, i: (0, 0, 0))
    return pl.pallas_call(
        functools.partial(_retention_kernel, g_chunk=g_chunk),
        out_shape=jax.ShapeDtypeStruct((batch * seq, heads * RET_V_DIM), BF16),
        grid=(batch, n),
        in_specs=[
            pl.BlockSpec((heads, c, LANES), lambda b, i: (q_blk, b * n + i, 0)),
            pl.BlockSpec((heads, c, LANES), lambda b, i: (k_blk, b * n + i, 0)),
            pl.BlockSpec((2 * heads, c, LANES), lambda b, i: (v_blk, b * n + i, 0)),
            pl.BlockSpec((c, heads * RET_V_DIM), lambda b, i: (b * n + i, 0)),
            pl.BlockSpec((c, dk), lambda b, i: (i, 0)),
            pl.BlockSpec((c, dk), lambda b, i: (i, 0)),
            full3((heads, c, c)), full3((heads, c, LANES)), full3((heads, c, LANES)),
        ],
        out_specs=pl.BlockSpec((c, heads * RET_V_DIM), lambda b, i: (b * n + i, 0)),
        scratch_shapes=[pltpu.VMEM((heads, dk, RET_V_DIM), F32)],
        compiler_params=_cparams(("parallel", "arbitrary")),
        name="retention",
    )(proj, proj, proj, gates, *consts)


def _mem_attn_kernel(q_ref, k_ref, v_ref, o_ref):
    scale = MEM_HEAD_DIM ** -0.5
    for h in range(MEM_HEADS):
        s = (_dot_nt(q_ref[2 * h], k_ref[2 * h]) + _dot_nt(q_ref[2 * h + 1], k_ref[2 * h + 1])) * scale
        mx = jnp.max(s, axis=-1, keepdims=True)
        p = jnp.exp(s - mx)
        p = (p / jnp.sum(p, axis=-1, keepdims=True)).astype(BF16)
        for half in range(2):
            c0 = h * MEM_HEAD_DIM + half * LANES
            o_ref[:, c0:c0 + LANES] = _dot(p, v_ref[2 * h + half]).astype(o_ref.dtype)


def _mem_attention(proj, mkv, *, batch, seq, mem_len, q_blk, tq):
    nq = seq // tq
    chunks = MEM_HEADS * MEM_HEAD_DIM // LANES
    return pl.pallas_call(
        _mem_attn_kernel,
        out_shape=jax.ShapeDtypeStruct((batch * seq, MEM_HEADS * MEM_HEAD_DIM), BF16),
        grid=(batch, nq),
        in_specs=[
            pl.BlockSpec((chunks, tq, LANES), lambda b, i: (q_blk, b * nq + i, 0)),
            pl.BlockSpec((chunks, mem_len, LANES), lambda b, i: (0, b, 0)),
            pl.BlockSpec((chunks, mem_len, LANES), lambda b, i: (1, b, 0)),
        ],
        out_specs=pl.BlockSpec((tq, MEM_HEADS * MEM_HEAD_DIM), lambda b, i: (b * nq + i, 0)),
        compiler_params=_cparams(("parallel", "parallel")),
        name="mem_attention",
    )(proj, mkv, mkv)


def _sigmoid(x):
    return 1.0 / (1.0 + jnp.exp(-x))


def _merge_kernel(ya_ref, yb_ref, yc_ref, wa_ref, wb_ref, wc_ref, ga_ref, gb_ref, gc_ref, o_ref):
    out = _sigmoid(ga_ref[...].astype(F32)) * _dot(ya_ref[...], wa_ref[...])
    out = out + _sigmoid(gb_ref[...].astype(F32)) * _dot(yb_ref[...], wb_ref[...])
    out = out + _sigmoid(gc_ref[...].astype(F32)) * _dot(yc_ref[...], wc_ref[...])
    o_ref[...] = out.astype(o_ref.dtype)


def _merge(ya, yb, yc, wa, wb, wc, gates, *, d_model, gate_col0, tm, tn):
    m, kdim = ya.shape
    y_spec = pl.BlockSpec((tm, kdim), lambda i, j: (i, 0))
    w_spec = pl.BlockSpec((kdim, tn), lambda i, j: (0, j))

    def g_spec(branch):
        base = (gate_col0 + branch * d_model) // tn
        return pl.BlockSpec((tm, tn), lambda i, j: (i, base + j))

    return pl.pallas_call(
        _merge_kernel,
        out_shape=jax.ShapeDtypeStruct((m, d_model), BF16),
        grid=(m // tm, d_model // tn),
        in_specs=[y_spec, y_spec, y_spec, w_spec, w_spec, w_spec, g_spec(0), g_spec(1), g_spec(2)],
        out_specs=pl.BlockSpec((tm, tn), lambda i, j: (i, j)),
        compiler_params=_cparams(("parallel", "parallel")),
        name="branch_merge",
    )(ya, yb, yc, wa, wb, wc, gates, gates, gates)


def _out_proj_kernel(y_ref, w_ref, g_ref, x_ref, o_ref):
    o_ref[...] = x_ref[...] + _rms(_dot(y_ref[...], w_ref[...]), g_ref[...])


def _out_proj(y, w, g, x2d, *, tm):
    m, d = x2d.shape
    return pl.pallas_call(
        _out_proj_kernel,
        out_shape=jax.ShapeDtypeStruct((m, d), F32),
        grid=(m // tm,),
        in_specs=[
            pl.BlockSpec((tm, d), lambda i: (i, 0)),
            pl.BlockSpec((d, d), lambda i: (0, 0)),
            pl.BlockSpec((1, d), lambda i: (0, 0)),
            pl.BlockSpec((tm, d), lambda i: (i, 0)),
        ],
        out_specs=pl.BlockSpec((tm, d), lambda i: (i, 0)),
        compiler_params=_cparams(("parallel",)),
        name="out_proj",
    )(y, w, g, x2d)


def _gelu_tanh(x):
    return 0.5 * x * (1.0 + jnp.tanh(math.sqrt(2.0 / math.pi) * (x + 0.044715 * (x * x * x))))


def _ffn_kernel(x_ref, halo_ref, gpre_ref, wg_ref, wv_ref, cwg_ref, cwv_ref, cbg_ref, cbv_ref,
                wd_ref, gpost_ref, o_ref, h_ref, acc_ref, *, seq):
    i, j = pl.program_id(0), pl.program_id(1)
    tm = x_ref.shape[0]

    @pl.when(j == 0)
    def _():
        h_ref[HALO:, :] = _rms(x_ref[...], gpre_ref[...]).astype(BF16)
        keep = jnp.where((i * tm) % seq == 0, 0.0, 1.0)
        h_ref[:HALO, :] = (_rms(halo_ref[...], gpre_ref[...]) * keep).astype(BF16)
        acc_ref[...] = jnp.zeros(acc_ref.shape, F32)

    h = h_ref[...]

    def conv(u, w_ref, b_ref):
        w = w_ref[...]
        y = b_ref[...] + w[2:3] * u[HALO:]
        y = y + w[1:2] * pltpu.roll(u, 1, 0)[HALO:]
        return y + w[0:1] * pltpu.roll(u, 2, 0)[HALO:]

    yg = conv(_dot(h, wg_ref[...]), cwg_ref, cbg_ref)
    yv = conv(_dot(h, wv_ref[...]), cwv_ref, cbv_ref)
    act = (_gelu_tanh(yg) * yv).astype(BF16)
    acc_ref[...] += _dot(act, wd_ref[...])

    @pl.when(j == pl.num_programs(1) - 1)
    def _():
        o_ref[...] = x_ref[...] + _rms(acc_ref[...], gpost_ref[...])


def _ffn(x2d, gpre, w_up, conv_w, conv_b, w_down, gpost, *, seq, tm, tn):
    m, d = x2d.shape
    d_ff = w_down.shape[0]
    nj = d_ff // tn
    halo_blocks = tm // HALO
    return pl.pallas_call(
        functools.partial(_ffn_kernel, seq=seq),
        out_shape=jax.ShapeDtypeStruct((m, d), F32),
        grid=(m // tm, nj),
        in_specs=[
            pl.BlockSpec((tm, d), lambda i, j: (i, 0)),
            pl.BlockSpec((HALO, d), lambda i, j: (jnp.maximum(i * halo_blocks - 1, 0), 0)),
            pl.BlockSpec((1, d), lambda i, j: (0, 0)),
            pl.BlockSpec((d, tn), lambda i, j: (0, j)),
            pl.BlockSpec((d, tn), lambda i, j: (0, nj + j)),
            pl.BlockSpec((CONV_WIDTH, tn), lambda i, j: (0, j)),
            pl.BlockSpec((CONV_WIDTH, tn), lambda i, j: (0, nj + j)),
            pl.BlockSpec((1, tn), lambda i, j: (0, j)),
            pl.BlockSpec((1, tn), lambda i, j: (0, nj + j)),
            pl.BlockSpec((tn, d), lambda i, j: (j, 0)),
            pl.BlockSpec((1, d), lambda i, j: (0, 0)),
        ],
        out_specs=pl.BlockSpec((tm, d), lambda i, j: (i, 0)),
        scratch_shapes=[pltpu.VMEM((tm + HALO, d), BF16), pltpu.VMEM((tm, d), F32)],
        compiler_params=_cparams(("parallel", "arbitrary")),
        name="conv_ffn",
    )(x2d, x2d, gpre, w_up, w_up, conv_w, conv_w, conv_b, conv_b, w_down, gpost)


def _pick(n, prefs):
    for t in prefs:
        if n % t == 0:
            return t
    raise ValueError(f"no tile in {prefs} divides {n}")


def kernel(x, mem, w_in, cmp_pe_k, cmp_w1_k, cmp_w2_k, cmp_pe_v, cmp_w1_v, cmp_w2_v, rel_bias, w_mem_kv,
           w_br_nsa, w_br_ret, w_br_mem, w_o, w_up, conv_w, conv_b, w_down, g_pre_mix, g_post_mix, g_mem,
           g_pre_ffn, g_post_ffn):
    batch, seq, d_model = x.shape
    mem_len = mem.shape[1]
    depth = w_in.shape[0]
    m = batch * seq
    grps, hpg, dh = NSA_GROUPS, NSA_HPG, HEAD_DIM

    nsa_q = NSA_HEADS * dh
    nsa_kv = 3 * 2 * grps * dh
    nsa_gates = 3 * NSA_HEADS
    ret_qk = RET_HEADS * RET_QK_DIM
    ret_v = RET_HEADS * RET_V_DIM
    mem_q = MEM_HEADS * MEM_HEAD_DIM
    splits = (nsa_q, nsa_kv, nsa_gates, ret_qk, ret_qk, ret_v, ret_v, mem_q, 3 * d_model)
    offs = np.concatenate([[0], np.cumsum(splits)])
    o_nq, o_nkv, o_ng, o_rq, o_rk, o_rv, o_rg, o_mq, o_bg = (int(v) for v in offs[:-1])

    q_chunk0 = 0
    rv_chunk0 = q_chunk0 + nsa_q // LANES
    mq_chunk0 = rv_chunk0 + ret_v // LANES
    rq_chunk0 = mq_chunk0 + mem_q // LANES
    rk_chunk0 = rq_chunk0 + ret_qk // LANES
    kv_chunk0 = rk_chunk0 + ret_qk // LANES
    bg_col0 = ret_v
    ng_col0 = bg_col0 + 3 * d_model
    gate_cols = -(-(ng_col0 + grps * LANES) // 512) * 512

    tbl_flat = rel_bias.reshape(-1)
    cmp_bias, band = _bias_tables(tbl_flat, seq)
    x2d = x.reshape(m, d_model)

    for l in range(depth):
        w = w_in[l]
        cols = lambda o, n: w[:, o:o + n]
        w_attn = jnp.concatenate([cols(o_nq, nsa_q), cols(o_rv, ret_v), cols(o_mq, mem_q), cols(o_rq, ret_qk),
                                  cols(o_rk, ret_qk), cols(o_nkv, nsa_kv)], axis=1).astype(BF16)
        ng = cols(o_ng, nsa_gates).reshape(d_model, 3, grps, hpg).transpose(0, 2, 1, 3).reshape(d_model, grps, 3 * hpg)
        ng = jnp.pad(ng, ((0, 0), (0, 0), (0, LANES - 3 * hpg))).reshape(d_model, grps * LANES)
        w_gate = jnp.concatenate([cols(o_rg, ret_v), cols(o_bg, 3 * d_model), ng], axis=1)
        w_gate = jnp.pad(w_gate, ((0, 0), (0, gate_cols - w_gate.shape[1]))).astype(BF16)

        tm_proj = _pick(m, (1024, 512, 256, 128))
        g_pre = g_pre_mix[l].reshape(1, d_model)
        proj = _norm_proj(x2d, g_pre, w_attn, head_major=True, tm=tm_proj, tn=512, name="in_proj_heads")
        gates = _norm_proj(x2d, g_pre, w_gate, head_major=False, tm=tm_proj, tn=512, name="in_proj_gates")

        rows = seq // CMP_STRIDE
        cmp_rows = proj[kv_chunk0:kv_chunk0 + 2 * grps].reshape(2 * grps, batch, rows, CMP_STRIDE * dh)
        pe = jnp.stack([cmp_pe_k[l], cmp_pe_v[l]]).reshape(2, 2, CMP_STRIDE * dh)
        w1 = jnp.stack([cmp_w1_k[l], cmp_w1_v[l]]).astype(BF16)
        w2 = jnp.stack([cmp_w2_k[l], cmp_w2_v[l]]).astype(BF16)
        cmp_kv, cmp_kv_t = _compress(cmp_rows, pe, w1, w2, batch)
        y_a = _nsa_attention(proj, cmp_kv, cmp_kv_t, gates, cmp_bias, band, batch=batch, seq=seq,
                             kv_chunk0=kv_chunk0, gate_blk0=ng_col0 // LANES)

        y_b = _retention(proj, gates, batch=batch, seq=seq, q_blk=rq_chunk0 // RET_HEADS,
                         k_blk=rk_chunk0 // RET_HEADS, v_blk=rv_chunk0 // (2 * RET_HEADS))

        mem2d = mem.reshape(batch * mem_len, d_model)
        mkv = _norm_proj(mem2d, g_mem[l].reshape(1, d_model), w_mem_kv[l].astype(BF16), head_major=True,
                         tm=_pick(batch * mem_len, (1024, 512, 256)), tn=512, name="mem_kv_proj")
        y_c = _mem_attention(proj, mkv, batch=batch, seq=seq, mem_len=mem_len,
                             q_blk=mq_chunk0 // (mem_q // LANES), tq=512)

        merged = _merge(y_a, y_b, y_c, w_br_nsa[l].astype(BF16), w_br_ret[l].astype(BF16),
                        w_br_mem[l].astype(BF16), gates, d_model=d_model, gate_col0=bg_col0,
                        tm=tm_proj, tn=512)
        x2d = _out_proj(merged, w_o[l].astype(BF16), g_post_mix[l].reshape(1, d_model), x2d, tm=256)

        x2d = _ffn(x2d, g_pre_ffn[l].reshape(1, d_model), w_up[l].astype(BF16), conv_w[l],
                   conv_b[l].reshape(1, -1), w_down[l].astype(BF16), g_post_ffn[l].reshape(1, d_model),
                   seq=seq, tm=512, tn=512)
    return x2d.reshape(batch, seq, d_model)
```

```python
import functools
import math

import jax
import jax.numpy as jnp
import numpy as np
from jax import lax
from jax.experimental import pallas as pl
from jax.experimental.pallas import tpu as pltpu

F32 = jnp.float32
BF16 = jnp.bfloat16

LANES = 128
EPS = 1e-6
NEG = -1e30

NSA_HEADS = 8
NSA_GROUPS = 2
NSA_HPG = NSA_HEADS // NSA_GROUPS
HEAD_DIM = 128
CMP_BLOCK = 32
CMP_STRIDE = 16
CMP_HIDDEN = 256
SEL_BLOCK = 64
SEL_TOPK = 16
SEL_LOCAL = 2
FORCE_SCORE = 1e4
WINDOW = 512
RET_HEADS = 4
RET_QK_DIM = 128
RET_V_DIM = 256
RET_CHUNK = 128
MEM_HEADS = 4
MEM_HEAD_DIM = 256
REL_BUCKETS = 32
REL_MAX_DIST = 128
CONV_WIDTH = 3

QT = 128
KCH = 2 * QT
PAD_BLOCKS = 8
HALO = 16

VMEM_LIMIT = 56 * 1024 * 1024


def _cparams(sem, vmem=VMEM_LIMIT):
    return pltpu.CompilerParams(dimension_semantics=sem, vmem_limit_bytes=vmem)


def _dot(a, b):
    return jnp.dot(a, b, preferred_element_type=F32)


def _dot_nt(a, b):
    return lax.dot_general(a, b, (((1,), (1,)), ((), ())), preferred_element_type=F32)


def _rms(x, g):
    ms = jnp.mean(x * x, axis=-1, keepdims=True)
    return x * lax.rsqrt(ms + EPS) * g


def _norm_proj_kernel(x_ref, g_ref, w_ref, o_ref, h_ref, *, head_major):
    @pl.when(pl.program_id(1) == 0)
    def _():
        h_ref[...] = _rms(x_ref[...], g_ref[...]).astype(BF16)

    acc = _dot(h_ref[...], w_ref[...])
    if head_major:
        for c in range(acc.shape[1] // LANES):
            o_ref[c] = acc[:, c * LANES:(c + 1) * LANES].astype(o_ref.dtype)
    else:
        o_ref[...] = acc.astype(o_ref.dtype)


def _norm_proj(x2d, g, w, *, head_major, tm, tn, name):
    m, d = x2d.shape
    n = w.shape[1]
    grid = (m // tm, n // tn)
    if head_major:
        out_shape = jax.ShapeDtypeStruct((n // LANES, m, LANES), BF16)
        out_spec = pl.BlockSpec((tn // LANES, tm, LANES), lambda i, j: (j, i, 0))
    else:
        out_shape = jax.ShapeDtypeStruct((m, n), BF16)
        out_spec = pl.BlockSpec((tm, tn), lambda i, j: (i, j))
    return pl.pallas_call(
        functools.partial(_norm_proj_kernel, head_major=head_major),
        out_shape=out_shape,
        grid=grid,
        in_specs=[
            pl.BlockSpec((tm, d), lambda i, j: (i, 0)),
            pl.BlockSpec((1, d), lambda i, j: (0, 0)),
            pl.BlockSpec((d, tn), lambda i, j: (0, j)),
        ],
        out_specs=out_spec,
        scratch_shapes=[pltpu.VMEM((tm, d), BF16)],
        compiler_params=_cparams(("parallel", "arbitrary")),
        name=name,
    )(x2d, g, w)


def _compress_kernel(x_ref, pe_ref, w1_ref, w2_ref, o_ref, ot_ref):
    half = CMP_STRIDE * HEAD_DIM
    x = x_ref[0, 0].astype(F32)
    pe = pe_ref[0]
    a = _dot((x + pe[0:1]).astype(BF16), w1_ref[0, :half, :])
    b = _dot((x + pe[1:2]).astype(BF16), w1_ref[0, half:, :])
    rows = x.shape[0]
    hid = a + pltpu.roll(b, rows - 1, 0)
    hid = hid * (1.0 / (1.0 + jnp.exp(-hid)))
    out = _dot(hid.astype(BF16), w2_ref[0])
    o_ref[0, 0] = out.astype(o_ref.dtype)
    ot_ref[0, 0] = out.T.astype(ot_ref.dtype)


def _compress(cmp_rows, pe, w1, w2, batch):
    _, _, rows, width = cmp_rows.shape
    n_out = 2 * NSA_GROUPS
    return pl.pallas_call(
        _compress_kernel,
        out_shape=(jax.ShapeDtypeStruct((batch, n_out, rows, HEAD_DIM), BF16),
                   jax.ShapeDtypeStruct((batch, n_out, HEAD_DIM, rows), BF16)),
        grid=(batch, n_out),
        in_specs=[
            pl.BlockSpec((1, 1, rows, width), lambda b, c: (c, b, 0, 0)),
            pl.BlockSpec((1, 2, width), lambda b, c: (c // NSA_GROUPS, 0, 0)),
            pl.BlockSpec((1, 2 * width, CMP_HIDDEN), lambda b, c: (c // NSA_GROUPS, 0, 0)),
            pl.BlockSpec((1, CMP_HIDDEN, HEAD_DIM), lambda b, c: (c // NSA_GROUPS, 0, 0)),
        ],
        out_specs=(pl.BlockSpec((1, 1, rows, HEAD_DIM), lambda b, c: (b, c, 0, 0)),
                   pl.BlockSpec((1, 1, HEAD_DIM, rows), lambda b, c: (b, c, 0, 0))),
        compiler_params=_cparams(("parallel", "parallel")),
        name="nsa_compress",
    )(cmp_rows, pe, w1, w2)


def _bias_from_rel(rel, tbl_ref, head):
    max_exact = REL_BUCKETS // 2
    n = jnp.maximum(rel, 0)
    nf = jnp.maximum(n, 1).astype(F32)
    large = max_exact + (jnp.log(nf / max_exact) / math.log(REL_MAX_DIST / max_exact)
                         * (REL_BUCKETS - max_exact)).astype(jnp.int32)
    large = jnp.minimum(large, REL_BUCKETS - 1)
    bucket = jnp.where(n < max_exact, n, large)
    out = jnp.zeros(rel.shape, F32)
    for b in range(REL_BUCKETS):
        out = jnp.where(bucket == b, tbl_ref[b * NSA_HEADS + head], out)
    return out


def _cmp_bias_kernel(tbl_ref, o_ref):
    head, blk = pl.program_id(0), pl.program_id(1)
    rows, cols = o_ref.shape[1], o_ref.shape[2]
    n = lax.broadcasted_iota(jnp.int32, (rows, cols), 0)
    t = blk * cols + lax.broadcasted_iota(jnp.int32, (rows, cols), 1)
    o_ref[0] = _bias_from_rel(t - (n * CMP_STRIDE + CMP_BLOCK - 1), tbl_ref, head)


def _band_bias_kernel(tbl_ref, o_ref):
    head = pl.program_id(0)
    rows, cols = o_ref.shape[1], o_ref.shape[2]
    c = lax.broadcasted_iota(jnp.int32, (rows, cols), 0)
    i = lax.broadcasted_iota(jnp.int32, (rows, cols), 1)
    rel = i - c + QT
    far = tbl_ref[(REL_BUCKETS - 1) * NSA_HEADS + head]
    o_ref[0] = jnp.where(rel >= 0, (_bias_from_rel(rel, tbl_ref, head) - far) * (HEAD_DIM ** 0.5), NEG)


def _bias_tables(tbl_flat, seq):
    cols = 256
    smem = pl.BlockSpec(memory_space=pltpu.SMEM)
    cmp_bias = pl.pallas_call(
        _cmp_bias_kernel,
        out_shape=jax.ShapeDtypeStruct((NSA_HEADS, LANES, seq), F32),
        grid=(NSA_HEADS, seq // cols),
        in_specs=[smem],
        out_specs=pl.BlockSpec((1, LANES, cols), lambda h, r: (h, 0, r)),
        compiler_params=_cparams(("parallel", "parallel")),
        name="cmp_bias",
    )(tbl_flat)
    band = pl.pallas_call(
        _band_bias_kernel,
        out_shape=jax.ShapeDtypeStruct((NSA_HEADS, 2 * QT, QT), F32),
        grid=(NSA_HEADS,),
        in_specs=[smem],
        out_specs=pl.BlockSpec((1, 2 * QT, QT), lambda h: (h, 0, 0)),
        compiler_params=_cparams(("parallel",)),
        name="band_bias",
    )(tbl_flat)
    return cmp_bias, band


def _split3(x):
    hi = x.astype(BF16)
    r1 = x - hi.astype(F32)
    mid = r1.astype(BF16)
    lo = (r1 - mid.astype(F32)).astype(BF16)
    return hi, mid, lo


def _nsa_kernel(q_ref, kc_ref, vct_ref, ks_ref, vs_ref, kw_ref, vw_ref, gate_ref, cbias_ref, band_ref,
                o_ref, kpad_s, vt_s, kpad_w, vt_w, acc_s, acc_w, selb_ref, posb_ref):
    qi = pl.program_id(1)
    hpg, grps = NSA_HPG, NSA_GROUPS
    scale = HEAD_DIM ** -0.5
    n_tiles = vt_s.shape[1] - 1
    seq = n_tiles * QT
    n_sel = seq // SEL_BLOCK
    n_cmp = (seq - CMP_BLOCK) // CMP_STRIDE + 1
    blocks_per_tile = QT // SEL_BLOCK

    @pl.when(qi == 0)
    def _():
        for kpad, vt, k_ref, v_ref in ((kpad_s, vt_s, ks_ref, vs_ref), (kpad_w, vt_w, kw_ref, vw_ref)):
            for g in range(grps):
                kpad[g, :QT, :] = jnp.zeros((QT, HEAD_DIM), BF16)
                kpad[g, QT:, :] = k_ref[g]
                vt[g, 0] = jnp.zeros((HEAD_DIM, QT), BF16)
                for t in range(n_tiles):
                    vt[g, t + 1] = v_ref[g, t * QT:(t + 1) * QT, :].astype(F32).T.astype(BF16)
        neg_rows = jnp.full((PAD_BLOCKS, QT), NEG, F32)
        for g in range(grps):
            selb_ref[g, :PAD_BLOCKS, :] = neg_rows
        posb_ref[:PAD_BLOCKS, :] = neg_rows
        posb_ref[PAD_BLOCKS:, :] = jnp.zeros((n_sel, QT), F32)

    qs = [q_ref[g * hpg:(g + 1) * hpg].reshape(hpg * QT, HEAD_DIM) for g in range(grps)]
    sub = lax.broadcasted_iota(jnp.int32, (QT, QT), 0)
    lane = lax.broadcasted_iota(jnp.int32, (QT, QT), 1)
    valid_c = (qi * QT + lane) - (sub * CMP_STRIDE + CMP_BLOCK - 1) >= 0
    jb = lax.broadcasted_iota(jnp.int32, (n_sel, QT), 0)
    nn = lax.broadcasted_iota(jnp.int32, (n_sel, QT), 1)
    overlap_t = jnp.where((nn * CMP_STRIDE < jb * SEL_BLOCK + SEL_BLOCK)
                          & (nn * CMP_STRIDE + CMP_BLOCK > jb * SEL_BLOCK)
                          & (nn < n_cmp), 1.0, 0.0).astype(BF16)
    cur = (qi * QT + nn) // SEL_BLOCK
    forced = (jb == 0) | ((cur - jb >= 0) & (cur - jb < SEL_LOCAL))

    def scores(kpad, g, tile0, n_keys):
        start = pl.multiple_of((jnp.maximum(tile0, -1) + 1) * QT, QT)
        return _dot_nt(kpad[g, pl.ds(start, n_keys), :], qs[g])

    near = qi - 1
    mid_tile = qi - 3
    far_tile = qi - WINDOW // QT
    raw_w = [(scores(kpad_w, g, near, KCH), scores(kpad_w, g, mid_tile, KCH), scores(kpad_w, g, far_tile, QT))
             for g in range(grps)]
    raw_s = [scores(kpad_s, g, near, KCH) for g in range(grps)]

    o_cmp = []
    for g in range(grps):
        sc = _dot_nt(kc_ref[0, g], qs[g]) * scale
        p_heads = []
        p_sum = None
        for h in range(hpg):
            s = jnp.where(valid_c, sc[:, h * QT:(h + 1) * QT] + cbias_ref[g * hpg + h], NEG)
            mx = jnp.max(s, axis=0, keepdims=True)
            p = jnp.where(valid_c, jnp.exp(s - mx), 0.0)
            p = p / jnp.maximum(jnp.sum(p, axis=0, keepdims=True), 1e-30)
            p_heads.append(p.astype(BF16))
            p_sum = p if p_sum is None else p_sum + p
        o_cmp.append(_dot(vct_ref[0, g], jnp.concatenate(p_heads, axis=1)))

        hi, mid, lo = _split3(p_sum)
        imp = _dot(overlap_t, hi) + _dot(overlap_t, mid) + _dot(overlap_t, lo)
        imp = jnp.where(forced, FORCE_SCORE, imp)
        imp = jnp.where(jb <= cur, imp, -1.0)
        rank = jnp.zeros((n_sel, QT), F32)
        for i in range(n_sel):
            other = jnp.broadcast_to(imp[i:i + 1, :], (n_sel, QT))
            rank = rank + jnp.where(jb > i, jnp.where(other >= imp, 1.0, 0.0), jnp.where(other > imp, 1.0, 0.0))
        selb_ref[g, PAD_BLOCKS:, :] = jnp.where(rank < min(SEL_TOPK, n_sel), 0.0, NEG)

    def row_bias(ref, tile0, n_keys):
        r0 = PAD_BLOCKS + tile0 * blocks_per_tile
        return jnp.concatenate([jnp.broadcast_to(ref[pl.ds(r0 + b, 1), :], (SEL_BLOCK, QT))
                                for b in range(n_keys // SEL_BLOCK)], axis=0)

    def add_shared(s, bias):
        return jnp.concatenate([s[:, h * QT:(h + 1) * QT] + bias for h in range(hpg)], axis=1)

    def add_band(s, bias, g):
        return jnp.concatenate([s[:, h * QT:(h + 1) * QT] + (band_ref[g * hpg + h] + bias) for h in range(hpg)],
                               axis=1)

    def weighted_values(vt, g, tile0, p):
        t0 = jnp.maximum(tile0, -1) + 1
        pb = p.astype(BF16)
        pv = _dot(vt[g, t0], pb[:QT])
        for t in range(1, p.shape[0] // QT):
            pv = pv + _dot(vt[g, t0 + t], pb[t * QT:(t + 1) * QT])
        return pv

    to_exp2 = scale * math.log2(math.e)

    def flash_first(s, vt, g, tile0, acc_ref):
        m = jnp.max(s, axis=0, keepdims=True)
        p = jnp.exp2((s - m) * to_exp2)
        acc_ref[g] = weighted_values(vt, g, tile0, p)
        return m, jnp.sum(p, axis=0, keepdims=True)

    def flash_next(s, vt, g, tile0, acc_ref, m, l):
        m_new = jnp.maximum(m, jnp.max(s, axis=0, keepdims=True))
        alpha = jnp.exp2((m - m_new) * to_exp2)
        p = jnp.exp2((s - m_new) * to_exp2)
        acc_ref[g] = acc_ref[g] * alpha + weighted_values(vt, g, tile0, p)
        return m_new, alpha * l + jnp.sum(p, axis=0, keepdims=True)


    edge = jnp.where(sub > lane, 0.0, NEG)
    stats_w = [None] * grps
    for g in range(grps):
        s = add_band(raw_w[g][0], row_bias(posb_ref, near, KCH), g)
        stats_w[g] = flash_first(s, vt_w, g, near, acc_w)
    for g in range(grps):
        s = add_shared(raw_w[g][1], row_bias(posb_ref, mid_tile, KCH))
        stats_w[g] = flash_next(s, vt_w, g, mid_tile, acc_w, *stats_w[g])
    for g in range(grps):
        s = add_shared(raw_w[g][2], row_bias(posb_ref, far_tile, QT) + edge)
        stats_w[g] = flash_next(s, vt_w, g, far_tile, acc_w, *stats_w[g])

    stats_s = []
    for g in range(grps):
        s = add_band(raw_s[g], row_bias(selb_ref.at[g], near, KCH), g)
        stats_s.extend(flash_first(s, vt_s, g, near, acc_s))

    def sel_body(e, carry):
        tile0 = qi + 1 - 2 * (e + 1)
        raw = [scores(kpad_s, g, tile0, KCH) for g in range(grps)]
        out = []
        for g in range(grps):
            s = add_shared(raw[g], row_bias(selb_ref.at[g], tile0, KCH))
            out.extend(flash_next(s, vt_s, g, tile0, acc_s, carry[2 * g], carry[2 * g + 1]))
        return tuple(out)

    stats_s = lax.fori_loop(1, (qi + 2) // 2, sel_body, tuple(stats_s))

    for g in range(grps):
        gate = _sigmoid(gate_ref[:, g * LANES:(g + 1) * LANES].astype(F32)).T
        inv_s = 1.0 / jnp.maximum(stats_s[2 * g + 1], 1e-30)
        inv_w = 1.0 / jnp.maximum(stats_w[g][1], 1e-30)
        for h in range(hpg):
            sl = slice(h * QT, (h + 1) * QT)
            o_t = (gate[h:h + 1] * o_cmp[g][:, sl]
                   + gate[hpg + h:hpg + h + 1] * (acc_s[g, :, sl] * inv_s[:, sl])
                   + gate[2 * hpg + h:2 * hpg + h + 1] * (acc_w[g, :, sl] * inv_w[:, sl]))
            c0 = (g * hpg + h) * HEAD_DIM
            o_ref[:, c0:c0 + HEAD_DIM] = o_t.T.astype(o_ref.dtype)


def _nsa_attention(proj, cmp_kv, cmp_kv_t, gates, cmp_bias, band, *, batch, seq, kv_chunk0, gate_blk0):
    nq = seq // QT
    grps = NSA_GROUPS
    n_sel = seq // SEL_BLOCK
    rows_c = seq // CMP_STRIDE
    assert WINDOW // QT <= PAD_BLOCKS // (QT // SEL_BLOCK) and WINDOW == 4 * QT and KCH == 2 * QT
    assert kv_chunk0 % grps == 0 and gate_blk0 % grps == 0

    def kv_spec(branch, kv):
        blk = (kv_chunk0 + (branch * 2 + kv) * grps) // grps
        return pl.BlockSpec((grps, seq, HEAD_DIM), lambda b, i: (blk, b, 0))

    return pl.pallas_call(
        _nsa_kernel,
        out_shape=jax.ShapeDtypeStruct((batch * seq, NSA_HEADS * HEAD_DIM), BF16),
        grid=(batch, nq),
        in_specs=[
            pl.BlockSpec((NSA_HEADS, QT, HEAD_DIM), lambda b, i: (0, b * nq + i, 0)),
            pl.BlockSpec((1, grps, rows_c, HEAD_DIM), lambda b, i: (b, 0, 0, 0)),
            pl.BlockSpec((1, grps, HEAD_DIM, rows_c), lambda b, i: (b, 1, 0, 0)),
            kv_spec(1, 0), kv_spec(1, 1), kv_spec(2, 0), kv_spec(2, 1),
            pl.BlockSpec((QT, grps * LANES), lambda b, i: (b * nq + i, gate_blk0 // grps)),
            pl.BlockSpec((NSA_HEADS, LANES, QT), lambda b, i: (0, 0, i)),
            pl.BlockSpec((NSA_HEADS, KCH, QT), lambda b, i: (0, 0, 0)),
        ],
        out_specs=pl.BlockSpec((QT, NSA_HEADS * HEAD_DIM), lambda b, i: (b * nq + i, 0)),
        scratch_shapes=[
            pltpu.VMEM((grps, seq + QT, HEAD_DIM), BF16),
            pltpu.VMEM((grps, nq + 1, HEAD_DIM, QT), BF16),
            pltpu.VMEM((grps, seq + QT, HEAD_DIM), BF16),
            pltpu.VMEM((grps, nq + 1, HEAD_DIM, QT), BF16),
            pltpu.VMEM((grps, HEAD_DIM, NSA_HPG * QT), F32),
            pltpu.VMEM((grps, HEAD_DIM, NSA_HPG * QT), F32),
            pltpu.VMEM((grps, PAD_BLOCKS + n_sel, QT), F32),
            pltpu.VMEM((PAD_BLOCKS + n_sel, QT), F32),
        ],
        compiler_params=_cparams(("arbitrary", "arbitrary")),
        name="nsa_attention",
    )(proj, cmp_kv, cmp_kv_t, proj, proj, proj, proj, gates, cmp_bias, band)


def _retention_kernel(q_ref, k_ref, v_ref, g_ref, cos_ref, sin_ref, dmat_ref, xi_ref, zeta_ref,
                      o_ref, state_ref, *, g_chunk):
    @pl.when(pl.program_id(1) == 0)
    def _():
        state_ref[...] = jnp.zeros(state_ref.shape, F32)

    cos, sin = cos_ref[...], sin_ref[...]
    half = RET_QK_DIM // 2
    heads = range(RET_HEADS)
    qb, kr, v, inner, cross = [], [], [], [], []
    for h in heads:
        q = q_ref[h].astype(F32)
        k = k_ref[h].astype(F32)
        qb.append((q * cos + pltpu.roll(q, half, 1) * sin).astype(BF16))
        kr.append((k * cos + pltpu.roll(k, half, 1) * sin) * (RET_QK_DIM ** -0.5))
        v.append(jnp.concatenate([v_ref[2 * h], v_ref[2 * h + 1]], axis=1))
    for h in heads:
        inner.append(_dot_nt(qb[h], kr[h].astype(BF16)))
        cross.append(_dot(qb[h], state_ref[h].astype(BF16)))
    for h in heads:
        kz = (kr[h] * zeta_ref[h]).T.astype(BF16)
        state_ref[h] = state_ref[h] * g_chunk[h] + _dot(kz, v[h])
    for h in heads:
        o = _dot((inner[h] * dmat_ref[h]).astype(BF16), v[h]) + cross[h] * xi_ref[h][:, 0:1]
        mu = jnp.mean(o, axis=-1, keepdims=True)
        var = jnp.mean(jnp.square(o - mu), axis=-1, keepdims=True)
        on = (o - mu) * lax.rsqrt(var + EPS)
        gate = g_ref[:, h * RET_V_DIM:(h + 1) * RET_V_DIM].astype(F32)
        gate = gate * (1.0 / (1.0 + jnp.exp(-gate)))
        o_ref[:, h * RET_V_DIM:(h + 1) * RET_V_DIM] = (on * gate).astype(o_ref.dtype)


def _retention(proj, gates, *, batch, seq, q_blk, k_blk, v_blk):
    c = RET_CHUNK
    n = seq // c
    heads = RET_HEADS
    dk = RET_QK_DIM
    theta = 1.0 / (10000.0 ** np.linspace(0.0, 1.0, dk // 2))
    ang = np.arange(seq)[:, None] * theta[None, :]
    cos = np.concatenate([np.cos(ang), np.cos(ang)], axis=1)
    sin = np.concatenate([-np.sin(ang), np.sin(ang)], axis=1)
    log_g = np.log(1.0 - np.exp2(-5.0 - np.arange(heads)))
    j = np.arange(c)
    diff = j[:, None] - j[None, :]
    dmat = np.where(diff >= 0, np.exp(np.maximum(diff, 0)[None] * log_g[:, None, None]), 0.0)
    xi = np.broadcast_to(np.exp((j + 1.0)[None, :] * log_g[:, None])[:, :, None], (heads, c, LANES))
    zeta = np.broadcast_to(np.exp((c - 1.0 - j)[None, :] * log_g[:, None])[:, :, None], (heads, c, LANES))
    g_chunk = tuple(float(v) for v in np.exp(c * log_g))
    consts = [jnp.asarray(a, F32) for a in (cos, sin, dmat, xi, zeta)]

    full3 = lambda shape: pl.BlockSpec(shape, lambda b, i: (0, 0, 0))
    return pl.pallas_call(
        functools.partial(_retention_kernel, g_chunk=g_chunk),
        out_shape=jax.ShapeDtypeStruct((batch * seq, heads * RET_V_DIM), BF16),
        grid=(batch, n),
        in_specs=[
            pl.BlockSpec((heads, c, LANES), lambda b, i: (q_blk, b * n + i, 0)),
            pl.BlockSpec((heads, c, LANES), lambda b, i: (k_blk, b * n + i, 0)),
            pl.BlockSpec((2 * heads, c, LANES), lambda b, i: (v_blk, b * n + i, 0)),
            pl.BlockSpec((c, heads * RET_V_DIM), lambda b, i: (b * n + i, 0)),
            pl.BlockSpec((c, dk), lambda b, i: (i, 0)),
            pl.BlockSpec((c, dk), lambda b, i: (i, 0)),
            full3((heads, c, c)), full3((heads, c, LANES)), full3((heads, c, LANES)),
        ],
        out_specs=pl.BlockSpec((c, heads * RET_V_DIM), lambda b, i: (b * n + i, 0)),
        scratch_shapes=[pltpu.VMEM((heads, dk, RET_V_DIM), F32)],
        compiler_params=_cparams(("parallel", "arbitrary")),
        name="retention",
    )(proj, proj, proj, gates, *consts)


def _mem_attn_kernel(q_ref, k_ref, v_ref, o_ref):
    scale = MEM_HEAD_DIM ** -0.5
    for h in range(MEM_HEADS):
        s = (_dot_nt(q_ref[2 * h], k_ref[2 * h]) + _dot_nt(q_ref[2 * h + 1], k_ref[2 * h + 1])) * scale
        mx = jnp.max(s, axis=-1, keepdims=True)
        p = jnp.exp(s - mx)
        p = (p / jnp.sum(p, axis=-1, keepdims=True)).astype(BF16)
        for half in range(2):
            c0 = h * MEM_HEAD_DIM + half * LANES
            o_ref[:, c0:c0 + LANES] = _dot(p, v_ref[2 * h + half]).astype(o_ref.dtype)


def _mem_attention(proj, mkv, *, batch, seq, mem_len, q_blk, tq):
    nq = seq // tq
    chunks = MEM_HEADS * MEM_HEAD_DIM // LANES
    return pl.pallas_call(
        _mem_attn_kernel,
        out_shape=jax.ShapeDtypeStruct((batch * seq, MEM_HEADS * MEM_HEAD_DIM), BF16),
        grid=(batch, nq),
        in_specs=[
            pl.BlockSpec((chunks, tq, LANES), lambda b, i: (q_blk, b * nq + i, 0)),
            pl.BlockSpec((chunks, mem_len, LANES), lambda b, i: (0, b, 0)),
            pl.BlockSpec((chunks, mem_len, LANES), lambda b, i: (1, b, 0)),
        ],
        out_specs=pl.BlockSpec((tq, MEM_HEADS * MEM_HEAD_DIM), lambda b, i: (b * nq + i, 0)),
        compiler_params=_cparams(("parallel", "parallel")),
        name="mem_attention",
    )(proj, mkv, mkv)


def _sigmoid(x):
    return 1.0 / (1.0 + jnp.exp(-x))


def _merge_kernel(ya_ref, yb_ref, yc_ref, wa_ref, wb_ref, wc_ref, ga_ref, gb_ref, gc_ref, o_ref):
    out = _sigmoid(ga_ref[...].astype(F32)) * _dot(ya_ref[...], wa_ref[...])
    out = out + _sigmoid(gb_ref[...].astype(F32)) * _dot(yb_ref[...], wb_ref[...])
    out = out + _sigmoid(gc_ref[...].astype(F32)) * _dot(yc_ref[...], wc_ref[...])
    o_ref[...] = out.astype(o_ref.dtype)


def _merge(ya, yb, yc, wa, wb, wc, gates, *, d_model, gate_col0, tm, tn):
    m, kdim = ya.shape
    y_spec = pl.BlockSpec((tm, kdim), lambda i, j: (i, 0))
    w_spec = pl.BlockSpec((kdim, tn), lambda i, j: (0, j))

    def g_spec(branch):
        base = (gate_col0 + branch * d_model) // tn
        return pl.BlockSpec((tm, tn), lambda i, j: (i, base + j))

    return pl.pallas_call(
        _merge_kernel,
        out_shape=jax.ShapeDtypeStruct((m, d_model), BF16),
        grid=(m // tm, d_model // tn),
        in_specs=[y_spec, y_spec, y_spec, w_spec, w_spec, w_spec, g_spec(0), g_spec(1), g_spec(2)],
        out_specs=pl.BlockSpec((tm, tn), lambda i, j: (i, j)),
        compiler_params=_cparams(("parallel", "parallel")),
        name="branch_merge",
    )(ya, yb, yc, wa, wb, wc, gates, gates, gates)


def _out_proj_kernel(y_ref, w_ref, g_ref, x_ref, o_ref):
    o_ref[...] = x_ref[...] + _rms(_dot(y_ref[...], w_ref[...]), g_ref[...])


def _out_proj(y, w, g, x2d, *, tm):
    m, d = x2d.shape
    return pl.pallas_call(
        _out_proj_kernel,
        out_shape=jax.ShapeDtypeStruct((m, d), F32),
        grid=(m // tm,),
        in_specs=[
            pl.BlockSpec((tm, d), lambda i: (i, 0)),
            pl.BlockSpec((d, d), lambda i: (0, 0)),
            pl.BlockSpec((1, d), lambda i: (0, 0)),
            pl.BlockSpec((tm, d), lambda i: (i, 0)),
        ],
        out_specs=pl.BlockSpec((tm, d), lambda i: (i, 0)),
        compiler_params=_cparams(("parallel",)),
        name="out_proj",
    )(y, w, g, x2d)


def _gelu_tanh(x):
    return 0.5 * x * (1.0 + jnp.tanh(math.sqrt(2.0 / math.pi) * (x + 0.044715 * (x * x * x))))


def _ffn_kernel(x_ref, halo_ref, gpre_ref, wg_ref, wv_ref, cwg_ref, cwv_ref, cbg_ref, cbv_ref,
                wd_ref, gpost_ref, o_ref, h_ref, acc_ref, *, seq):
    i, j = pl.program_id(0), pl.program_id(1)
    tm = x_ref.shape[0]

    @pl.when(j == 0)
    def _():
        h_ref[HALO:, :] = _rms(x_ref[...], gpre_ref[...]).astype(BF16)
        keep = jnp.where((i * tm) % seq == 0, 0.0, 1.0)
        h_ref[:HALO, :] = (_rms(halo_ref[...], gpre_ref[...]) * keep).astype(BF16)
        acc_ref[...] = jnp.zeros(acc_ref.shape, F32)

    h = h_ref[...]

    def conv(u, w_ref, b_ref):
        w = w_ref[...]
        y = b_ref[...] + w[2:3] * u[HALO:]
        y = y + w[1:2] * pltpu.roll(u, 1, 0)[HALO:]
        return y + w[0:1] * pltpu.roll(u, 2, 0)[HALO:]

    yg = conv(_dot(h, wg_ref[...]), cwg_ref, cbg_ref)
    yv = conv(_dot(h, wv_ref[...]), cwv_ref, cbv_ref)
    act = (_gelu_tanh(yg) * yv).astype(BF16)
    acc_ref[...] += _dot(act, wd_ref[...])

    @pl.when(j == pl.num_programs(1) - 1)
    def _():
        o_ref[...] = x_ref[...] + _rms(acc_ref[...], gpost_ref[...])


def _ffn(x2d, gpre, w_up, conv_w, conv_b, w_down, gpost, *, seq, tm, tn):
    m, d = x2d.shape
    d_ff = w_down.shape[0]
    nj = d_ff // tn
    halo_blocks = tm // HALO
    return pl.pallas_call(
        functools.partial(_ffn_kernel, seq=seq),
        out_shape=jax.ShapeDtypeStruct((m, d), F32),
        grid=(m // tm, nj),
        in_specs=[
            pl.BlockSpec((tm, d), lambda i, j: (i, 0)),
            pl.BlockSpec((HALO, d), lambda i, j: (jnp.maximum(i * halo_blocks - 1, 0), 0)),
            pl.BlockSpec((1, d), lambda i, j: (0, 0)),
            pl.BlockSpec((d, tn), lambda i, j: (0, j)),
            pl.BlockSpec((d, tn), lambda i, j: (0, nj + j)),
            pl.BlockSpec((CONV_WIDTH, tn), lambda i, j: (0, j)),
            pl.BlockSpec((CONV_WIDTH, tn), lambda i, j: (0, nj + j)),
            pl.BlockSpec((1, tn), lambda i, j: (0, j)),
            pl.BlockSpec((1, tn), lambda i, j: (0, nj + j)),
            pl.BlockSpec((tn, d), lambda i, j: (j, 0)),
            pl.BlockSpec((1, d), lambda i, j: (0, 0)),
        ],
        out_specs=pl.BlockSpec((tm, d), lambda i, j: (i, 0)),
        scratch_shapes=[pltpu.VMEM((tm + HALO, d), BF16), pltpu.VMEM((tm, d), F32)],
        compiler_params=_cparams(("parallel", "arbitrary")),
        name="conv_ffn",
    )(x2d, x2d, gpre, w_up, w_up, conv_w, conv_w, conv_b, conv_b, w_down, gpost)


def _pick(n, prefs):
    for t in prefs:
        if n % t == 0:
            return t
    raise ValueError(f"no tile in {prefs} divides {n}")


def kernel(x, mem, w_in, cmp_pe_k, cmp_w1_k, cmp_w2_k, cmp_pe_v, cmp_w1_v, cmp_w2_v, rel_bias, w_mem_kv,
           w_br_nsa, w_br_ret, w_br_mem, w_o, w_up, conv_w, conv_b, w_down, g_pre_mix, g_post_mix, g_mem,
           g_pre_ffn, g_post_ffn):
    batch, seq, d_model = x.shape
    mem_len = mem.shape[1]
    depth = w_in.shape[0]
    m = batch * seq
    grps, hpg, dh = NSA_GROUPS, NSA_HPG, HEAD_DIM

    nsa_q = NSA_HEADS * dh
    nsa_kv = 3 * 2 * grps * dh
    nsa_gates = 3 * NSA_HEADS
    ret_qk = RET_HEADS * RET_QK_DIM
    ret_v = RET_HEADS * RET_V_DIM
    mem_q = MEM_HEADS * MEM_HEAD_DIM
    splits = (nsa_q, nsa_kv, nsa_gates, ret_qk, ret_qk, ret_v, ret_v, mem_q, 3 * d_model)
    offs = np.concatenate([[0], np.cumsum(splits)])
    o_nq, o_nkv, o_ng, o_rq, o_rk, o_rv, o_rg, o_mq, o_bg = (int(v) for v in offs[:-1])

    q_chunk0 = 0
    rv_chunk0 = q_chunk0 + nsa_q // LANES
    mq_chunk0 = rv_chunk0 + ret_v // LANES
    rq_chunk0 = mq_chunk0 + mem_q // LANES
    rk_chunk0 = rq_chunk0 + ret_qk // LANES
    kv_chunk0 = rk_chunk0 + ret_qk // LANES
    bg_col0 = ret_v
    ng_col0 = bg_col0 + 3 * d_model
    gate_cols = -(-(ng_col0 + grps * LANES) // 512) * 512

    tbl_flat = rel_bias.reshape(-1)
    cmp_bias, band = _bias_tables(tbl_flat, seq)
    x2d = x.reshape(m, d_model)

    for l in range(depth):
        w = w_in[l]
        cols = lambda o, n: w[:, o:o + n]
        w_attn = jnp.concatenate([cols(o_nq, nsa_q), cols(o_rv, ret_v), cols(o_mq, mem_q), cols(o_rq, ret_qk),
                                  cols(o_rk, ret_qk), cols(o_nkv, nsa_kv)], axis=1).astype(BF16)
        ng = cols(o_ng, nsa_gates).reshape(d_model, 3, grps, hpg).transpose(0, 2, 1, 3).reshape(d_model, grps, 3 * hpg)
        ng = jnp.pad(ng, ((0, 0), (0, 0), (0, LANES - 3 * hpg))).reshape(d_model, grps * LANES)
        w_gate = jnp.concatenate([cols(o_rg, ret_v), cols(o_bg, 3 * d_model), ng], axis=1)
        w_gate = jnp.pad(w_gate, ((0, 0), (0, gate_cols - w_gate.shape[1]))).astype(BF16)

        tm_proj = _pick(m, (1024, 512, 256, 128))
        g_pre = g_pre_mix[l].reshape(1, d_model)
        proj = _norm_proj(x2d, g_pre, w_attn, head_major=True, tm=tm_proj, tn=512, name="in_proj_heads")
        gates = _norm_proj(x2d, g_pre, w_gate, head_major=False, tm=tm_proj, tn=512, name="in_proj_gates")

        rows = seq // CMP_STRIDE
        cmp_rows = proj[kv_chunk0:kv_chunk0 + 2 * grps].reshape(2 * grps, batch, rows, CMP_STRIDE * dh)
        pe = jnp.stack([cmp_pe_k[l], cmp_pe_v[l]]).reshape(2, 2, CMP_STRIDE * dh)
        w1 = jnp.stack([cmp_w1_k[l], cmp_w1_v[l]]).astype(BF16)
        w2 = jnp.stack([cmp_w2_k[l], cmp_w2_v[l]]).astype(BF16)
        cmp_kv, cmp_kv_t = _compress(cmp_rows, pe, w1, w2, batch)
        y_a = _nsa_attention(proj, cmp_kv, cmp_kv_t, gates, cmp_bias, band, batch=batch, seq=seq,
                             kv_chunk0=kv_chunk0, gate_blk0=ng_col0 // LANES)

        y_b = _retention(proj, gates, batch=batch, seq=seq, q_blk=rq_chunk0 // RET_HEADS,
                         k_blk=rk_chunk0 // RET_HEADS, v_blk=rv_chunk0 // (2 * RET_HEADS))

        mem2d = mem.reshape(batch * mem_len, d_model)
        mkv = _norm_proj(mem2d, g_mem[l].reshape(1, d_model), w_mem_kv[l].astype(BF16), head_major=True,
                         tm=_pick(batch * mem_len, (1024, 512, 256)), tn=512, name="mem_kv_proj")
        y_c = _mem_attention(proj, mkv, batch=batch, seq=seq, mem_len=mem_len,
                             q_blk=mq_chunk0 // (mem_q // LANES), tq=512)

        merged = _merge(y_a, y_b, y_c, w_br_nsa[l].astype(BF16), w_br_ret[l].astype(BF16),
                        w_br_mem[l].astype(BF16), gates, d_model=d_model, gate_col0=bg_col0,
                        tm=tm_proj, tn=512)
        x2d = _out_proj(merged, w_o[l].astype(BF16), g_post_mix[l].reshape(1, d_model), x2d, tm=256)

        x2d = _ffn(x2d, g_pre_ffn[l].reshape(1, d_model), w_up[l].astype(BF16), conv_w[l],
                   conv_b[l].reshape(1, -1), w_down[l].astype(BF16), g_post_ffn[l].reshape(1, d_model),
                   seq=seq, tm=512, tn=512)
    return x2d.reshape(batch, seq, d_model)
```

```python
import functools
import math
from typing import NamedTuple

import jax
import jax.numpy as jnp
import numpy as np
from jax import lax
from jax.experimental import pallas as pl
from jax.experimental.pallas import tpu as pltpu

F32 = jnp.float32
BF16 = jnp.bfloat16

LANES = 128
MXU_WIDTH = 256
EPS = 1e-6
NEG = -1e30

NSA_HEADS = 8
NSA_GROUPS = 2
NSA_HPG = NSA_HEADS // NSA_GROUPS
HEAD_DIM = 128
CMP_BLOCK = 32
CMP_STRIDE = 16
CMP_HIDDEN = 256
SEL_BLOCK = 64
SEL_TOPK = 16
SEL_LOCAL = 2
FORCE_SCORE = 1e4
WINDOW = 512
RET_HEADS = 4
RET_QK_DIM = 128
RET_V_DIM = 256
RET_CHUNK = 128
MEM_HEADS = 4
MEM_HEAD_DIM = 256
REL_BUCKETS = 32
REL_MAX_DIST = 128
CONV_WIDTH = 3

QT = 128
KCH = 2 * QT
PAD_BLOCKS = 8
HALO = 16

VMEM_LIMIT = 56 * 1024 * 1024


def _cparams(sem, vmem=VMEM_LIMIT):
    return pltpu.CompilerParams(dimension_semantics=sem, vmem_limit_bytes=vmem)


def _dot(a, b):
    return jnp.dot(a, b, preferred_element_type=F32)


def _dot_nt(a, b):
    return lax.dot_general(a, b, (((1,), (1,)), ((), ())), preferred_element_type=F32)


def _rms(x, g):
    ms = jnp.mean(x * x, axis=-1, keepdims=True)
    return x * lax.rsqrt(ms + EPS) * g


def _norm_proj_kernel(x_ref, g_ref, w_ref, o_ref, h_ref, *, head_major):
    @pl.when(pl.program_id(1) == 0)
    def _():
        h_ref[...] = _rms(x_ref[...], g_ref[...]).astype(BF16)

    acc = _dot(h_ref[...], w_ref[...])
    if head_major:
        for c in range(acc.shape[1] // LANES):
            o_ref[c] = acc[:, c * LANES:(c + 1) * LANES].astype(o_ref.dtype)
    else:
        o_ref[...] = acc.astype(o_ref.dtype)


def _norm_proj(x2d, g, w, *, head_major, tm, tn, name):
    m, d = x2d.shape
    n = w.shape[1]
    grid = (m // tm, n // tn)
    if head_major:
        out_shape = jax.ShapeDtypeStruct((n // LANES, m, LANES), BF16)
        out_spec = pl.BlockSpec((tn // LANES, tm, LANES), lambda i, j: (j, i, 0))
    else:
        out_shape = jax.ShapeDtypeStruct((m, n), BF16)
        out_spec = pl.BlockSpec((tm, tn), lambda i, j: (i, j))
    return pl.pallas_call(
        functools.partial(_norm_proj_kernel, head_major=head_major),
        out_shape=out_shape,
        grid=grid,
        in_specs=[
            pl.BlockSpec((tm, d), lambda i, j: (i, 0)),
            pl.BlockSpec((1, d), lambda i, j: (0, 0)),
            pl.BlockSpec((d, tn), lambda i, j: (0, j)),
        ],
        out_specs=out_spec,
        scratch_shapes=[pltpu.VMEM((tm, d), BF16)],
        compiler_params=_cparams(("parallel", "arbitrary")),
        name=name,
    )(x2d, g, w)


def _compress_kernel(x_ref, pe_ref, w1_ref, w2_ref, o_ref, ot_ref):
    half = CMP_STRIDE * HEAD_DIM
    x = x_ref[0, 0].astype(F32)
    pe = pe_ref[0]
    a = _dot((x + pe[0:1]).astype(BF16), w1_ref[0, :half, :])
    b = _dot((x + pe[1:2]).astype(BF16), w1_ref[0, half:, :])
    rows = x.shape[0]
    hid = a + pltpu.roll(b, rows - 1, 0)
    hid = hid * (1.0 / (1.0 + jnp.exp(-hid)))
    out = _dot(hid.astype(BF16), w2_ref[0])
    o_ref[0, 0] = out.astype(o_ref.dtype)
    ot_ref[0, 0] = out.T.astype(ot_ref.dtype)


def _compress(cmp_rows, pe, w1, w2, batch):
    _, _, rows, width = cmp_rows.shape
    n_out = 2 * NSA_GROUPS
    return pl.pallas_call(
        _compress_kernel,
        out_shape=(jax.ShapeDtypeStruct((batch, n_out, rows, HEAD_DIM), BF16),
                   jax.ShapeDtypeStruct((batch, n_out, HEAD_DIM, rows), BF16)),
        grid=(batch, n_out),
        in_specs=[
            pl.BlockSpec((1, 1, rows, width), lambda b, c: (c, b, 0, 0)),
            pl.BlockSpec((1, 2, width), lambda b, c: (c // NSA_GROUPS, 0, 0)),
            pl.BlockSpec((1, 2 * width, CMP_HIDDEN), lambda b, c: (c // NSA_GROUPS, 0, 0)),
            pl.BlockSpec((1, CMP_HIDDEN, HEAD_DIM), lambda b, c: (c // NSA_GROUPS, 0, 0)),
        ],
        out_specs=(pl.BlockSpec((1, 1, rows, HEAD_DIM), lambda b, c: (b, c, 0, 0)),
                   pl.BlockSpec((1, 1, HEAD_DIM, rows), lambda b, c: (b, c, 0, 0))),
        compiler_params=_cparams(("parallel", "parallel")),
        name="nsa_compress",
    )(cmp_rows, pe, w1, w2)


def _bias_from_rel(rel, tbl_ref, head):
    max_exact = REL_BUCKETS // 2
    n = jnp.maximum(rel, 0)
    nf = jnp.maximum(n, 1).astype(F32)
    large = max_exact + (jnp.log(nf / max_exact) / math.log(REL_MAX_DIST / max_exact)
                         * (REL_BUCKETS - max_exact)).astype(jnp.int32)
    large = jnp.minimum(large, REL_BUCKETS - 1)
    bucket = jnp.where(n < max_exact, n, large)
    out = jnp.zeros(rel.shape, F32)
    for b in range(REL_BUCKETS):
        out = jnp.where(bucket == b, tbl_ref[b * NSA_HEADS + head], out)
    return out


def _cmp_bias_kernel(tbl_ref, o_ref):
    head, blk = pl.program_id(0), pl.program_id(1)
    rows, cols = o_ref.shape[1], o_ref.shape[2]
    n = lax.broadcasted_iota(jnp.int32, (rows, cols), 0)
    t = blk * cols + lax.broadcasted_iota(jnp.int32, (rows, cols), 1)
    o_ref[0] = _bias_from_rel(t - (n * CMP_STRIDE + CMP_BLOCK - 1), tbl_ref, head)


def _band_bias_kernel(tbl_ref, o_ref):
    head = pl.program_id(0)
    rows, cols = o_ref.shape[1], o_ref.shape[2]
    c = lax.broadcasted_iota(jnp.int32, (rows, cols), 0)
    i = lax.broadcasted_iota(jnp.int32, (rows, cols), 1)
    rel = i - c + QT
    far = tbl_ref[(REL_BUCKETS - 1) * NSA_HEADS + head]
    o_ref[0] = jnp.where(rel >= 0, (_bias_from_rel(rel, tbl_ref, head) - far) * (HEAD_DIM ** 0.5), NEG)


def _bias_tables(tbl_flat, seq):
    cols = 256
    smem = pl.BlockSpec(memory_space=pltpu.SMEM)
    cmp_bias = pl.pallas_call(
        _cmp_bias_kernel,
        out_shape=jax.ShapeDtypeStruct((NSA_HEADS, LANES, seq), F32),
        grid=(NSA_HEADS, seq // cols),
        in_specs=[smem],
        out_specs=pl.BlockSpec((1, LANES, cols), lambda h, r: (h, 0, r)),
        compiler_params=_cparams(("parallel", "parallel")),
        name="cmp_bias",
    )(tbl_flat)
    band = pl.pallas_call(
        _band_bias_kernel,
        out_shape=jax.ShapeDtypeStruct((NSA_HEADS, 2 * QT, QT), F32),
        grid=(NSA_HEADS,),
        in_specs=[smem],
        out_specs=pl.BlockSpec((1, 2 * QT, QT), lambda h: (h, 0, 0)),
        compiler_params=_cparams(("parallel",)),
        name="band_bias",
    )(tbl_flat)
    return cmp_bias, band


def _split3(x):
    hi = x.astype(BF16)
    r1 = x - hi.astype(F32)
    mid = r1.astype(BF16)
    lo = (r1 - mid.astype(F32)).astype(BF16)
    return hi, mid, lo


def _nsa_kernel(q_ref, kc_ref, vct_ref, ks_ref, vs_ref, kw_ref, vw_ref, gate_ref, cbias_ref, band_ref,
                o_ref, kpad_s, vt_s, kpad_w, vt_w, acc_s, acc_w, selb_ref, posb_ref):
    qi = pl.program_id(1)
    hpg, grps = NSA_HPG, NSA_GROUPS
    scale = HEAD_DIM ** -0.5
    n_tiles = vt_s.shape[1] - 1
    seq = n_tiles * QT
    n_sel = seq // SEL_BLOCK
    n_cmp = (seq - CMP_BLOCK) // CMP_STRIDE + 1
    blocks_per_tile = QT // SEL_BLOCK

    @pl.when(qi == 0)
    def _():
        for kpad, vt, k_ref, v_ref in ((kpad_s, vt_s, ks_ref, vs_ref), (kpad_w, vt_w, kw_ref, vw_ref)):
            for g in range(grps):
                kpad[g, :QT, :] = jnp.zeros((QT, HEAD_DIM), BF16)
                kpad[g, QT:, :] = k_ref[g]
                vt[g, 0] = jnp.zeros((HEAD_DIM, QT), BF16)
                for t in range(n_tiles):
                    vt[g, t + 1] = v_ref[g, t * QT:(t + 1) * QT, :].astype(F32).T.astype(BF16)
        neg_rows = jnp.full((PAD_BLOCKS, QT), NEG, F32)
        for g in range(grps):
            selb_ref[g, :PAD_BLOCKS, :] = neg_rows
        posb_ref[:PAD_BLOCKS, :] = neg_rows
        posb_ref[PAD_BLOCKS:, :] = jnp.zeros((n_sel, QT), F32)

    qs = [q_ref[g * hpg:(g + 1) * hpg].reshape(hpg * QT, HEAD_DIM) for g in range(grps)]
    sub = lax.broadcasted_iota(jnp.int32, (QT, QT), 0)
    lane = lax.broadcasted_iota(jnp.int32, (QT, QT), 1)
    valid_c = (qi * QT + lane) - (sub * CMP_STRIDE + CMP_BLOCK - 1) >= 0
    jb = lax.broadcasted_iota(jnp.int32, (n_sel, QT), 0)
    nn = lax.broadcasted_iota(jnp.int32, (n_sel, QT), 1)
    overlap_t = jnp.where((nn * CMP_STRIDE < jb * SEL_BLOCK + SEL_BLOCK)
                          & (nn * CMP_STRIDE + CMP_BLOCK > jb * SEL_BLOCK)
                          & (nn < n_cmp), 1.0, 0.0).astype(BF16)
    cur = (qi * QT + nn) // SEL_BLOCK
    forced = (jb == 0) | ((cur - jb >= 0) & (cur - jb < SEL_LOCAL))

    def scores(kpad, g, tile0, n_keys):
        start = pl.multiple_of((jnp.maximum(tile0, -1) + 1) * QT, QT)
        return _dot_nt(kpad[g, pl.ds(start, n_keys), :], qs[g])

    near = qi - 1
    mid_tile = qi - 3
    far_tile = qi - WINDOW // QT
    raw_w = [(scores(kpad_w, g, near, KCH), scores(kpad_w, g, mid_tile, KCH), scores(kpad_w, g, far_tile, QT))
             for g in range(grps)]
    raw_s = [scores(kpad_s, g, near, KCH) for g in range(grps)]

    o_cmp = []
    for g in range(grps):
        sc = _dot_nt(kc_ref[0, g], qs[g]) * scale
        p_heads = []
        p_sum = None
        for h in range(hpg):
            s = jnp.where(valid_c, sc[:, h * QT:(h + 1) * QT] + cbias_ref[g * hpg + h], NEG)
            mx = jnp.max(s, axis=0, keepdims=True)
            p = jnp.where(valid_c, jnp.exp(s - mx), 0.0)
            p = p / jnp.maximum(jnp.sum(p, axis=0, keepdims=True), 1e-30)
            p_heads.append(p.astype(BF16))
            p_sum = p if p_sum is None else p_sum + p
        o_cmp.append(_dot(vct_ref[0, g], jnp.concatenate(p_heads, axis=1)))

        hi, mid, lo = _split3(p_sum)
        imp = _dot(overlap_t, hi) + _dot(overlap_t, mid) + _dot(overlap_t, lo)
        imp = jnp.where(forced, FORCE_SCORE, imp)
        imp = jnp.where(jb <= cur, imp, -1.0)
        rank = jnp.zeros((n_sel, QT), F32)
        for i in range(n_sel):
            other = jnp.broadcast_to(imp[i:i + 1, :], (n_sel, QT))
            rank = rank + jnp.where(jb > i, jnp.where(other >= imp, 1.0, 0.0), jnp.where(other > imp, 1.0, 0.0))
        selb_ref[g, PAD_BLOCKS:, :] = jnp.where(rank < min(SEL_TOPK, n_sel), 0.0, NEG)

    def row_bias(ref, tile0, n_keys):
        r0 = PAD_BLOCKS + tile0 * blocks_per_tile
        return jnp.concatenate([jnp.broadcast_to(ref[pl.ds(r0 + b, 1), :], (SEL_BLOCK, QT))
                                for b in range(n_keys // SEL_BLOCK)], axis=0)

    def add_shared(s, bias):
        return jnp.concatenate([s[:, h * QT:(h + 1) * QT] + bias for h in range(hpg)], axis=1)

    def add_band(s, bias, g):
        return jnp.concatenate([s[:, h * QT:(h + 1) * QT] + (band_ref[g * hpg + h] + bias) for h in range(hpg)],
                               axis=1)

    def weighted_values(vt, g, tile0, p):
        t0 = jnp.maximum(tile0, -1) + 1
        pb = p.astype(BF16)
        pv = _dot(vt[g, t0], pb[:QT])
        for t in range(1, p.shape[0] // QT):
            pv = pv + _dot(vt[g, t0 + t], pb[t * QT:(t + 1) * QT])
        return pv

    to_exp2 = scale * math.log2(math.e)

    def flash_first(s, vt, g, tile0, acc_ref):
        m = jnp.max(s, axis=0, keepdims=True)
        p = jnp.exp2((s - m) * to_exp2)
        acc_ref[g] = weighted_values(vt, g, tile0, p)
        return m, jnp.sum(p, axis=0, keepdims=True)

    def flash_next(s, vt, g, tile0, acc_ref, m, l):
        m_new = jnp.maximum(m, jnp.max(s, axis=0, keepdims=True))
        alpha = jnp.exp2((m - m_new) * to_exp2)
        p = jnp.exp2((s - m_new) * to_exp2)
        acc_ref[g] = acc_ref[g] * alpha + weighted_values(vt, g, tile0, p)
        return m_new, alpha * l + jnp.sum(p, axis=0, keepdims=True)


    edge = jnp.where(sub > lane, 0.0, NEG)
    stats_w = [None] * grps
    for g in range(grps):
        s = add_band(raw_w[g][0], row_bias(posb_ref, near, KCH), g)
        stats_w[g] = flash_first(s, vt_w, g, near, acc_w)
    for g in range(grps):
        s = add_shared(raw_w[g][1], row_bias(posb_ref, mid_tile, KCH))
        stats_w[g] = flash_next(s, vt_w, g, mid_tile, acc_w, *stats_w[g])
    for g in range(grps):
        s = add_shared(raw_w[g][2], row_bias(posb_ref, far_tile, QT) + edge)
        stats_w[g] = flash_next(s, vt_w, g, far_tile, acc_w, *stats_w[g])

    stats_s = []
    for g in range(grps):
        s = add_band(raw_s[g], row_bias(selb_ref.at[g], near, KCH), g)
        stats_s.extend(flash_first(s, vt_s, g, near, acc_s))

    def sel_body(e, carry):
        tile0 = qi + 1 - 2 * (e + 1)
        raw = [scores(kpad_s, g, tile0, KCH) for g in range(grps)]
        out = []
        for g in range(grps):
            s = add_shared(raw[g], row_bias(selb_ref.at[g], tile0, KCH))
            out.extend(flash_next(s, vt_s, g, tile0, acc_s, carry[2 * g], carry[2 * g + 1]))
        return tuple(out)

    stats_s = lax.fori_loop(1, (qi + 2) // 2, sel_body, tuple(stats_s))

    for g in range(grps):
        gate = _sigmoid(gate_ref[:, g * LANES:(g + 1) * LANES].astype(F32)).T
        inv_s = 1.0 / jnp.maximum(stats_s[2 * g + 1], 1e-30)
        inv_w = 1.0 / jnp.maximum(stats_w[g][1], 1e-30)
        for h in range(hpg):
            sl = slice(h * QT, (h + 1) * QT)
            o_t = (gate[h:h + 1] * o_cmp[g][:, sl]
                   + gate[hpg + h:hpg + h + 1] * (acc_s[g, :, sl] * inv_s[:, sl])
                   + gate[2 * hpg + h:2 * hpg + h + 1] * (acc_w[g, :, sl] * inv_w[:, sl]))
            c0 = (g * hpg + h) * HEAD_DIM
            o_ref[:, c0:c0 + HEAD_DIM] = o_t.T.astype(o_ref.dtype)


def _nsa_attention(proj, cmp_kv, cmp_kv_t, gates, cmp_bias, band, *, batch, seq, kv_chunk0, gate_blk0):
    nq = seq // QT
    grps = NSA_GROUPS
    n_sel = seq // SEL_BLOCK
    rows_c = seq // CMP_STRIDE
    assert WINDOW // QT <= PAD_BLOCKS // (QT // SEL_BLOCK) and WINDOW == 4 * QT and KCH == 2 * QT
    assert kv_chunk0 % grps == 0 and gate_blk0 % grps == 0

    def kv_spec(branch, kv):
        blk = (kv_chunk0 + (branch * 2 + kv) * grps) // grps
        return pl.BlockSpec((grps, seq, HEAD_DIM), lambda b, i: (blk, b, 0))

    return pl.pallas_call(
        _nsa_kernel,
        out_shape=jax.ShapeDtypeStruct((batch * seq, NSA_HEADS * HEAD_DIM), BF16),
        grid=(batch, nq),
        in_specs=[
            pl.BlockSpec((NSA_HEADS, QT, HEAD_DIM), lambda b, i: (0, b * nq + i, 0)),
            pl.BlockSpec((1, grps, rows_c, HEAD_DIM), lambda b, i: (b, 0, 0, 0)),
            pl.BlockSpec((1, grps, HEAD_DIM, rows_c), lambda b, i: (b, 1, 0, 0)),
            kv_spec(1, 0), kv_spec(1, 1), kv_spec(2, 0), kv_spec(2, 1),
            pl.BlockSpec((QT, grps * LANES), lambda b, i: (b * nq + i, gate_blk0 // grps)),
            pl.BlockSpec((NSA_HEADS, LANES, QT), lambda b, i: (0, 0, i)),
            pl.BlockSpec((NSA_HEADS, KCH, QT), lambda b, i: (0, 0, 0)),
        ],
        out_specs=pl.BlockSpec((QT, NSA_HEADS * HEAD_DIM), lambda b, i: (b * nq + i, 0)),
        scratch_shapes=[
            pltpu.VMEM((grps, seq + QT, HEAD_DIM), BF16),
            pltpu.VMEM((grps, nq + 1, HEAD_DIM, QT), BF16),
            pltpu.VMEM((grps, seq + QT, HEAD_DIM), BF16),
            pltpu.VMEM((grps, nq + 1, HEAD_DIM, QT), BF16),
            pltpu.VMEM((grps, HEAD_DIM, NSA_HPG * QT), F32),
            pltpu.VMEM((grps, HEAD_DIM, NSA_HPG * QT), F32),
            pltpu.VMEM((grps, PAD_BLOCKS + n_sel, QT), F32),
            pltpu.VMEM((PAD_BLOCKS + n_sel, QT), F32),
        ],
        compiler_params=_cparams(("arbitrary", "arbitrary")),
        name="nsa_attention",
    )(proj, cmp_kv, cmp_kv_t, proj, proj, proj, proj, gates, cmp_bias, band)


def _retention_kernel(q_ref, k_ref, v_ref, g_ref, cos_ref, sin_ref, dmat_ref, xi_ref, zeta_ref,
                      o_ref, state_ref, *, g_chunk):
    @pl.when(pl.program_id(1) == 0)
    def _():
        state_ref[...] = jnp.zeros(state_ref.shape, F32)

    cos, sin = cos_ref[...], sin_ref[...]
    half = RET_QK_DIM // 2
    heads = range(RET_HEADS)
    qb, kr, v, inner, cross = [], [], [], [], []
    for h in heads:
        q = q_ref[h].astype(F32)
        k = k_ref[h].astype(F32)
        qb.append((q * cos + pltpu.roll(q, half, 1) * sin).astype(BF16))
        kr.append((k * cos + pltpu.roll(k, half, 1) * sin) * (RET_QK_DIM ** -0.5))
        v.append(jnp.concatenate([v_ref[2 * h], v_ref[2 * h + 1]], axis=1))
    for h in heads:
        inner.append(_dot_nt(qb[h], kr[h].astype(BF16)))
        cross.append(_dot(qb[h], state_ref[h].astype(BF16)))
    for h in heads:
        kz = (kr[h] * zeta_ref[h]).T.astype(BF16)
        state_ref[h] = state_ref[h] * g_chunk[h] + _dot(kz, v[h])
    for h in heads:
        o = _dot((inner[h] * dmat_ref[h]).astype(BF16), v[h]) + cross[h] * xi_ref[h][:, 0:1]
        mu = jnp.mean(o, axis=-1, keepdims=True)
        var = jnp.mean(jnp.square(o - mu), axis=-1, keepdims=True)
        on = (o - mu) * lax.rsqrt(var + EPS)
        gate = g_ref[:, h * RET_V_DIM:(h + 1) * RET_V_DIM].astype(F32)
        gate = gate * (1.0 / (1.0 + jnp.exp(-gate)))
        o_ref[:, h * RET_V_DIM:(h + 1) * RET_V_DIM] = (on * gate).astype(o_ref.dtype)


def _retention(proj, gates, *, batch, seq, q_blk, k_blk, v_blk):
    c = RET_CHUNK
    n = seq // c
    heads = RET_HEADS
    dk = RET_QK_DIM
    theta = 1.0 / (10000.0 ** np.linspace(0.0, 1.0, dk // 2))
    ang = np.arange(seq)[:, None] * theta[None, :]
    cos = np.concatenate([np.cos(ang), np.cos(ang)], axis=1)
    sin = np.concatenate([-np.sin(ang), np.sin(ang)], axis=1)
    log_g = np.log(1.0 - np.exp2(-5.0 - np.arange(heads)))
    j = np.arange(c)
    diff = j[:, None] - j[None, :]
    dmat = np.where(diff >= 0, np.exp(np.maximum(diff, 0)[None] * log_g[:, None, None]), 0.0)
    xi = np.broadcast_to(np.exp((j + 1.0)[None, :] * log_g[:, None])[:, :, None], (heads, c, LANES))
    zeta = np.broadcast_to(np.exp((c - 1.0 - j)[None, :] * log_g[:, None])[:, :, None], (heads, c, LANES))
    g_chunk = tuple(float(v) for v in np.exp(c * log_g))
    consts = [jnp.asarray(a, F32) for a in (cos, sin, dmat, xi, zeta)]

    full3 = lambda shape: pl.BlockSpec(shape, lambda b, i: (0, 0, 0))
    return pl.pallas_call(
        functools.partial(_retention_kernel, g_chunk=g_chunk),
        out_shape=jax.ShapeDtypeStruct((batch * seq, heads * RET_V_DIM), BF16),
        grid=(batch, n),
        in_specs=[
            pl.BlockSpec((heads, c, LANES), lambda b, i: (q_blk, b * n + i, 0)),
            pl.BlockSpec((heads, c, LANES), lambda b, i: (k_blk, b * n + i, 0)),
            pl.BlockSpec((2 * heads, c, LANES), lambda b, i: (v_blk, b * n + i, 0)),
            pl.BlockSpec((c, heads * RET_V_DIM), lambda b, i: (b * n + i, 0)),
            pl.BlockSpec((c, dk), lambda b, i: (i, 0)),
            pl.BlockSpec((c, dk), lambda b, i: (i, 0)),
            full3((heads, c, c)), full3((heads, c, LANES)), full3((heads, c, LANES)),
        ],
        out_specs=pl.BlockSpec((c, heads * RET_V_DIM), lambda b, i: (b * n + i, 0)),
        scratch_shapes=[pltpu.VMEM((heads, dk, RET_V_DIM), F32)],
        compiler_params=_cparams(("parallel", "arbitrary")),
        name="retention",
    )(proj, proj, proj, gates, *consts)


def _mem_attn_kernel(q_ref, k_ref, v_ref, o_ref):
    scale = MEM_HEAD_DIM ** -0.5
    scores = [(_dot_nt(q_ref[2 * h], k_ref[2 * h]) + _dot_nt(q_ref[2 * h + 1], k_ref[2 * h + 1])) * scale
              for h in range(MEM_HEADS)]
    for h in range(MEM_HEADS):
        s = scores[h]
        mx = jnp.max(s, axis=-1, keepdims=True)
        p = jnp.exp(s - mx)
        p = (p / jnp.sum(p, axis=-1, keepdims=True)).astype(BF16)
        for half in range(2):
            c0 = h * MEM_HEAD_DIM + half * LANES
            o_ref[:, c0:c0 + LANES] = _dot(p, v_ref[2 * h + half]).astype(o_ref.dtype)


def _mem_attention(proj, mkv, *, batch, seq, mem_len, q_blk, tq):
    nq = seq // tq
    chunks = MEM_HEADS * MEM_HEAD_DIM // LANES
    return pl.pallas_call(
        _mem_attn_kernel,
        out_shape=jax.ShapeDtypeStruct((batch * seq, MEM_HEADS * MEM_HEAD_DIM), BF16),
        grid=(batch, nq),
        in_specs=[
            pl.BlockSpec((chunks, tq, LANES), lambda b, i: (q_blk, b * nq + i, 0)),
            pl.BlockSpec((chunks, mem_len, LANES), lambda b, i: (0, b, 0)),
            pl.BlockSpec((chunks, mem_len, LANES), lambda b, i: (1, b, 0)),
        ],
        out_specs=pl.BlockSpec((tq, MEM_HEADS * MEM_HEAD_DIM), lambda b, i: (b * nq + i, 0)),
        compiler_params=_cparams(("parallel", "parallel")),
        name="mem_attention",
    )(proj, mkv, mkv)


def _sigmoid(x):
    return 1.0 / (1.0 + jnp.exp(-x))


def _merge_kernel(ya_ref, yb_ref, yc_ref, wa_ref, wb_ref, wc_ref, ga_ref, gb_ref, gc_ref, o_ref):
    out = _sigmoid(ga_ref[...].astype(F32)) * _dot(ya_ref[...], wa_ref[...])
    out = out + _sigmoid(gb_ref[...].astype(F32)) * _dot(yb_ref[...], wb_ref[...])
    out = out + _sigmoid(gc_ref[...].astype(F32)) * _dot(yc_ref[...], wc_ref[...])
    o_ref[...] = out.astype(o_ref.dtype)


def _merge(ya, yb, yc, wa, wb, wc, gates, *, d_model, gate_col0, tm, tn):
    m, kdim = ya.shape
    y_spec = pl.BlockSpec((tm, kdim), lambda i, j: (i, 0))
    w_spec = pl.BlockSpec((kdim, tn), lambda i, j: (0, j))

    def g_spec(branch):
        base = (gate_col0 + branch * d_model) // tn
        return pl.BlockSpec((tm, tn), lambda i, j: (i, base + j))

    return pl.pallas_call(
        _merge_kernel,
        out_shape=jax.ShapeDtypeStruct((m, d_model), BF16),
        grid=(m // tm, d_model // tn),
        in_specs=[y_spec, y_spec, y_spec, w_spec, w_spec, w_spec, g_spec(0), g_spec(1), g_spec(2)],
        out_specs=pl.BlockSpec((tm, tn), lambda i, j: (i, j)),
        compiler_params=_cparams(("parallel", "parallel")),
        name="branch_merge",
    )(ya, yb, yc, wa, wb, wc, gates, gates, gates)


def _out_proj_kernel(y_ref, w_ref, g_ref, x_ref, o_ref):
    o_ref[...] = x_ref[...] + _rms(_dot(y_ref[...], w_ref[...]), g_ref[...])


def _out_proj(y, w, g, x2d, *, tm):
    m, d = x2d.shape
    return pl.pallas_call(
        _out_proj_kernel,
        out_shape=jax.ShapeDtypeStruct((m, d), F32),
        grid=(m // tm,),
        in_specs=[
            pl.BlockSpec((tm, d), lambda i: (i, 0)),
            pl.BlockSpec((d, d), lambda i: (0, 0)),
            pl.BlockSpec((1, d), lambda i: (0, 0)),
            pl.BlockSpec((tm, d), lambda i: (i, 0)),
        ],
        out_specs=pl.BlockSpec((tm, d), lambda i: (i, 0)),
        compiler_params=_cparams(("parallel",)),
        name="out_proj",
    )(y, w, g, x2d)


def _gelu_tanh(x):
    return 0.5 * x * (1.0 + jnp.tanh(math.sqrt(2.0 / math.pi) * (x + 0.044715 * (x * x * x))))


def _ffn_kernel(x_ref, halo_ref, gpre_ref, wg_ref, wv_ref, cwg_ref, cwv_ref, cbg_ref, cbv_ref,
                wd_ref, gpost_ref, o_ref, h_ref, acc_ref, *, seq):
    i, j = pl.program_id(0), pl.program_id(1)
    tm = x_ref.shape[0]

    @pl.when(j == 0)
    def _():
        h_ref[HALO:, :] = _rms(x_ref[...], gpre_ref[...]).astype(BF16)
        keep = jnp.where((i * tm) % seq == 0, 0.0, 1.0)
        h_ref[:HALO, :] = (_rms(halo_ref[...], gpre_ref[...]) * keep).astype(BF16)
        acc_ref[...] = jnp.zeros(acc_ref.shape, F32)

    h = h_ref[...]

    def conv(u, w_ref, b_ref):
        w = w_ref[...]
        y = b_ref[...] + w[2:3] * u[HALO:]
        y = y + w[1:2] * pltpu.roll(u, 1, 0)[HALO:]
        return y + w[0:1] * pltpu.roll(u, 2, 0)[HALO:]

    yg = conv(_dot(h, wg_ref[...]), cwg_ref, cbg_ref)
    yv = conv(_dot(h, wv_ref[...]), cwv_ref, cbv_ref)
    act = (_gelu_tanh(yg) * yv).astype(BF16)
    acc_ref[...] += _dot(act, wd_ref[...])

    @pl.when(j == pl.num_programs(1) - 1)
    def _():
        o_ref[...] = x_ref[...] + _rms(acc_ref[...], gpost_ref[...])


def _ffn(x2d, gpre, w_up, conv_w, conv_b, w_down, gpost, *, seq, tm, tn):
    m, d = x2d.shape
    d_ff = w_down.shape[0]
    nj = d_ff // tn
    halo_blocks = tm // HALO
    return pl.pallas_call(
        functools.partial(_ffn_kernel, seq=seq),
        out_shape=jax.ShapeDtypeStruct((m, d), F32),
        grid=(m // tm, nj),
        in_specs=[
            pl.BlockSpec((tm, d), lambda i, j: (i, 0)),
            pl.BlockSpec((HALO, d), lambda i, j: (jnp.maximum(i * halo_blocks - 1, 0), 0)),
            pl.BlockSpec((1, d), lambda i, j: (0, 0)),
            pl.BlockSpec((d, tn), lambda i, j: (0, j)),
            pl.BlockSpec((d, tn), lambda i, j: (0, nj + j)),
            pl.BlockSpec((CONV_WIDTH, tn), lambda i, j: (0, j)),
            pl.BlockSpec((CONV_WIDTH, tn), lambda i, j: (0, nj + j)),
            pl.BlockSpec((1, tn), lambda i, j: (0, j)),
            pl.BlockSpec((1, tn), lambda i, j: (0, nj + j)),
            pl.BlockSpec((tn, d), lambda i, j: (j, 0)),
            pl.BlockSpec((1, d), lambda i, j: (0, 0)),
        ],
        out_specs=pl.BlockSpec((tm, d), lambda i, j: (i, 0)),
        scratch_shapes=[pltpu.VMEM((tm + HALO, d), BF16), pltpu.VMEM((tm, d), F32)],
        compiler_params=_cparams(("parallel", "arbitrary")),
        name="conv_ffn",
    )(x2d, x2d, gpre, w_up, w_up, conv_w, conv_w, conv_b, conv_b, w_down, gpost)


def _pick(n, prefs):
    for t in prefs:
        if n % t == 0:
            return t
    raise ValueError(f"no tile in {prefs} divides {n}")


class _Tiles(NamedTuple):
    proj_rows: int
    proj_cols: int
    gate_cols: int
    merge_cols: int
    out_rows: int
    ffn_rows: int
    ffn_cols: int
    mem_q_rows: int


def _tiles(m):
    return _Tiles(proj_rows=_pick(m, (1024, 512, 256, 128)), proj_cols=2 * MXU_WIDTH, gate_cols=3 * MXU_WIDTH,
                  merge_cols=4 * MXU_WIDTH,
                  out_rows=_pick(m, (512, 256, 128)), ffn_rows=_pick(m, (512, 256, 128)), ffn_cols=2 * MXU_WIDTH,
                  mem_q_rows=512)


def kernel(x, mem, w_in, cmp_pe_k, cmp_w1_k, cmp_w2_k, cmp_pe_v, cmp_w1_v, cmp_w2_v, rel_bias, w_mem_kv,
           w_br_nsa, w_br_ret, w_br_mem, w_o, w_up, conv_w, conv_b, w_down, g_pre_mix, g_post_mix, g_mem,
           g_pre_ffn, g_post_ffn):
    batch, seq, d_model = x.shape
    mem_len = mem.shape[1]
    depth = w_in.shape[0]
    m = batch * seq
    grps, hpg, dh = NSA_GROUPS, NSA_HPG, HEAD_DIM

    nsa_q = NSA_HEADS * dh
    nsa_kv = 3 * 2 * grps * dh
    nsa_gates = 3 * NSA_HEADS
    ret_qk = RET_HEADS * RET_QK_DIM
    ret_v = RET_HEADS * RET_V_DIM
    mem_q = MEM_HEADS * MEM_HEAD_DIM
    splits = (nsa_q, nsa_kv, nsa_gates, ret_qk, ret_qk, ret_v, ret_v, mem_q, 3 * d_model)
    offs = np.concatenate([[0], np.cumsum(splits)])
    o_nq, o_nkv, o_ng, o_rq, o_rk, o_rv, o_rg, o_mq, o_bg = (int(v) for v in offs[:-1])

    q_chunk0 = 0
    rv_chunk0 = q_chunk0 + nsa_q // LANES
    mq_chunk0 = rv_chunk0 + ret_v // LANES
    rq_chunk0 = mq_chunk0 + mem_q // LANES
    rk_chunk0 = rq_chunk0 + ret_qk // LANES
    kv_chunk0 = rk_chunk0 + ret_qk // LANES
    bg_col0 = ret_v
    ng_col0 = bg_col0 + 3 * d_model
    tiles = _tiles(m)
    gate_cols = -(-(ng_col0 + grps * LANES) // tiles.gate_cols) * tiles.gate_cols

    tbl_flat = rel_bias.reshape(-1)
    cmp_bias, band = _bias_tables(tbl_flat, seq)
    x2d = x.reshape(m, d_model)

    for l in range(depth):
        w = w_in[l]
        cols = lambda o, n: w[:, o:o + n]
        w_attn = jnp.concatenate([cols(o_nq, nsa_q), cols(o_rv, ret_v), cols(o_mq, mem_q), cols(o_rq, ret_qk),
                                  cols(o_rk, ret_qk), cols(o_nkv, nsa_kv)], axis=1).astype(BF16)
        ng = cols(o_ng, nsa_gates).reshape(d_model, 3, grps, hpg).transpose(0, 2, 1, 3).reshape(d_model, grps, 3 * hpg)
        ng = jnp.pad(ng, ((0, 0), (0, 0), (0, LANES - 3 * hpg))).reshape(d_model, grps * LANES)
        w_gate = jnp.concatenate([cols(o_rg, ret_v), cols(o_bg, 3 * d_model), ng], axis=1)
        w_gate = jnp.pad(w_gate, ((0, 0), (0, gate_cols - w_gate.shape[1]))).astype(BF16)

        g_pre = g_pre_mix[l].reshape(1, d_model)
        proj = _norm_proj(x2d, g_pre, w_attn, head_major=True, tm=tiles.proj_rows, tn=tiles.proj_cols,
                          name="in_proj_heads")
        gates = _norm_proj(x2d, g_pre, w_gate, head_major=False, tm=tiles.proj_rows, tn=tiles.gate_cols,
                           name="in_proj_gates")

        rows = seq // CMP_STRIDE
        cmp_rows = proj[kv_chunk0:kv_chunk0 + 2 * grps].reshape(2 * grps, batch, rows, CMP_STRIDE * dh)
        pe = jnp.stack([cmp_pe_k[l], cmp_pe_v[l]]).reshape(2, 2, CMP_STRIDE * dh)
        w1 = jnp.stack([cmp_w1_k[l], cmp_w1_v[l]]).astype(BF16)
        w2 = jnp.stack([cmp_w2_k[l], cmp_w2_v[l]]).astype(BF16)
        cmp_kv, cmp_kv_t = _compress(cmp_rows, pe, w1, w2, batch)
        y_a = _nsa_attention(proj, cmp_kv, cmp_kv_t, gates, cmp_bias, band, batch=batch, seq=seq,
                             kv_chunk0=kv_chunk0, gate_blk0=ng_col0 // LANES)

        y_b = _retention(proj, gates, batch=batch, seq=seq, q_blk=rq_chunk0 // RET_HEADS,
                         k_blk=rk_chunk0 // RET_HEADS, v_blk=rv_chunk0 // (2 * RET_HEADS))

        mem2d = mem.reshape(batch * mem_len, d_model)
        mkv = _norm_proj(mem2d, g_mem[l].reshape(1, d_model), w_mem_kv[l].astype(BF16), head_major=True,
                         tm=_tiles(batch * mem_len).proj_rows, tn=tiles.proj_cols, name="mem_kv_proj")
        y_c = _mem_attention(proj, mkv, batch=batch, seq=seq, mem_len=mem_len,
                             q_blk=mq_chunk0 // (mem_q // LANES), tq=tiles.mem_q_rows)

        merged = _merge(y_a, y_b, y_c, w_br_nsa[l].astype(BF16), w_br_ret[l].astype(BF16),
                        w_br_mem[l].astype(BF16), gates, d_model=d_model, gate_col0=bg_col0,
                        tm=tiles.proj_rows, tn=tiles.merge_cols)
        x2d = _out_proj(merged, w_o[l].astype(BF16), g_post_mix[l].reshape(1, d_model), x2d, tm=tiles.out_rows)

        x2d = _ffn(x2d, g_pre_ffn[l].reshape(1, d_model), w_up[l].astype(BF16), conv_w[l],
                   conv_b[l].reshape(1, -1), w_down[l].astype(BF16), g_post_ffn[l].reshape(1, d_model),
                   seq=seq, tm=tiles.ffn_rows, tn=tiles.ffn_cols)
    return x2d.reshape(batch, seq, d_model)
```

```python
import functools
import math
from typing import NamedTuple

import jax
import jax.numpy as jnp
import numpy as np
from jax import lax
from jax.experimental import pallas as pl
from jax.experimental.pallas import tpu as pltpu

F32 = jnp.float32
BF16 = jnp.bfloat16

LANES = 128
MXU_WIDTH = 256
EPS = 1e-6
NEG = -1e30

NSA_HEADS = 8
NSA_GROUPS = 2
NSA_HPG = NSA_HEADS // NSA_GROUPS
HEAD_DIM = 128
CMP_BLOCK = 32
CMP_STRIDE = 16
CMP_HIDDEN = 256
SEL_BLOCK = 64
SEL_TOPK = 16
SEL_LOCAL = 2
FORCE_SCORE = 1e4
WINDOW = 512
RET_HEADS = 4
RET_QK_DIM = 128
RET_V_DIM = 256
RET_CHUNK = 128
MEM_HEADS = 4
MEM_HEAD_DIM = 256
REL_BUCKETS = 32
REL_MAX_DIST = 128
CONV_WIDTH = 3

QT = 128
KCH = 2 * QT
PAD_BLOCKS = 8
HALO = 16

VMEM_LIMIT = 56 * 1024 * 1024


def _cparams(sem, vmem=VMEM_LIMIT):
    return pltpu.CompilerParams(dimension_semantics=sem, vmem_limit_bytes=vmem)


def _dot(a, b):
    return jnp.dot(a, b, preferred_element_type=F32)


def _dot_nt(a, b):
    return lax.dot_general(a, b, (((1,), (1,)), ((), ())), preferred_element_type=F32)


def _rms(x, g):
    ms = jnp.mean(x * x, axis=-1, keepdims=True)
    return x * lax.rsqrt(ms + EPS) * g


def _norm_proj_kernel(x_ref, g_ref, w_ref, o_ref, h_ref):
    @pl.when(pl.program_id(1) == 0)
    def _():
        h_ref[...] = _rms(x_ref[...], g_ref[...]).astype(BF16)

    acc = _dot(h_ref[...], w_ref[...])
    for c in range(acc.shape[1] // LANES):
        o_ref[c] = acc[:, c * LANES:(c + 1) * LANES].astype(o_ref.dtype)


def _norm_proj(x2d, g, w, *, tm, tn, name):
    m, d = x2d.shape
    n = w.shape[1]
    return pl.pallas_call(
        _norm_proj_kernel,
        out_shape=(jax.ShapeDtypeStruct((n // LANES, m, LANES), BF16), jax.ShapeDtypeStruct((m, d), BF16)),
        grid=(m // tm, n // tn),
        in_specs=[
            pl.BlockSpec((tm, d), lambda i, j: (i, 0)),
            pl.BlockSpec((1, d), lambda i, j: (0, 0)),
            pl.BlockSpec((d, tn), lambda i, j: (0, j)),
        ],
        out_specs=(pl.BlockSpec((tn // LANES, tm, LANES), lambda i, j: (j, i, 0)),
                   pl.BlockSpec((tm, d), lambda i, j: (i, 0))),
        compiler_params=_cparams(("parallel", "arbitrary")),
        name=name,
    )(x2d, g, w)


def _proj_kernel(h_ref, w_ref, o_ref):
    o_ref[...] = _dot(h_ref[...], w_ref[...]).astype(o_ref.dtype)


def _proj(h, w, *, tm, tn, name):
    m, d = h.shape
    n = w.shape[1]
    return pl.pallas_call(
        _proj_kernel,
        out_shape=jax.ShapeDtypeStruct((m, n), BF16),
        grid=(m // tm, n // tn),
        in_specs=[pl.BlockSpec((tm, d), lambda i, j: (i, 0)), pl.BlockSpec((d, tn), lambda i, j: (0, j))],
        out_specs=pl.BlockSpec((tm, tn), lambda i, j: (i, j)),
        compiler_params=_cparams(("parallel", "parallel")),
        name=name,
    )(h, w)


def _compress_kernel(x_ref, pe_ref, w1_ref, w2_ref, o_ref, ot_ref):
    half = CMP_STRIDE * HEAD_DIM
    x = x_ref[0, 0].astype(F32)
    pe = pe_ref[0]
    a = _dot((x + pe[0:1]).astype(BF16), w1_ref[0, :half, :])
    b = _dot((x + pe[1:2]).astype(BF16), w1_ref[0, half:, :])
    rows = x.shape[0]
    hid = a + pltpu.roll(b, rows - 1, 0)
    hid = hid * (1.0 / (1.0 + jnp.exp(-hid)))
    out = _dot(hid.astype(BF16), w2_ref[0])
    o_ref[0, 0] = out.astype(o_ref.dtype)
    ot_ref[0, 0] = out.T.astype(ot_ref.dtype)


def _compress(cmp_rows, pe, w1, w2, batch):
    _, _, rows, width = cmp_rows.shape
    n_out = 2 * NSA_GROUPS
    return pl.pallas_call(
        _compress_kernel,
        out_shape=(jax.ShapeDtypeStruct((batch, n_out, rows, HEAD_DIM), BF16),
                   jax.ShapeDtypeStruct((batch, n_out, HEAD_DIM, rows), BF16)),
        grid=(batch, n_out),
        in_specs=[
            pl.BlockSpec((1, 1, rows, width), lambda b, c: (c, b, 0, 0)),
            pl.BlockSpec((1, 2, width), lambda b, c: (c // NSA_GROUPS, 0, 0)),
            pl.BlockSpec((1, 2 * width, CMP_HIDDEN), lambda b, c: (c // NSA_GROUPS, 0, 0)),
            pl.BlockSpec((1, CMP_HIDDEN, HEAD_DIM), lambda b, c: (c // NSA_GROUPS, 0, 0)),
        ],
        out_specs=(pl.BlockSpec((1, 1, rows, HEAD_DIM), lambda b, c: (b, c, 0, 0)),
                   pl.BlockSpec((1, 1, HEAD_DIM, rows), lambda b, c: (b, c, 0, 0))),
        compiler_params=_cparams(("parallel", "parallel")),
        name="nsa_compress",
    )(cmp_rows, pe, w1, w2)


def _bias_from_rel(rel, tbl_ref, head):
    max_exact = REL_BUCKETS // 2
    n = jnp.maximum(rel, 0)
    nf = jnp.maximum(n, 1).astype(F32)
    large = max_exact + (jnp.log(nf / max_exact) / math.log(REL_MAX_DIST / max_exact)
                         * (REL_BUCKETS - max_exact)).astype(jnp.int32)
    large = jnp.minimum(large, REL_BUCKETS - 1)
    bucket = jnp.where(n < max_exact, n, large)
    out = jnp.zeros(rel.shape, F32)
    for b in range(REL_BUCKETS):
        out = jnp.where(bucket == b, tbl_ref[b * NSA_HEADS + head], out)
    return out


def _cmp_bias_kernel(tbl_ref, o_ref):
    head, blk = pl.program_id(0), pl.program_id(1)
    rows, cols = o_ref.shape[1], o_ref.shape[2]
    n = lax.broadcasted_iota(jnp.int32, (rows, cols), 0)
    t = blk * cols + lax.broadcasted_iota(jnp.int32, (rows, cols), 1)
    o_ref[0] = _bias_from_rel(t - (n * CMP_STRIDE + CMP_BLOCK - 1), tbl_ref, head)


def _band_bias_kernel(tbl_ref, o_ref):
    head = pl.program_id(0)
    rows, cols = o_ref.shape[1], o_ref.shape[2]
    c = lax.broadcasted_iota(jnp.int32, (rows, cols), 0)
    i = lax.broadcasted_iota(jnp.int32, (rows, cols), 1)
    rel = i - c + QT
    far = tbl_ref[(REL_BUCKETS - 1) * NSA_HEADS + head]
    o_ref[0] = jnp.where(rel >= 0, (_bias_from_rel(rel, tbl_ref, head) - far) * (HEAD_DIM ** 0.5), NEG)


def _bias_tables(tbl_flat, seq):
    cols = 256
    smem = pl.BlockSpec(memory_space=pltpu.SMEM)
    cmp_bias = pl.pallas_call(
        _cmp_bias_kernel,
        out_shape=jax.ShapeDtypeStruct((NSA_HEADS, LANES, seq), F32),
        grid=(NSA_HEADS, seq // cols),
        in_specs=[smem],
        out_specs=pl.BlockSpec((1, LANES, cols), lambda h, r: (h, 0, r)),
        compiler_params=_cparams(("parallel", "parallel")),
        name="cmp_bias",
    )(tbl_flat)
    band = pl.pallas_call(
        _band_bias_kernel,
        out_shape=jax.ShapeDtypeStruct((NSA_HEADS, 2 * QT, QT), F32),
        grid=(NSA_HEADS,),
        in_specs=[smem],
        out_specs=pl.BlockSpec((1, 2 * QT, QT), lambda h: (h, 0, 0)),
        compiler_params=_cparams(("parallel",)),
        name="band_bias",
    )(tbl_flat)
    return cmp_bias, band


def _split3(x):
    hi = x.astype(BF16)
    r1 = x - hi.astype(F32)
    mid = r1.astype(BF16)
    lo = (r1 - mid.astype(F32)).astype(BF16)
    return hi, mid, lo


def _nsa_kernel(q_ref, kc_ref, vct_ref, ks_ref, vs_ref, kw_ref, vw_ref, gate_ref, cbias_ref, band_ref,
                o_ref, kpad_s, vt_s, kpad_w, vt_w, acc_s, acc_w, selb_ref, posb_ref):
    qi = pl.program_id(1)
    hpg, grps = NSA_HPG, NSA_GROUPS
    scale = HEAD_DIM ** -0.5
    n_tiles = vt_s.shape[1] - 1
    seq = n_tiles * QT
    n_sel = seq // SEL_BLOCK
    n_cmp = (seq - CMP_BLOCK) // CMP_STRIDE + 1
    blocks_per_tile = QT // SEL_BLOCK

    @pl.when(qi == 0)
    def _():
        for kpad, vt, k_ref, v_ref in ((kpad_s, vt_s, ks_ref, vs_ref), (kpad_w, vt_w, kw_ref, vw_ref)):
            for g in range(grps):
                kpad[g, :QT, :] = jnp.zeros((QT, HEAD_DIM), BF16)
                kpad[g, QT:, :] = k_ref[g]
                vt[g, 0] = jnp.zeros((HEAD_DIM, QT), BF16)
                for t in range(n_tiles):
                    vt[g, t + 1] = v_ref[g, t * QT:(t + 1) * QT, :].astype(F32).T.astype(BF16)
        neg_rows = jnp.full((PAD_BLOCKS, QT), NEG, F32)
        for g in range(grps):
            selb_ref[g, :PAD_BLOCKS, :] = neg_rows
        posb_ref[:PAD_BLOCKS, :] = neg_rows
        posb_ref[PAD_BLOCKS:, :] = jnp.zeros((n_sel, QT), F32)

    qs = [q_ref[g * hpg:(g + 1) * hpg].reshape(hpg * QT, HEAD_DIM) for g in range(grps)]
    sub = lax.broadcasted_iota(jnp.int32, (QT, QT), 0)
    lane = lax.broadcasted_iota(jnp.int32, (QT, QT), 1)
    valid_c = (qi * QT + lane) - (sub * CMP_STRIDE + CMP_BLOCK - 1) >= 0
    jb = lax.broadcasted_iota(jnp.int32, (n_sel, QT), 0)
    nn = lax.broadcasted_iota(jnp.int32, (n_sel, QT), 1)
    overlap_t = jnp.where((nn * CMP_STRIDE < jb * SEL_BLOCK + SEL_BLOCK)
                          & (nn * CMP_STRIDE + CMP_BLOCK > jb * SEL_BLOCK)
                          & (nn < n_cmp), 1.0, 0.0).astype(BF16)
    cur = (qi * QT + nn) // SEL_BLOCK
    forced = (jb == 0) | ((cur - jb >= 0) & (cur - jb < SEL_LOCAL))

    def scores(kpad, g, tile0, n_keys):
        start = pl.multiple_of((jnp.maximum(tile0, -1) + 1) * QT, QT)
        return _dot_nt(kpad[g, pl.ds(start, n_keys), :], qs[g])

    near = qi - 1
    mid_tile = qi - 3
    far_tile = qi - WINDOW // QT
    raw_w = [(scores(kpad_w, g, near, KCH), scores(kpad_w, g, mid_tile, KCH), scores(kpad_w, g, far_tile, QT))
             for g in range(grps)]
    raw_s = [scores(kpad_s, g, near, KCH) for g in range(grps)]

    o_cmp = []
    for g in range(grps):
        sc = _dot_nt(kc_ref[0, g], qs[g]) * scale
        p_heads = []
        p_sum = None
        for h in range(hpg):
            s = jnp.where(valid_c, sc[:, h * QT:(h + 1) * QT] + cbias_ref[g * hpg + h], NEG)
            mx = jnp.max(s, axis=0, keepdims=True)
            p = jnp.where(valid_c, jnp.exp(s - mx), 0.0)
            p = p / jnp.maximum(jnp.sum(p, axis=0, keepdims=True), 1e-30)
            p_heads.append(p.astype(BF16))
            p_sum = p if p_sum is None else p_sum + p
        o_cmp.append(_dot(vct_ref[0, g], jnp.concatenate(p_heads, axis=1)))

        hi, mid, lo = _split3(p_sum)
        imp = _dot(overlap_t, hi) + _dot(overlap_t, mid) + _dot(overlap_t, lo)
        imp = jnp.where(forced, FORCE_SCORE, imp)
        imp = jnp.where(jb <= cur, imp, -1.0)
        rank = jnp.zeros((n_sel, QT), F32)
        for i in range(n_sel):
            other = jnp.broadcast_to(imp[i:i + 1, :], (n_sel, QT))
            rank = rank + jnp.where(jb > i, jnp.where(other >= imp, 1.0, 0.0), jnp.where(other > imp, 1.0, 0.0))
        selb_ref[g, PAD_BLOCKS:, :] = jnp.where(rank < min(SEL_TOPK, n_sel), 0.0, NEG)

    def row_bias(ref, tile0, n_keys):
        r0 = PAD_BLOCKS + tile0 * blocks_per_tile
        return jnp.concatenate([jnp.broadcast_to(ref[pl.ds(r0 + b, 1), :], (SEL_BLOCK, QT))
                                for b in range(n_keys // SEL_BLOCK)], axis=0)

    def add_shared(s, bias):
        return jnp.concatenate([s[:, h * QT:(h + 1) * QT] + bias for h in range(hpg)], axis=1)

    def add_band(s, bias, g):
        return jnp.concatenate([s[:, h * QT:(h + 1) * QT] + (band_ref[g * hpg + h] + bias) for h in range(hpg)],
                               axis=1)

    def weighted_values(vt, g, tile0, p):
        t0 = jnp.maximum(tile0, -1) + 1
        pb = p.astype(BF16)
        pv = _dot(vt[g, t0], pb[:QT])
        for t in range(1, p.shape[0] // QT):
            pv = pv + _dot(vt[g, t0 + t], pb[t * QT:(t + 1) * QT])
        return pv

    to_exp2 = scale * math.log2(math.e)

    def flash_first(s, vt, g, tile0, acc_ref):
        m = jnp.max(s, axis=0, keepdims=True)
        p = jnp.exp2((s - m) * to_exp2)
        acc_ref[g] = weighted_values(vt, g, tile0, p)
        return m, jnp.sum(p, axis=0, keepdims=True)

    def flash_next(s, vt, g, tile0, acc_ref, m, l):
        m_new = jnp.maximum(m, jnp.max(s, axis=0, keepdims=True))
        alpha = jnp.exp2((m - m_new) * to_exp2)
        p = jnp.exp2((s - m_new) * to_exp2)
        acc_ref[g] = acc_ref[g] * alpha + weighted_values(vt, g, tile0, p)
        return m_new, alpha * l + jnp.sum(p, axis=0, keepdims=True)


    edge = jnp.where(sub > lane, 0.0, NEG)
    stats_w = [None] * grps
    for g in range(grps):
        s = add_band(raw_w[g][0], row_bias(posb_ref, near, KCH), g)
        stats_w[g] = flash_first(s, vt_w, g, near, acc_w)
    for g in range(grps):
        s = add_shared(raw_w[g][1], row_bias(posb_ref, mid_tile, KCH))
        stats_w[g] = flash_next(s, vt_w, g, mid_tile, acc_w, *stats_w[g])
    for g in range(grps):
        s = add_shared(raw_w[g][2], row_bias(posb_ref, far_tile, QT) + edge)
        stats_w[g] = flash_next(s, vt_w, g, far_tile, acc_w, *stats_w[g])

    stats_s = []
    for g in range(grps):
        s = add_band(raw_s[g], row_bias(selb_ref.at[g], near, KCH), g)
        stats_s.extend(flash_first(s, vt_s, g, near, acc_s))

    def sel_body(e, carry):
        tile0 = qi + 1 - 2 * (e + 1)
        raw = [scores(kpad_s, g, tile0, KCH) for g in range(grps)]
        out = []
        for g in range(grps):
            s = add_shared(raw[g], row_bias(selb_ref.at[g], tile0, KCH))
            out.extend(flash_next(s, vt_s, g, tile0, acc_s, carry[2 * g], carry[2 * g + 1]))
        return tuple(out)

    stats_s = lax.fori_loop(1, (qi + 2) // 2, sel_body, tuple(stats_s))

    for g in range(grps):
        gate = _sigmoid(gate_ref[:, g * LANES:(g + 1) * LANES].astype(F32)).T
        inv_s = 1.0 / jnp.maximum(stats_s[2 * g + 1], 1e-30)
        inv_w = 1.0 / jnp.maximum(stats_w[g][1], 1e-30)
        for h in range(hpg):
            sl = slice(h * QT, (h + 1) * QT)
            o_t = (gate[h:h + 1] * o_cmp[g][:, sl]
                   + gate[hpg + h:hpg + h + 1] * (acc_s[g, :, sl] * inv_s[:, sl])
                   + gate[2 * hpg + h:2 * hpg + h + 1] * (acc_w[g, :, sl] * inv_w[:, sl]))
            c0 = (g * hpg + h) * HEAD_DIM
            o_ref[:, c0:c0 + HEAD_DIM] = o_t.T.astype(o_ref.dtype)


def _nsa_attention(proj, cmp_kv, cmp_kv_t, gates, cmp_bias, band, *, batch, seq, kv_chunk0, gate_blk0):
    nq = seq // QT
    grps = NSA_GROUPS
    n_sel = seq // SEL_BLOCK
    rows_c = seq // CMP_STRIDE
    assert WINDOW // QT <= PAD_BLOCKS // (QT // SEL_BLOCK) and WINDOW == 4 * QT and KCH == 2 * QT
    assert kv_chunk0 % grps == 0 and gate_blk0 % grps == 0

    def kv_spec(branch, kv):
        blk = (kv_chunk0 + (branch * 2 + kv) * grps) // grps
        return pl.BlockSpec((grps, seq, HEAD_DIM), lambda b, i: (blk, b, 0))

    return pl.pallas_call(
        _nsa_kernel,
        out_shape=jax.ShapeDtypeStruct((batch * seq, NSA_HEADS * HEAD_DIM), BF16),
        grid=(batch, nq),
        in_specs=[
            pl.BlockSpec((NSA_HEADS, QT, HEAD_DIM), lambda b, i: (0, b * nq + i, 0)),
            pl.BlockSpec((1, grps, rows_c, HEAD_DIM), lambda b, i: (b, 0, 0, 0)),
            pl.BlockSpec((1, grps, HEAD_DIM, rows_c), lambda b, i: (b, 1, 0, 0)),
            kv_spec(1, 0), kv_spec(1, 1), kv_spec(2, 0), kv_spec(2, 1),
            pl.BlockSpec((QT, grps * LANES), lambda b, i: (b * nq + i, gate_blk0 // grps)),
            pl.BlockSpec((NSA_HEADS, LANES, QT), lambda b, i: (0, 0, i)),
            pl.BlockSpec((NSA_HEADS, KCH, QT), lambda b, i: (0, 0, 0)),
        ],
        out_specs=pl.BlockSpec((QT, NSA_HEADS * HEAD_DIM), lambda b, i: (b * nq + i, 0)),
        scratch_shapes=[
            pltpu.VMEM((grps, seq + QT, HEAD_DIM), BF16),
            pltpu.VMEM((grps, nq + 1, HEAD_DIM, QT), BF16),
            pltpu.VMEM((grps, seq + QT, HEAD_DIM), BF16),
            pltpu.VMEM((grps, nq + 1, HEAD_DIM, QT), BF16),
            pltpu.VMEM((grps, HEAD_DIM, NSA_HPG * QT), F32),
            pltpu.VMEM((grps, HEAD_DIM, NSA_HPG * QT), F32),
            pltpu.VMEM((grps, PAD_BLOCKS + n_sel, QT), F32),
            pltpu.VMEM((PAD_BLOCKS + n_sel, QT), F32),
        ],
        compiler_params=_cparams(("arbitrary", "arbitrary")),
        name="nsa_attention",
    )(proj, cmp_kv, cmp_kv_t, proj, proj, proj, proj, gates, cmp_bias, band)


def _retention_kernel(q_ref, k_ref, v_ref, g_ref, cos_ref, sin_ref, dmat_ref, xi_ref, zeta_ref,
                      o_ref, state_ref, *, g_chunk):
    @pl.when(pl.program_id(1) == 0)
    def _():
        state_ref[...] = jnp.zeros(state_ref.shape, F32)

    cos, sin = cos_ref[...], sin_ref[...]
    half = RET_QK_DIM // 2
    heads = range(RET_HEADS)
    qb, kr, v, inner, cross = [], [], [], [], []
    for h in heads:
        q = q_ref[h].astype(F32)
        k = k_ref[h].astype(F32)
        qb.append((q * cos + pltpu.roll(q, half, 1) * sin).astype(BF16))
        kr.append((k * cos + pltpu.roll(k, half, 1) * sin) * (RET_QK_DIM ** -0.5))
        v.append(jnp.concatenate([v_ref[2 * h], v_ref[2 * h + 1]], axis=1))
    for h in heads:
        inner.append(_dot_nt(qb[h], kr[h].astype(BF16)))
        cross.append(_dot(qb[h], state_ref[h].astype(BF16)))
    for h in heads:
        kz = (kr[h] * zeta_ref[h]).T.astype(BF16)
        state_ref[h] = state_ref[h] * g_chunk[h] + _dot(kz, v[h])
    for h in heads:
        o = _dot((inner[h] * dmat_ref[h]).astype(BF16), v[h]) + cross[h] * xi_ref[h][:, 0:1]
        mu = jnp.mean(o, axis=-1, keepdims=True)
        var = jnp.mean(jnp.square(o - mu), axis=-1, keepdims=True)
        on = (o - mu) * lax.rsqrt(var + EPS)
        gate = g_ref[:, h * RET_V_DIM:(h + 1) * RET_V_DIM].astype(F32)
        gate = gate * (1.0 / (1.0 + jnp.exp(-gate)))
        o_ref[:, h * RET_V_DIM:(h + 1) * RET_V_DIM] = (on * gate).astype(o_ref.dtype)


def _retention(proj, gates, *, batch, seq, q_blk, k_blk, v_blk):
    c = RET_CHUNK
    n = seq // c
    heads = RET_HEADS
    dk = RET_QK_DIM
    theta = 1.0 / (10000.0 ** np.linspace(0.0, 1.0, dk // 2))
    ang = np.arange(seq)[:, None] * theta[None, :]
    cos = np.concatenate([np.cos(ang), np.cos(ang)], axis=1)
    sin = np.concatenate([-np.sin(ang), np.sin(ang)], axis=1)
    log_g = np.log(1.0 - np.exp2(-5.0 - np.arange(heads)))
    j = np.arange(c)
    diff = j[:, None] - j[None, :]
    dmat = np.where(diff >= 0, np.exp(np.maximum(diff, 0)[None] * log_g[:, None, None]), 0.0)
    xi = np.broadcast_to(np.exp((j + 1.0)[None, :] * log_g[:, None])[:, :, None], (heads, c, LANES))
    zeta = np.broadcast_to(np.exp((c - 1.0 - j)[None, :] * log_g[:, None])[:, :, None], (heads, c, LANES))
    g_chunk = tuple(float(v) for v in np.exp(c * log_g))
    consts = [jnp.asarray(a, F32) for a in (cos, sin, dmat, xi, zeta)]

    full3 = lambda shape: pl.BlockSpec(shape, lambda b, i: (0, 0, 0))
    return pl.pallas_call(
        functools.partial(_retention_kernel, g_chunk=g_chunk),
        out_shape=jax.ShapeDtypeStruct((batch * seq, heads * RET_V_DIM), BF16),
        grid=(batch, n),
        in_specs=[
            pl.BlockSpec((heads, c, LANES), lambda b, i: (q_blk, b * n + i, 0)),
            pl.BlockSpec((heads, c, LANES), lambda b, i: (k_blk, b * n + i, 0)),
            pl.BlockSpec((2 * heads, c, LANES), lambda b, i: (v_blk, b * n + i, 0)),
            pl.BlockSpec((c, heads * RET_V_DIM), lambda b, i: (b * n + i, 0)),
            pl.BlockSpec((c, dk), lambda b, i: (i, 0)),
            pl.BlockSpec((c, dk), lambda b, i: (i, 0)),
            full3((heads, c, c)), full3((heads, c, LANES)), full3((heads, c, LANES)),
        ],
        out_specs=pl.BlockSpec((c, heads * RET_V_DIM), lambda b, i: (b * n + i, 0)),
        scratch_shapes=[pltpu.VMEM((heads, dk, RET_V_DIM), F32)],
        compiler_params=_cparams(("parallel", "arbitrary")),
        name="retention",
    )(proj, proj, proj, gates, *consts)


def _mem_attn_kernel(q_ref, k_ref, v_ref, o_ref):
    scale = MEM_HEAD_DIM ** -0.5
    scores = [(_dot_nt(q_ref[2 * h], k_ref[2 * h]) + _dot_nt(q_ref[2 * h + 1], k_ref[2 * h + 1])) * scale
              for h in range(MEM_HEADS)]
    for h in range(MEM_HEADS):
        s = scores[h]
        mx = jnp.max(s, axis=-1, keepdims=True)
        p = jnp.exp(s - mx)
        p = (p / jnp.sum(p, axis=-1, keepdims=True)).astype(BF16)
        for half in range(2):
            c0 = h * MEM_HEAD_DIM + half * LANES
            o_ref[:, c0:c0 + LANES] = _dot(p, v_ref[2 * h + half]).astype(o_ref.dtype)


def _mem_attention(proj, mkv, *, batch, seq, mem_len, q_blk, tq):
    nq = seq // tq
    chunks = MEM_HEADS * MEM_HEAD_DIM // LANES
    return pl.pallas_call(
        _mem_attn_kernel,
        out_shape=jax.ShapeDtypeStruct((batch * seq, MEM_HEADS * MEM_HEAD_DIM), BF16),
        grid=(batch, nq),
        in_specs=[
            pl.BlockSpec((chunks, tq, LANES), lambda b, i: (q_blk, b * nq + i, 0)),
            pl.BlockSpec((chunks, mem_len, LANES), lambda b, i: (0, b, 0)),
            pl.BlockSpec((chunks, mem_len, LANES), lambda b, i: (1, b, 0)),
        ],
        out_specs=pl.BlockSpec((tq, MEM_HEADS * MEM_HEAD_DIM), lambda b, i: (b * nq + i, 0)),
        compiler_params=_cparams(("parallel", "parallel")),
        name="mem_attention",
    )(proj, mkv, mkv)


def _sigmoid(x):
    return 1.0 / (1.0 + jnp.exp(-x))


def _merge_kernel(ya_ref, yb_ref, yc_ref, wa_ref, wb_ref, wc_ref, ga_ref, gb_ref, gc_ref, o_ref):
    out = _sigmoid(ga_ref[...].astype(F32)) * _dot(ya_ref[...], wa_ref[...])
    out = out + _sigmoid(gb_ref[...].astype(F32)) * _dot(yb_ref[...], wb_ref[...])
    out = out + _sigmoid(gc_ref[...].astype(F32)) * _dot(yc_ref[...], wc_ref[...])
    o_ref[...] = out.astype(o_ref.dtype)


def _merge(ya, yb, yc, wa, wb, wc, gates, *, d_model, gate_col0, tm, tn):
    m, kdim = ya.shape
    y_spec = pl.BlockSpec((tm, kdim), lambda i, j: (i, 0))
    w_spec = pl.BlockSpec((kdim, tn), lambda i, j: (0, j))

    def g_spec(branch):
        base = (gate_col0 + branch * d_model) // tn
        return pl.BlockSpec((tm, tn), lambda i, j: (i, base + j))

    return pl.pallas_call(
        _merge_kernel,
        out_shape=jax.ShapeDtypeStruct((m, d_model), BF16),
        grid=(m // tm, d_model // tn),
        in_specs=[y_spec, y_spec, y_spec, w_spec, w_spec, w_spec, g_spec(0), g_spec(1), g_spec(2)],
        out_specs=pl.BlockSpec((tm, tn), lambda i, j: (i, j)),
        compiler_params=_cparams(("parallel", "parallel")),
        name="branch_merge",
    )(ya, yb, yc, wa, wb, wc, gates, gates, gates)


def _out_proj_kernel(y_ref, w_ref, g_ref, x_ref, o_ref):
    o_ref[...] = x_ref[...] + _rms(_dot(y_ref[...], w_ref[...]), g_ref[...])


def _out_proj(y, w, g, x2d, *, tm):
    m, d = x2d.shape
    return pl.pallas_call(
        _out_proj_kernel,
        out_shape=jax.ShapeDtypeStruct((m, d), F32),
        grid=(m // tm,),
        in_specs=[
            pl.BlockSpec((tm, d), lambda i: (i, 0)),
            pl.BlockSpec((d, d), lambda i: (0, 0)),
            pl.BlockSpec((1, d), lambda i: (0, 0)),
            pl.BlockSpec((tm, d), lambda i: (i, 0)),
        ],
        out_specs=pl.BlockSpec((tm, d), lambda i: (i, 0)),
        compiler_params=_cparams(("parallel",)),
        name="out_proj",
    )(y, w, g, x2d)


def _gelu_tanh(x):
    return 0.5 * x * (1.0 + jnp.tanh(math.sqrt(2.0 / math.pi) * (x + 0.044715 * (x * x * x))))


def _ffn_kernel(x_ref, halo_ref, gpre_ref, wg_ref, wv_ref, cwg_ref, cwv_ref, cbg_ref, cbv_ref,
                wd_ref, gpost_ref, o_ref, h_ref, acc_ref, *, seq):
    i, j = pl.program_id(0), pl.program_id(1)
    tm = x_ref.shape[0]

    @pl.when(j == 0)
    def _():
        h_ref[HALO:, :] = _rms(x_ref[...], gpre_ref[...]).astype(BF16)
        keep = jnp.where((i * tm) % seq == 0, 0.0, 1.0)
        h_ref[:HALO, :] = (_rms(halo_ref[...], gpre_ref[...]) * keep).astype(BF16)
        acc_ref[...] = jnp.zeros(acc_ref.shape, F32)

    h = h_ref[...]

    def conv(u, w_ref, b_ref):
        w = w_ref[...]
        y = b_ref[...] + w[2:3] * u[HALO:]
        y = y + w[1:2] * pltpu.roll(u, 1, 0)[HALO:]
        return y + w[0:1] * pltpu.roll(u, 2, 0)[HALO:]

    yg = conv(_dot(h, wg_ref[...]), cwg_ref, cbg_ref)
    yv = conv(_dot(h, wv_ref[...]), cwv_ref, cbv_ref)
    act = (_gelu_tanh(yg) * yv).astype(BF16)
    acc_ref[...] += _dot(act, wd_ref[...])

    @pl.when(j == pl.num_programs(1) - 1)
    def _():
        o_ref[...] = x_ref[...] + _rms(acc_ref[...], gpost_ref[...])


def _ffn(x2d, gpre, w_up, conv_w, conv_b, w_down, gpost, *, seq, tm, tn):
    m, d = x2d.shape
    d_ff = w_down.shape[0]
    nj = d_ff // tn
    halo_blocks = tm // HALO
    return pl.pallas_call(
        functools.partial(_ffn_kernel, seq=seq),
        out_shape=jax.ShapeDtypeStruct((m, d), F32),
        grid=(m // tm, nj),
        in_specs=[
            pl.BlockSpec((tm, d), lambda i, j: (i, 0)),
            pl.BlockSpec((HALO, d), lambda i, j: (jnp.maximum(i * halo_blocks - 1, 0), 0)),
            pl.BlockSpec((1, d), lambda i, j: (0, 0)),
            pl.BlockSpec((d, tn), lambda i, j: (0, j)),
            pl.BlockSpec((d, tn), lambda i, j: (0, nj + j)),
            pl.BlockSpec((CONV_WIDTH, tn), lambda i, j: (0, j)),
            pl.BlockSpec((CONV_WIDTH, tn), lambda i, j: (0, nj + j)),
            pl.BlockSpec((1, tn), lambda i, j: (0, j)),
            pl.BlockSpec((1, tn), lambda i, j: (0, nj + j)),
            pl.BlockSpec((tn, d), lambda i, j: (j, 0)),
            pl.BlockSpec((1, d), lambda i, j: (0, 0)),
        ],
        out_specs=pl.BlockSpec((tm, d), lambda i, j: (i, 0)),
        scratch_shapes=[pltpu.VMEM((tm + HALO, d), BF16), pltpu.VMEM((tm, d), F32)],
        compiler_params=_cparams(("parallel", "arbitrary")),
        name="conv_ffn",
    )(x2d, x2d, gpre, w_up, w_up, conv_w, conv_w, conv_b, conv_b, w_down, gpost)


def _pick(n, prefs):
    for t in prefs:
        if n % t == 0:
            return t
    raise ValueError(f"no tile in {prefs} divides {n}")


class _Tiles(NamedTuple):
    proj_rows: int
    proj_cols: int
    gate_cols: int
    merge_cols: int
    out_rows: int
    ffn_rows: int
    ffn_cols: int
    mem_q_rows: int


def _tiles(m):
    return _Tiles(proj_rows=_pick(m, (1024, 512, 256, 128)), proj_cols=2 * MXU_WIDTH, gate_cols=3 * MXU_WIDTH,
                  merge_cols=4 * MXU_WIDTH,
                  out_rows=_pick(m, (512, 256, 128)), ffn_rows=_pick(m, (512, 256, 128)), ffn_cols=2 * MXU_WIDTH,
                  mem_q_rows=512)


def kernel(x, mem, w_in, cmp_pe_k, cmp_w1_k, cmp_w2_k, cmp_pe_v, cmp_w1_v, cmp_w2_v, rel_bias, w_mem_kv,
           w_br_nsa, w_br_ret, w_br_mem, w_o, w_up, conv_w, conv_b, w_down, g_pre_mix, g_post_mix, g_mem,
           g_pre_ffn, g_post_ffn):
    batch, seq, d_model = x.shape
    mem_len = mem.shape[1]
    depth = w_in.shape[0]
    m = batch * seq
    grps, hpg, dh = NSA_GROUPS, NSA_HPG, HEAD_DIM

    nsa_q = NSA_HEADS * dh
    nsa_kv = 3 * 2 * grps * dh
    nsa_gates = 3 * NSA_HEADS
    ret_qk = RET_HEADS * RET_QK_DIM
    ret_v = RET_HEADS * RET_V_DIM
    mem_q = MEM_HEADS * MEM_HEAD_DIM
    splits = (nsa_q, nsa_kv, nsa_gates, ret_qk, ret_qk, ret_v, ret_v, mem_q, 3 * d_model)
    offs = np.concatenate([[0], np.cumsum(splits)])
    o_nq, o_nkv, o_ng, o_rq, o_rk, o_rv, o_rg, o_mq, o_bg = (int(v) for v in offs[:-1])

    q_chunk0 = 0
    rv_chunk0 = q_chunk0 + nsa_q // LANES
    mq_chunk0 = rv_chunk0 + ret_v // LANES
    rq_chunk0 = mq_chunk0 + mem_q // LANES
    rk_chunk0 = rq_chunk0 + ret_qk // LANES
    kv_chunk0 = rk_chunk0 + ret_qk // LANES
    bg_col0 = ret_v
    ng_col0 = bg_col0 + 3 * d_model
    tiles = _tiles(m)
    gate_cols = -(-(ng_col0 + grps * LANES) // tiles.gate_cols) * tiles.gate_cols

    tbl_flat = rel_bias.reshape(-1)
    cmp_bias, band = _bias_tables(tbl_flat, seq)
    x2d = x.reshape(m, d_model)

    for l in range(depth):
        w = w_in[l]
        cols = lambda o, n: w[:, o:o + n]
        w_attn = jnp.concatenate([cols(o_nq, nsa_q), cols(o_rv, ret_v), cols(o_mq, mem_q), cols(o_rq, ret_qk),
                                  cols(o_rk, ret_qk), cols(o_nkv, nsa_kv)], axis=1).astype(BF16)
        ng = cols(o_ng, nsa_gates).reshape(d_model, 3, grps, hpg).transpose(0, 2, 1, 3).reshape(d_model, grps, 3 * hpg)
        ng = jnp.pad(ng, ((0, 0), (0, 0), (0, LANES - 3 * hpg))).reshape(d_model, grps * LANES)
        w_gate = jnp.concatenate([cols(o_rg, ret_v), cols(o_bg, 3 * d_model), ng], axis=1)
        w_gate = jnp.pad(w_gate, ((0, 0), (0, gate_cols - w_gate.shape[1]))).astype(BF16)

        g_pre = g_pre_mix[l].reshape(1, d_model)
        proj, h_mix = _norm_proj(x2d, g_pre, w_attn, tm=tiles.proj_rows, tn=tiles.proj_cols, name="in_proj_heads")
        gates = _proj(h_mix, w_gate, tm=tiles.proj_rows, tn=tiles.gate_cols, name="in_proj_gates")

        rows = seq // CMP_STRIDE
        cmp_rows = proj[kv_chunk0:kv_chunk0 + 2 * grps].reshape(2 * grps, batch, rows, CMP_STRIDE * dh)
        pe = jnp.stack([cmp_pe_k[l], cmp_pe_v[l]]).reshape(2, 2, CMP_STRIDE * dh)
        w1 = jnp.stack([cmp_w1_k[l], cmp_w1_v[l]]).astype(BF16)
        w2 = jnp.stack([cmp_w2_k[l], cmp_w2_v[l]]).astype(BF16)
        cmp_kv, cmp_kv_t = _compress(cmp_rows, pe, w1, w2, batch)
        y_a = _nsa_attention(proj, cmp_kv, cmp_kv_t, gates, cmp_bias, band, batch=batch, seq=seq,
                             kv_chunk0=kv_chunk0, gate_blk0=ng_col0 // LANES)

        y_b = _retention(proj, gates, batch=batch, seq=seq, q_blk=rq_chunk0 // RET_HEADS,
                         k_blk=rk_chunk0 // RET_HEADS, v_blk=rv_chunk0 // (2 * RET_HEADS))

        mem2d = mem.reshape(batch * mem_len, d_model)
        mkv, _ = _norm_proj(mem2d, g_mem[l].reshape(1, d_model), w_mem_kv[l].astype(BF16),
                            tm=_tiles(batch * mem_len).proj_rows, tn=tiles.proj_cols, name="mem_kv_proj")
        y_c = _mem_attention(proj, mkv, batch=batch, seq=seq, mem_len=mem_len,
                             q_blk=mq_chunk0 // (mem_q // LANES), tq=tiles.mem_q_rows)

        merged = _merge(y_a, y_b, y_c, w_br_nsa[l].astype(BF16), w_br_ret[l].astype(BF16),
                        w_br_mem[l].astype(BF16), gates, d_model=d_model, gate_col0=bg_col0,
                        tm=tiles.proj_rows, tn=tiles.merge_cols)
        x2d = _out_proj(merged, w_o[l].astype(BF16), g_post_mix[l].reshape(1, d_model), x2d, tm=tiles.out_rows)

        x2d = _ffn(x2d, g_pre_ffn[l].reshape(1, d_model), w_up[l].astype(BF16), conv_w[l],
                   conv_b[l].reshape(1, -1), w_down[l].astype(BF16), g_post_ffn[l].reshape(1, d_model),
                   seq=seq, tm=tiles.ffn_rows, tn=tiles.ffn_cols)
    return x2d.reshape(batch, seq, d_model)
```

```python
import functools
import math
from typing import NamedTuple

import jax
import jax.numpy as jnp
import numpy as np
from jax import lax
from jax.experimental import pallas as pl
from jax.experimental.pallas import tpu as pltpu

F32 = jnp.float32
BF16 = jnp.bfloat16

LANES = 128
MXU_WIDTH = 256
EPS = 1e-6
NEG = -1e30

NSA_HEADS = 8
NSA_GROUPS = 2
NSA_HPG = NSA_HEADS // NSA_GROUPS
HEAD_DIM = 128
CMP_BLOCK = 32
CMP_STRIDE = 16
CMP_HIDDEN = 256
SEL_BLOCK = 64
SEL_TOPK = 16
SEL_LOCAL = 2
FORCE_SCORE = 1e4
WINDOW = 512
RET_HEADS = 4
RET_QK_DIM = 128
RET_V_DIM = 256
RET_CHUNK = 128
MEM_HEADS = 4
MEM_HEAD_DIM = 256
REL_BUCKETS = 32
REL_MAX_DIST = 128
CONV_WIDTH = 3

QT = 128
KCH = 2 * QT
PAD_BLOCKS = 8
HALO = 16

VMEM_LIMIT = 56 * 1024 * 1024


def _cparams(sem, vmem=VMEM_LIMIT):
    return pltpu.CompilerParams(dimension_semantics=sem, vmem_limit_bytes=vmem)


def _dot(a, b):
    return jnp.dot(a, b, preferred_element_type=F32)


def _dot_nt(a, b):
    return lax.dot_general(a, b, (((1,), (1,)), ((), ())), preferred_element_type=F32)


def _rms(x, g):
    ms = jnp.mean(x * x, axis=-1, keepdims=True)
    return x * lax.rsqrt(ms + EPS) * g


def _norm_proj_kernel(x_ref, g_ref, w_ref, o_ref, h_ref):
    @pl.when(pl.program_id(1) == 0)
    def _():
        h_ref[...] = _rms(x_ref[...], g_ref[...]).astype(BF16)

    acc = _dot(h_ref[...], w_ref[...])
    for c in range(acc.shape[1] // LANES):
        o_ref[c] = acc[:, c * LANES:(c + 1) * LANES].astype(o_ref.dtype)


def _norm_proj(x2d, g, w, *, tm, tn, name):
    m, d = x2d.shape
    n = w.shape[1]
    return pl.pallas_call(
        _norm_proj_kernel,
        out_shape=(jax.ShapeDtypeStruct((n // LANES, m, LANES), BF16), jax.ShapeDtypeStruct((m, d), BF16)),
        grid=(m // tm, n // tn),
        in_specs=[
            pl.BlockSpec((tm, d), lambda i, j: (i, 0)),
            pl.BlockSpec((1, d), lambda i, j: (0, 0)),
            pl.BlockSpec((d, tn), lambda i, j: (0, j)),
        ],
        out_specs=(pl.BlockSpec((tn // LANES, tm, LANES), lambda i, j: (j, i, 0)),
                   pl.BlockSpec((tm, d), lambda i, j: (i, 0))),
        compiler_params=_cparams(("parallel", "arbitrary")),
        name=name,
    )(x2d, g, w)


def _proj_kernel(h_ref, w_ref, o_ref):
    o_ref[...] = _dot(h_ref[...], w_ref[...]).astype(o_ref.dtype)


def _proj(h, w, *, tm, tn, name):
    m, d = h.shape
    n = w.shape[1]
    return pl.pallas_call(
        _proj_kernel,
        out_shape=jax.ShapeDtypeStruct((m, n), BF16),
        grid=(m // tm, n // tn),
        in_specs=[pl.BlockSpec((tm, d), lambda i, j: (i, 0)), pl.BlockSpec((d, tn), lambda i, j: (0, j))],
        out_specs=pl.BlockSpec((tm, tn), lambda i, j: (i, j)),
        compiler_params=_cparams(("parallel", "parallel")),
        name=name,
    )(h, w)


def _compress_kernel(x_ref, pe_ref, w1_ref, w2_ref, o_ref, ot_ref, xf_ref):
    half = CMP_STRIDE * HEAD_DIM
    xf_ref[...] = x_ref[0].astype(F32)
    rows = xf_ref.shape[0] // CMP_STRIDE
    x = jnp.concatenate([xf_ref[pl.ds(r, rows, stride=CMP_STRIDE), :] for r in range(CMP_STRIDE)], axis=1)
    pe = pe_ref[0]
    a = _dot((x + pe[0:1]).astype(BF16), w1_ref[0, :half, :])
    b = _dot((x + pe[1:2]).astype(BF16), w1_ref[0, half:, :])
    hid = a + pltpu.roll(b, rows - 1, 0)
    hid = hid * (1.0 / (1.0 + jnp.exp(-hid)))
    out = _dot(hid.astype(BF16), w2_ref[0])
    o_ref[0, 0] = out.astype(o_ref.dtype)
    ot_ref[0, 0] = out.T.astype(ot_ref.dtype)


def _compress(proj, kv_chunk0, pe, w1, w2, *, batch, seq):
    rows = seq // CMP_STRIDE
    width = CMP_STRIDE * HEAD_DIM
    n_out = 2 * NSA_GROUPS
    return pl.pallas_call(
        _compress_kernel,
        out_shape=(jax.ShapeDtypeStruct((batch, n_out, rows, HEAD_DIM), BF16),
                   jax.ShapeDtypeStruct((batch, n_out, HEAD_DIM, rows), BF16)),
        grid=(batch, n_out),
        in_specs=[
            pl.BlockSpec((1, seq, HEAD_DIM), lambda b, c: (kv_chunk0 + c, b, 0)),
            pl.BlockSpec((1, 2, width), lambda b, c: (c // NSA_GROUPS, 0, 0)),
            pl.BlockSpec((1, 2 * width, CMP_HIDDEN), lambda b, c: (c // NSA_GROUPS, 0, 0)),
            pl.BlockSpec((1, CMP_HIDDEN, HEAD_DIM), lambda b, c: (c // NSA_GROUPS, 0, 0)),
        ],
        out_specs=(pl.BlockSpec((1, 1, rows, HEAD_DIM), lambda b, c: (b, c, 0, 0)),
                   pl.BlockSpec((1, 1, HEAD_DIM, rows), lambda b, c: (b, c, 0, 0))),
        scratch_shapes=[pltpu.VMEM((seq, HEAD_DIM), F32)],
        compiler_params=_cparams(("parallel", "parallel")),
        name="nsa_compress",
    )(proj, pe, w1, w2)


def _bias_from_rel(rel, tbl_ref, head):
    max_exact = REL_BUCKETS // 2
    n = jnp.maximum(rel, 0)
    nf = jnp.maximum(n, 1).astype(F32)
    large = max_exact + (jnp.log(nf / max_exact) / math.log(REL_MAX_DIST / max_exact)
                         * (REL_BUCKETS - max_exact)).astype(jnp.int32)
    large = jnp.minimum(large, REL_BUCKETS - 1)
    bucket = jnp.where(n < max_exact, n, large)
    out = jnp.zeros(rel.shape, F32)
    for b in range(REL_BUCKETS):
        out = jnp.where(bucket == b, tbl_ref[b * NSA_HEADS + head], out)
    return out


def _cmp_bias_kernel(tbl_ref, o_ref):
    head, blk = pl.program_id(0), pl.program_id(1)
    rows, cols = o_ref.shape[1], o_ref.shape[2]
    n = lax.broadcasted_iota(jnp.int32, (rows, cols), 0)
    t = blk * cols + lax.broadcasted_iota(jnp.int32, (rows, cols), 1)
    o_ref[0] = _bias_from_rel(t - (n * CMP_STRIDE + CMP_BLOCK - 1), tbl_ref, head)


def _band_bias_kernel(tbl_ref, o_ref):
    head = pl.program_id(0)
    rows, cols = o_ref.shape[1], o_ref.shape[2]
    c = lax.broadcasted_iota(jnp.int32, (rows, cols), 0)
    i = lax.broadcasted_iota(jnp.int32, (rows, cols), 1)
    rel = i - c + QT
    far = tbl_ref[(REL_BUCKETS - 1) * NSA_HEADS + head]
    o_ref[0] = jnp.where(rel >= 0, (_bias_from_rel(rel, tbl_ref, head) - far) * (HEAD_DIM ** 0.5), NEG)


def _bias_tables(tbl_flat, seq):
    cols = 256
    smem = pl.BlockSpec(memory_space=pltpu.SMEM)
    cmp_bias = pl.pallas_call(
        _cmp_bias_kernel,
        out_shape=jax.ShapeDtypeStruct((NSA_HEADS, LANES, seq), F32),
        grid=(NSA_HEADS, seq // cols),
        in_specs=[smem],
        out_specs=pl.BlockSpec((1, LANES, cols), lambda h, r: (h, 0, r)),
        compiler_params=_cparams(("parallel", "parallel")),
        name="cmp_bias",
    )(tbl_flat)
    band = pl.pallas_call(
        _band_bias_kernel,
        out_shape=jax.ShapeDtypeStruct((NSA_HEADS, 2 * QT, QT), F32),
        grid=(NSA_HEADS,),
        in_specs=[smem],
        out_specs=pl.BlockSpec((1, 2 * QT, QT), lambda h: (h, 0, 0)),
        compiler_params=_cparams(("parallel",)),
        name="band_bias",
    )(tbl_flat)
    return cmp_bias, band


def _split3(x):
    hi = x.astype(BF16)
    r1 = x - hi.astype(F32)
    mid = r1.astype(BF16)
    lo = (r1 - mid.astype(F32)).astype(BF16)
    return hi, mid, lo


def _nsa_kernel(q_ref, kc_ref, vct_ref, ks_ref, vs_ref, kw_ref, vw_ref, gate_ref, cbias_ref, band_ref,
                o_ref, kpad_s, vt_s, kpad_w, vt_w, acc_s, acc_w, selb_ref, posb_ref):
    qi = pl.program_id(1)
    hpg, grps = NSA_HPG, NSA_GROUPS
    scale = HEAD_DIM ** -0.5
    n_tiles = vt_s.shape[1] - 1
    seq = n_tiles * QT
    n_sel = seq // SEL_BLOCK
    n_cmp = (seq - CMP_BLOCK) // CMP_STRIDE + 1
    blocks_per_tile = QT // SEL_BLOCK

    @pl.when(qi == 0)
    def _():
        for kpad, vt, k_ref, v_ref in ((kpad_s, vt_s, ks_ref, vs_ref), (kpad_w, vt_w, kw_ref, vw_ref)):
            for g in range(grps):
                kpad[g, :QT, :] = jnp.zeros((QT, HEAD_DIM), BF16)
                kpad[g, QT:, :] = k_ref[g]
                vt[g, 0] = jnp.zeros((HEAD_DIM, QT), BF16)
                for t in range(n_tiles):
                    vt[g, t + 1] = v_ref[g, t * QT:(t + 1) * QT, :].astype(F32).T.astype(BF16)
        neg_rows = jnp.full((PAD_BLOCKS, QT), NEG, F32)
        for g in range(grps):
            selb_ref[g, :PAD_BLOCKS, :] = neg_rows
        posb_ref[:PAD_BLOCKS, :] = neg_rows
        posb_ref[PAD_BLOCKS:, :] = jnp.zeros((n_sel, QT), F32)

    qs = [q_ref[g * hpg:(g + 1) * hpg].reshape(hpg * QT, HEAD_DIM) for g in range(grps)]
    sub = lax.broadcasted_iota(jnp.int32, (QT, QT), 0)
    lane = lax.broadcasted_iota(jnp.int32, (QT, QT), 1)
    valid_c = (qi * QT + lane) - (sub * CMP_STRIDE + CMP_BLOCK - 1) >= 0
    jb = lax.broadcasted_iota(jnp.int32, (n_sel, QT), 0)
    nn = lax.broadcasted_iota(jnp.int32, (n_sel, QT), 1)
    overlap_t = jnp.where((nn * CMP_STRIDE < jb * SEL_BLOCK + SEL_BLOCK)
                          & (nn * CMP_STRIDE + CMP_BLOCK > jb * SEL_BLOCK)
                          & (nn < n_cmp), 1.0, 0.0).astype(BF16)
    cur = (qi * QT + nn) // SEL_BLOCK
    forced = (jb == 0) | ((cur - jb >= 0) & (cur - jb < SEL_LOCAL))

    def scores(kpad, g, tile0, n_keys):
        start = pl.multiple_of((jnp.maximum(tile0, -1) + 1) * QT, QT)
        return _dot_nt(kpad[g, pl.ds(start, n_keys), :], qs[g])

    near = qi - 1
    mid_tile = qi - 3
    far_tile = qi - WINDOW // QT
    raw_w = [(scores(kpad_w, g, near, KCH), scores(kpad_w, g, mid_tile, KCH), scores(kpad_w, g, far_tile, QT))
             for g in range(grps)]
    raw_s = [scores(kpad_s, g, near, KCH) for g in range(grps)]

    o_cmp = []
    for g in range(grps):
        sc = _dot_nt(kc_ref[0, g], qs[g]) * scale
        p_heads = []
        p_sum = None
        for h in range(hpg):
            s = jnp.where(valid_c, sc[:, h * QT:(h + 1) * QT] + cbias_ref[g * hpg + h], NEG)
            mx = jnp.max(s, axis=0, keepdims=True)
            p = jnp.where(valid_c, jnp.exp(s - mx), 0.0)
            p = p / jnp.maximum(jnp.sum(p, axis=0, keepdims=True), 1e-30)
            p_heads.append(p.astype(BF16))
            p_sum = p if p_sum is None else p_sum + p
        o_cmp.append(_dot(vct_ref[0, g], jnp.concatenate(p_heads, axis=1)))

        hi, mid, lo = _split3(p_sum)
        imp = _dot(overlap_t, hi) + _dot(overlap_t, mid) + _dot(overlap_t, lo)
        imp = jnp.where(forced, FORCE_SCORE, imp)
        imp = jnp.where(jb <= cur, imp, -1.0)
        rank = jnp.zeros((n_sel, QT), F32)
        for i in range(n_sel):
            other = jnp.broadcast_to(imp[i:i + 1, :], (n_sel, QT))
            rank = rank + jnp.where(jb > i, jnp.where(other >= imp, 1.0, 0.0), jnp.where(other > imp, 1.0, 0.0))
        selb_ref[g, PAD_BLOCKS:, :] = jnp.where(rank < min(SEL_TOPK, n_sel), 0.0, NEG)

    def row_bias(ref, tile0, n_keys):
        r0 = PAD_BLOCKS + tile0 * blocks_per_tile
        return jnp.concatenate([jnp.broadcast_to(ref[pl.ds(r0 + b, 1), :], (SEL_BLOCK, QT))
                                for b in range(n_keys // SEL_BLOCK)], axis=0)

    def add_shared(s, bias):
        return jnp.concatenate([s[:, h * QT:(h + 1) * QT] + bias for h in range(hpg)], axis=1)

    def add_band(s, bias, g):
        return jnp.concatenate([s[:, h * QT:(h + 1) * QT] + (band_ref[g * hpg + h] + bias) for h in range(hpg)],
                               axis=1)

    def weighted_values(vt, g, tile0, p):
        t0 = jnp.maximum(tile0, -1) + 1
        pb = p.astype(BF16)
        pv = _dot(vt[g, t0], pb[:QT])
        for t in range(1, p.shape[0] // QT):
            pv = pv + _dot(vt[g, t0 + t], pb[t * QT:(t + 1) * QT])
        return pv

    to_exp2 = scale * math.log2(math.e)

    def flash_first(s, vt, g, tile0, acc_ref):
        m = jnp.max(s, axis=0, keepdims=True)
        p = jnp.exp2((s - m) * to_exp2)
        acc_ref[g] = weighted_values(vt, g, tile0, p)
        return m, jnp.sum(p, axis=0, keepdims=True)

    def flash_next(s, vt, g, tile0, acc_ref, m, l):
        m_new = jnp.maximum(m, jnp.max(s, axis=0, keepdims=True))
        alpha = jnp.exp2((m - m_new) * to_exp2)
        p = jnp.exp2((s - m_new) * to_exp2)
        acc_ref[g] = acc_ref[g] * alpha + weighted_values(vt, g, tile0, p)
        return m_new, alpha * l + jnp.sum(p, axis=0, keepdims=True)


    edge = jnp.where(sub > lane, 0.0, NEG)
    stats_w = [None] * grps
    for g in range(grps):
        s = add_band(raw_w[g][0], row_bias(posb_ref, near, KCH), g)
        stats_w[g] = flash_first(s, vt_w, g, near, acc_w)
    for g in range(grps):
        s = add_shared(raw_w[g][1], row_bias(posb_ref, mid_tile, KCH))
        stats_w[g] = flash_next(s, vt_w, g, mid_tile, acc_w, *stats_w[g])
    for g in range(grps):
        s = add_shared(raw_w[g][2], row_bias(posb_ref, far_tile, QT) + edge)
        stats_w[g] = flash_next(s, vt_w, g, far_tile, acc_w, *stats_w[g])

    stats_s = []
    for g in range(grps):
        s = add_band(raw_s[g], row_bias(selb_ref.at[g], near, KCH), g)
        stats_s.extend(flash_first(s, vt_s, g, near, acc_s))

    def sel_body(e, carry):
        tile0 = qi + 1 - 2 * (e + 1)
        raw = [scores(kpad_s, g, tile0, KCH) for g in range(grps)]
        out = []
        for g in range(grps):
            s = add_shared(raw[g], row_bias(selb_ref.at[g], tile0, KCH))
            out.extend(flash_next(s, vt_s, g, tile0, acc_s, carry[2 * g], carry[2 * g + 1]))
        return tuple(out)

    stats_s = lax.fori_loop(1, (qi + 2) // 2, sel_body, tuple(stats_s))

    for g in range(grps):
        gate = _sigmoid(gate_ref[:, g * LANES:(g + 1) * LANES].astype(F32)).T
        inv_s = 1.0 / jnp.maximum(stats_s[2 * g + 1], 1e-30)
        inv_w = 1.0 / jnp.maximum(stats_w[g][1], 1e-30)
        for h in range(hpg):
            sl = slice(h * QT, (h + 1) * QT)
            o_t = (gate[h:h + 1] * o_cmp[g][:, sl]
                   + gate[hpg + h:hpg + h + 1] * (acc_s[g, :, sl] * inv_s[:, sl])
                   + gate[2 * hpg + h:2 * hpg + h + 1] * (acc_w[g, :, sl] * inv_w[:, sl]))
            c0 = (g * hpg + h) * HEAD_DIM
            o_ref[:, c0:c0 + HEAD_DIM] = o_t.T.astype(o_ref.dtype)


def _nsa_attention(proj, cmp_kv, cmp_kv_t, gates, cmp_bias, band, *, batch, seq, kv_chunk0, gate_blk0):
    nq = seq // QT
    grps = NSA_GROUPS
    n_sel = seq // SEL_BLOCK
    rows_c = seq // CMP_STRIDE
    assert WINDOW // QT <= PAD_BLOCKS // (QT // SEL_BLOCK) and WINDOW == 4 * QT and KCH == 2 * QT
    assert kv_chunk0 % grps == 0 and gate_blk0 % grps == 0

    def kv_spec(branch, kv):
        blk = (kv_chunk0 + (branch * 2 + kv) * grps) // grps
        return pl.BlockSpec((grps, seq, HEAD_DIM), lambda b, i: (blk, b, 0))

    return pl.pallas_call(
        _nsa_kernel,
        out_shape=jax.ShapeDtypeStruct((batch * seq, NSA_HEADS * HEAD_DIM), BF16),
        grid=(batch, nq),
        in_specs=[
            pl.BlockSpec((NSA_HEADS, QT, HEAD_DIM), lambda b, i: (0, b * nq + i, 0)),
            pl.BlockSpec((1, grps, rows_c, HEAD_DIM), lambda b, i: (b, 0, 0, 0)),
            pl.BlockSpec((1, grps, HEAD_DIM, rows_c), lambda b, i: (b, 1, 0, 0)),
            kv_spec(1, 0), kv_spec(1, 1), kv_spec(2, 0), kv_spec(2, 1),
            pl.BlockSpec((QT, grps * LANES), lambda b, i: (b * nq + i, gate_blk0 // grps)),
            pl.BlockSpec((NSA_HEADS, LANES, QT), lambda b, i: (0, 0, i)),
            pl.BlockSpec((NSA_HEADS, KCH, QT), lambda b, i: (0, 0, 0)),
        ],
        out_specs=pl.BlockSpec((QT, NSA_HEADS * HEAD_DIM), lambda b, i: (b * nq + i, 0)),
        scratch_shapes=[
            pltpu.VMEM((grps, seq + QT, HEAD_DIM), BF16),
            pltpu.VMEM((grps, nq + 1, HEAD_DIM, QT), BF16),
            pltpu.VMEM((grps, seq + QT, HEAD_DIM), BF16),
            pltpu.VMEM((grps, nq + 1, HEAD_DIM, QT), BF16),
            pltpu.VMEM((grps, HEAD_DIM, NSA_HPG * QT), F32),
            pltpu.VMEM((grps, HEAD_DIM, NSA_HPG * QT), F32),
            pltpu.VMEM((grps, PAD_BLOCKS + n_sel, QT), F32),
            pltpu.VMEM((PAD_BLOCKS + n_sel, QT), F32),
        ],
        compiler_params=_cparams(("arbitrary", "arbitrary")),
        name="nsa_attention",
    )(proj, cmp_kv, cmp_kv_t, proj, proj, proj, proj, gates, cmp_bias, band)


def _retention_kernel(q_ref, k_ref, v_ref, g_ref, cos_ref, sin_ref, dmat_ref, xi_ref, zeta_ref,
                      o_ref, state_ref, *, g_chunk):
    @pl.when(pl.program_id(1) == 0)
    def _():
        state_ref[...] = jnp.zeros(state_ref.shape, F32)

    cos, sin = cos_ref[...], sin_ref[...]
    half = RET_QK_DIM // 2
    heads = range(RET_HEADS)
    qb, kr, v, inner, cross = [], [], [], [], []
    for h in heads:
        q = q_ref[h].astype(F32)
        k = k_ref[h].astype(F32)
        qb.append((q * cos + pltpu.roll(q, half, 1) * sin).astype(BF16))
        kr.append((k * cos + pltpu.roll(k, half, 1) * sin) * (RET_QK_DIM ** -0.5))
        v.append(jnp.concatenate([v_ref[2 * h], v_ref[2 * h + 1]], axis=1))
    for h in heads:
        inner.append(_dot_nt(qb[h], kr[h].astype(BF16)))
        cross.append(_dot(qb[h], state_ref[h].astype(BF16)))
    for h in heads:
        kz = (kr[h] * zeta_ref[h]).T.astype(BF16)
        state_ref[h] = state_ref[h] * g_chunk[h] + _dot(kz, v[h])
    for h in heads:
        o = _dot((inner[h] * dmat_ref[h]).astype(BF16), v[h]) + cross[h] * xi_ref[h][:, 0:1]
        mu = jnp.mean(o, axis=-1, keepdims=True)
        var = jnp.mean(jnp.square(o - mu), axis=-1, keepdims=True)
        on = (o - mu) * lax.rsqrt(var + EPS)
        gate = g_ref[:, h * RET_V_DIM:(h + 1) * RET_V_DIM].astype(F32)
        gate = gate * (1.0 / (1.0 + jnp.exp(-gate)))
        o_ref[:, h * RET_V_DIM:(h + 1) * RET_V_DIM] = (on * gate).astype(o_ref.dtype)


def _retention(proj, gates, *, batch, seq, q_blk, k_blk, v_blk):
    c = RET_CHUNK
    n = seq // c
    heads = RET_HEADS
    dk = RET_QK_DIM
    theta = 1.0 / (10000.0 ** np.linspace(0.0, 1.0, dk // 2))
    ang = np.arange(seq)[:, None] * theta[None, :]
    cos = np.concatenate([np.cos(ang), np.cos(ang)], axis=1)
    sin = np.concatenate([-np.sin(ang), np.sin(ang)], axis=1)
    log_g = np.log(1.0 - np.exp2(-5.0 - np.arange(heads)))
    j = np.arange(c)
    diff = j[:, None] - j[None, :]
    dmat = np.where(diff >= 0, np.exp(np.maximum(diff, 0)[None] * log_g[:, None, None]), 0.0)
    xi = np.broadcast_to(np.exp((j + 1.0)[None, :] * log_g[:, None])[:, :, None], (heads, c, LANES))
    zeta = np.broadcast_to(np.exp((c - 1.0 - j)[None, :] * log_g[:, None])[:, :, None], (heads, c, LANES))
    g_chunk = tuple(float(v) for v in np.exp(c * log_g))
    consts = [jnp.asarray(a, F32) for a in (cos, sin, dmat, xi, zeta)]

    full3 = lambda shape: pl.BlockSpec(shape, lambda b, i: (0, 0, 0))
    return pl.pallas_call(
        functools.partial(_retention_kernel, g_chunk=g_chunk),
        out_shape=jax.ShapeDtypeStruct((batch * seq, heads * RET_V_DIM), BF16),
        grid=(batch, n),
        in_specs=[
            pl.BlockSpec((heads, c, LANES), lambda b, i: (q_blk, b * n + i, 0)),
            pl.BlockSpec((heads, c, LANES), lambda b, i: (k_blk, b * n + i, 0)),
            pl.BlockSpec((2 * heads, c, LANES), lambda b, i: (v_blk, b * n + i, 0)),
            pl.BlockSpec((c, heads * RET_V_DIM), lambda b, i: (b * n + i, 0)),
            pl.BlockSpec((c, dk), lambda b, i: (i, 0)),
            pl.BlockSpec((c, dk), lambda b, i: (i, 0)),
            full3((heads, c, c)), full3((heads, c, LANES)), full3((heads, c, LANES)),
        ],
        out_specs=pl.BlockSpec((c, heads * RET_V_DIM), lambda b, i: (b * n + i, 0)),
        scratch_shapes=[pltpu.VMEM((heads, dk, RET_V_DIM), F32)],
        compiler_params=_cparams(("parallel", "arbitrary")),
        name="retention",
    )(proj, proj, proj, gates, *consts)


def _mem_attn_kernel(q_ref, k_ref, v_ref, o_ref):
    scale = MEM_HEAD_DIM ** -0.5
    scores = [(_dot_nt(q_ref[2 * h], k_ref[2 * h]) + _dot_nt(q_ref[2 * h + 1], k_ref[2 * h + 1])) * scale
              for h in range(MEM_HEADS)]
    for h in range(MEM_HEADS):
        s = scores[h]
        mx = jnp.max(s, axis=-1, keepdims=True)
        p = jnp.exp(s - mx)
        p = (p / jnp.sum(p, axis=-1, keepdims=True)).astype(BF16)
        for half in range(2):
            c0 = h * MEM_HEAD_DIM + half * LANES
            o_ref[:, c0:c0 + LANES] = _dot(p, v_ref[2 * h + half]).astype(o_ref.dtype)


def _mem_attention(proj, mkv, *, batch, seq, mem_len, q_blk, tq):
    nq = seq // tq
    chunks = MEM_HEADS * MEM_HEAD_DIM // LANES
    return pl.pallas_call(
        _mem_attn_kernel,
        out_shape=jax.ShapeDtypeStruct((batch * seq, MEM_HEADS * MEM_HEAD_DIM), BF16),
        grid=(batch, nq),
        in_specs=[
            pl.BlockSpec((chunks, tq, LANES), lambda b, i: (q_blk, b * nq + i, 0)),
            pl.BlockSpec((chunks, mem_len, LANES), lambda b, i: (0, b, 0)),
            pl.BlockSpec((chunks, mem_len, LANES), lambda b, i: (1, b, 0)),
        ],
        out_specs=pl.BlockSpec((tq, MEM_HEADS * MEM_HEAD_DIM), lambda b, i: (b * nq + i, 0)),
        compiler_params=_cparams(("parallel", "parallel")),
        name="mem_attention",
    )(proj, mkv, mkv)


def _sigmoid(x):
    return 1.0 / (1.0 + jnp.exp(-x))


def _merge_kernel(ya_ref, yb_ref, yc_ref, wa_ref, wb_ref, wc_ref, ga_ref, gb_ref, gc_ref, o_ref):
    out = _sigmoid(ga_ref[...].astype(F32)) * _dot(ya_ref[...], wa_ref[...])
    out = out + _sigmoid(gb_ref[...].astype(F32)) * _dot(yb_ref[...], wb_ref[...])
    out = out + _sigmoid(gc_ref[...].astype(F32)) * _dot(yc_ref[...], wc_ref[...])
    o_ref[...] = out.astype(o_ref.dtype)


def _merge(ya, yb, yc, wa, wb, wc, gates, *, d_model, gate_col0, tm, tn):
    m, kdim = ya.shape
    y_spec = pl.BlockSpec((tm, kdim), lambda i, j: (i, 0))
    w_spec = pl.BlockSpec((kdim, tn), lambda i, j: (0, j))

    def g_spec(branch):
        base = (gate_col0 + branch * d_model) // tn
        return pl.BlockSpec((tm, tn), lambda i, j: (i, base + j))

    return pl.pallas_call(
        _merge_kernel,
        out_shape=jax.ShapeDtypeStruct((m, d_model), BF16),
        grid=(m // tm, d_model // tn),
        in_specs=[y_spec, y_spec, y_spec, w_spec, w_spec, w_spec, g_spec(0), g_spec(1), g_spec(2)],
        out_specs=pl.BlockSpec((tm, tn), lambda i, j: (i, j)),
        compiler_params=_cparams(("parallel", "parallel")),
        name="branch_merge",
    )(ya, yb, yc, wa, wb, wc, gates, gates, gates)


def _out_proj_kernel(y_ref, w_ref, g_ref, x_ref, o_ref):
    o_ref[...] = x_ref[...] + _rms(_dot(y_ref[...], w_ref[...]), g_ref[...])


def _out_proj(y, w, g, x2d, *, tm):
    m, d = x2d.shape
    return pl.pallas_call(
        _out_proj_kernel,
        out_shape=jax.ShapeDtypeStruct((m, d), F32),
        grid=(m // tm,),
        in_specs=[
            pl.BlockSpec((tm, d), lambda i: (i, 0)),
            pl.BlockSpec((d, d), lambda i: (0, 0)),
            pl.BlockSpec((1, d), lambda i: (0, 0)),
            pl.BlockSpec((tm, d), lambda i: (i, 0)),
        ],
        out_specs=pl.BlockSpec((tm, d), lambda i: (i, 0)),
        compiler_params=_cparams(("parallel",)),
        name="out_proj",
    )(y, w, g, x2d)


def _gelu_tanh(x):
    return 0.5 * x * (1.0 + jnp.tanh(math.sqrt(2.0 / math.pi) * (x + 0.044715 * (x * x * x))))


def _ffn_kernel(x_ref, halo_ref, gpre_ref, wg_ref, wv_ref, cwg_ref, cwv_ref, cbg_ref, cbv_ref,
                wd_ref, gpost_ref, o_ref, h_ref, acc_ref, *, seq):
    i, j = pl.program_id(0), pl.program_id(1)
    tm = x_ref.shape[0]

    @pl.when(j == 0)
    def _():
        h_ref[HALO:, :] = _rms(x_ref[...], gpre_ref[...]).astype(BF16)
        keep = jnp.where((i * tm) % seq == 0, 0.0, 1.0)
        h_ref[:HALO, :] = (_rms(halo_ref[...], gpre_ref[...]) * keep).astype(BF16)
        acc_ref[...] = jnp.zeros(acc_ref.shape, F32)

    h = h_ref[...]

    def conv(u, w_ref, b_ref):
        w = w_ref[...]
        y = b_ref[...] + w[2:3] * u[HALO:]
        y = y + w[1:2] * pltpu.roll(u, 1, 0)[HALO:]
        return y + w[0:1] * pltpu.roll(u, 2, 0)[HALO:]

    yg = conv(_dot(h, wg_ref[...]), cwg_ref, cbg_ref)
    yv = conv(_dot(h, wv_ref[...]), cwv_ref, cbv_ref)
    act = (_gelu_tanh(yg) * yv).astype(BF16)
    acc_ref[...] += _dot(act, wd_ref[...])

    @pl.when(j == pl.num_programs(1) - 1)
    def _():
        o_ref[...] = x_ref[...] + _rms(acc_ref[...], gpost_ref[...])


def _ffn(x2d, gpre, w_up, conv_w, conv_b, w_down, gpost, *, seq, tm, tn):
    m, d = x2d.shape
    d_ff = w_down.shape[0]
    nj = d_ff // tn
    halo_blocks = tm // HALO
    return pl.pallas_call(
        functools.partial(_ffn_kernel, seq=seq),
        out_shape=jax.ShapeDtypeStruct((m, d), F32),
        grid=(m // tm, nj),
        in_specs=[
            pl.BlockSpec((tm, d), lambda i, j: (i, 0)),
            pl.BlockSpec((HALO, d), lambda i, j: (jnp.maximum(i * halo_blocks - 1, 0), 0)),
            pl.BlockSpec((1, d), lambda i, j: (0, 0)),
            pl.BlockSpec((d, tn), lambda i, j: (0, j)),
            pl.BlockSpec((d, tn), lambda i, j: (0, nj + j)),
            pl.BlockSpec((CONV_WIDTH, tn), lambda i, j: (0, j)),
            pl.BlockSpec((CONV_WIDTH, tn), lambda i, j: (0, nj + j)),
            pl.BlockSpec((1, tn), lambda i, j: (0, j)),
            pl.BlockSpec((1, tn), lambda i, j: (0, nj + j)),
            pl.BlockSpec((tn, d), lambda i, j: (j, 0)),
            pl.BlockSpec((1, d), lambda i, j: (0, 0)),
        ],
        out_specs=pl.BlockSpec((tm, d), lambda i, j: (i, 0)),
        scratch_shapes=[pltpu.VMEM((tm + HALO, d), BF16), pltpu.VMEM((tm, d), F32)],
        compiler_params=_cparams(("parallel", "arbitrary")),
        name="conv_ffn",
    )(x2d, x2d, gpre, w_up, w_up, conv_w, conv_w, conv_b, conv_b, w_down, gpost)


def _pick(n, prefs):
    for t in prefs:
        if n % t == 0:
            return t
    raise ValueError(f"no tile in {prefs} divides {n}")


class _Tiles(NamedTuple):
    proj_rows: int
    proj_cols: int
    gate_cols: int
    merge_cols: int
    out_rows: int
    ffn_rows: int
    ffn_cols: int
    mem_q_rows: int


def _tiles(m):
    return _Tiles(proj_rows=_pick(m, (1024, 512, 256, 128)), proj_cols=2 * MXU_WIDTH, gate_cols=3 * MXU_WIDTH,
                  merge_cols=4 * MXU_WIDTH,
                  out_rows=_pick(m, (512, 256, 128)), ffn_rows=_pick(m, (512, 256, 128)), ffn_cols=2 * MXU_WIDTH,
                  mem_q_rows=512)


def kernel(x, mem, w_in, cmp_pe_k, cmp_w1_k, cmp_w2_k, cmp_pe_v, cmp_w1_v, cmp_w2_v, rel_bias, w_mem_kv,
           w_br_nsa, w_br_ret, w_br_mem, w_o, w_up, conv_w, conv_b, w_down, g_pre_mix, g_post_mix, g_mem,
           g_pre_ffn, g_post_ffn):
    batch, seq, d_model = x.shape
    mem_len = mem.shape[1]
    depth = w_in.shape[0]
    m = batch * seq
    grps, hpg, dh = NSA_GROUPS, NSA_HPG, HEAD_DIM

    nsa_q = NSA_HEADS * dh
    nsa_kv = 3 * 2 * grps * dh
    nsa_gates = 3 * NSA_HEADS
    ret_qk = RET_HEADS * RET_QK_DIM
    ret_v = RET_HEADS * RET_V_DIM
    mem_q = MEM_HEADS * MEM_HEAD_DIM
    splits = (nsa_q, nsa_kv, nsa_gates, ret_qk, ret_qk, ret_v, ret_v, mem_q, 3 * d_model)
    offs = np.concatenate([[0], np.cumsum(splits)])
    o_nq, o_nkv, o_ng, o_rq, o_rk, o_rv, o_rg, o_mq, o_bg = (int(v) for v in offs[:-1])

    q_chunk0 = 0
    rv_chunk0 = q_chunk0 + nsa_q // LANES
    mq_chunk0 = rv_chunk0 + ret_v // LANES
    rq_chunk0 = mq_chunk0 + mem_q // LANES
    rk_chunk0 = rq_chunk0 + ret_qk // LANES
    kv_chunk0 = rk_chunk0 + ret_qk // LANES
    bg_col0 = ret_v
    ng_col0 = bg_col0 + 3 * d_model
    tiles = _tiles(m)
    gate_cols = -(-(ng_col0 + grps * LANES) // tiles.gate_cols) * tiles.gate_cols

    tbl_flat = rel_bias.reshape(-1)
    cmp_bias, band = _bias_tables(tbl_flat, seq)
    x2d = x.reshape(m, d_model)

    for l in range(depth):
        w = w_in[l]
        cols = lambda o, n: w[:, o:o + n]
        w_attn = jnp.concatenate([cols(o_nq, nsa_q), cols(o_rv, ret_v), cols(o_mq, mem_q), cols(o_rq, ret_qk),
                                  cols(o_rk, ret_qk), cols(o_nkv, nsa_kv)], axis=1).astype(BF16)
        ng = cols(o_ng, nsa_gates).reshape(d_model, 3, grps, hpg).transpose(0, 2, 1, 3).reshape(d_model, grps, 3 * hpg)
        ng = jnp.pad(ng, ((0, 0), (0, 0), (0, LANES - 3 * hpg))).reshape(d_model, grps * LANES)
        w_gate = jnp.concatenate([cols(o_rg, ret_v), cols(o_bg, 3 * d_model), ng], axis=1)
        w_gate = jnp.pad(w_gate, ((0, 0), (0, gate_cols - w_gate.shape[1]))).astype(BF16)

        g_pre = g_pre_mix[l].reshape(1, d_model)
        proj, h_mix = _norm_proj(x2d, g_pre, w_attn, tm=tiles.proj_rows, tn=tiles.proj_cols, name="in_proj_heads")
        gates = _proj(h_mix, w_gate, tm=tiles.proj_rows, tn=tiles.gate_cols, name="in_proj_gates")

        pe = jnp.stack([cmp_pe_k[l], cmp_pe_v[l]]).reshape(2, 2, CMP_STRIDE * dh)
        w1 = jnp.stack([cmp_w1_k[l], cmp_w1_v[l]]).astype(BF16)
        w2 = jnp.stack([cmp_w2_k[l], cmp_w2_v[l]]).astype(BF16)
        cmp_kv, cmp_kv_t = _compress(proj, kv_chunk0, pe, w1, w2, batch=batch, seq=seq)
        y_a = _nsa_attention(proj, cmp_kv, cmp_kv_t, gates, cmp_bias, band, batch=batch, seq=seq,
                             kv_chunk0=kv_chunk0, gate_blk0=ng_col0 // LANES)

        y_b = _retention(proj, gates, batch=batch, seq=seq, q_blk=rq_chunk0 // RET_HEADS,
                         k_blk=rk_chunk0 // RET_HEADS, v_blk=rv_chunk0 // (2 * RET_HEADS))

        mem2d = mem.reshape(batch * mem_len, d_model)
        mkv, _ = _norm_proj(mem2d, g_mem[l].reshape(1, d_model), w_mem_kv[l].astype(BF16),
                            tm=_tiles(batch * mem_len).proj_rows, tn=tiles.proj_cols, name="mem_kv_proj")
        y_c = _mem_attention(proj, mkv, batch=batch, seq=seq, mem_len=mem_len,
                             q_blk=mq_chunk0 // (mem_q // LANES), tq=tiles.mem_q_rows)

        merged = _merge(y_a, y_b, y_c, w_br_nsa[l].astype(BF16), w_br_ret[l].astype(BF16),
                        w_br_mem[l].astype(BF16), gates, d_model=d_model, gate_col0=bg_col0,
                        tm=tiles.proj_rows, tn=tiles.merge_cols)
        x2d = _out_proj(merged, w_o[l].astype(BF16), g_post_mix[l].reshape(1, d_model), x2d, tm=tiles.out_rows)

        x2d = _ffn(x2d, g_pre_ffn[l].reshape(1, d_model), w_up[l].astype(BF16), conv_w[l],
                   conv_b[l].reshape(1, -1), w_down[l].astype(BF16), g_post_ffn[l].reshape(1, d_model),
                   seq=seq, tm=tiles.ffn_rows, tn=tiles.ffn_cols)
    return x2d.reshape(batch, seq, d_model)
```

```python
import functools
import math
from typing import NamedTuple

import jax
import jax.numpy as jnp
import numpy as np
from jax import lax
from jax.experimental import pallas as pl
from jax.experimental.pallas import tpu as pltpu

F32 = jnp.float32
BF16 = jnp.bfloat16

LANES = 128
MXU_WIDTH = 256
EPS = 1e-6
NEG = -1e30

NSA_HEADS = 8
NSA_GROUPS = 2
NSA_HPG = NSA_HEADS // NSA_GROUPS
HEAD_DIM = 128
CMP_BLOCK = 32
CMP_STRIDE = 16
CMP_HIDDEN = 256
SEL_BLOCK = 64
SEL_TOPK = 16
SEL_LOCAL = 2
FORCE_SCORE = 1e4
WINDOW = 512
RET_HEADS = 4
RET_QK_DIM = 128
RET_V_DIM = 256
RET_CHUNK = 128
MEM_HEADS = 4
MEM_HEAD_DIM = 256
REL_BUCKETS = 32
REL_MAX_DIST = 128
CONV_WIDTH = 3

QT = 128
KCH = 2 * QT
PAD_BLOCKS = 8
HALO = 16

VMEM_LIMIT = 56 * 1024 * 1024


def _cparams(sem, vmem=VMEM_LIMIT):
    return pltpu.CompilerParams(dimension_semantics=sem, vmem_limit_bytes=vmem)


def _dot(a, b):
    return jnp.dot(a, b, preferred_element_type=F32)


def _dot_nt(a, b):
    return lax.dot_general(a, b, (((1,), (1,)), ((), ())), preferred_element_type=F32)


def _rms(x, g):
    ms = jnp.mean(x * x, axis=-1, keepdims=True)
    return x * lax.rsqrt(ms + EPS) * g


def _norm_proj_kernel(x_ref, g_ref, w_ref, o_ref, h_ref):
    @pl.when(pl.program_id(1) == 0)
    def _():
        h_ref[...] = _rms(x_ref[...], g_ref[...]).astype(BF16)

    acc = _dot(h_ref[...], w_ref[...])
    for c in range(acc.shape[1] // LANES):
        o_ref[c] = acc[:, c * LANES:(c + 1) * LANES].astype(o_ref.dtype)


def _norm_proj(x2d, g, w, *, tm, tn, name):
    m, d = x2d.shape
    n = w.shape[1]
    return pl.pallas_call(
        _norm_proj_kernel,
        out_shape=(jax.ShapeDtypeStruct((n // LANES, m, LANES), BF16), jax.ShapeDtypeStruct((m, d), BF16)),
        grid=(m // tm, n // tn),
        in_specs=[
            pl.BlockSpec((tm, d), lambda i, j: (i, 0)),
            pl.BlockSpec((1, d), lambda i, j: (0, 0)),
            pl.BlockSpec((d, tn), lambda i, j: (0, j)),
        ],
        out_specs=(pl.BlockSpec((tn // LANES, tm, LANES), lambda i, j: (j, i, 0)),
                   pl.BlockSpec((tm, d), lambda i, j: (i, 0))),
        compiler_params=_cparams(("parallel", "arbitrary")),
        name=name,
    )(x2d, g, w)


def _proj_kernel(h_ref, w_ref, o_ref):
    o_ref[...] = _dot(h_ref[...], w_ref[...]).astype(o_ref.dtype)


def _proj(h, w, *, tm, tn, name):
    m, d = h.shape
    n = w.shape[1]
    return pl.pallas_call(
        _proj_kernel,
        out_shape=jax.ShapeDtypeStruct((m, n), BF16),
        grid=(m // tm, n // tn),
        in_specs=[pl.BlockSpec((tm, d), lambda i, j: (i, 0)), pl.BlockSpec((d, tn), lambda i, j: (0, j))],
        out_specs=pl.BlockSpec((tm, tn), lambda i, j: (i, j)),
        compiler_params=_cparams(("parallel", "parallel")),
        name=name,
    )(h, w)


def _compress_kernel(x_ref, pe_ref, w1_ref, w2_ref, o_ref, ot_ref, xf_ref):
    half = CMP_STRIDE * HEAD_DIM
    xf_ref[...] = x_ref[0].astype(F32)
    rows = xf_ref.shape[0] // CMP_STRIDE
    x = jnp.concatenate([xf_ref[pl.ds(r, rows, stride=CMP_STRIDE), :] for r in range(CMP_STRIDE)], axis=1)
    pe = pe_ref[0]
    a = _dot((x + pe[0:1]).astype(BF16), w1_ref[0, :half, :])
    b = _dot((x + pe[1:2]).astype(BF16), w1_ref[0, half:, :])
    hid = a + pltpu.roll(b, rows - 1, 0)
    hid = hid * (1.0 / (1.0 + jnp.exp(-hid)))
    out = _dot(hid.astype(BF16), w2_ref[0])
    o_ref[0, 0] = out.astype(o_ref.dtype)
    ot_ref[0, 0] = out.T.astype(ot_ref.dtype)


def _compress(proj, kv_chunk0, pe, w1, w2, *, batch, seq):
    rows = seq // CMP_STRIDE
    width = CMP_STRIDE * HEAD_DIM
    n_out = 2 * NSA_GROUPS
    return pl.pallas_call(
        _compress_kernel,
        out_shape=(jax.ShapeDtypeStruct((batch, n_out, rows, HEAD_DIM), BF16),
                   jax.ShapeDtypeStruct((batch, n_out, HEAD_DIM, rows), BF16)),
        grid=(batch, n_out),
        in_specs=[
            pl.BlockSpec((1, seq, HEAD_DIM), lambda b, c: (kv_chunk0 + c, b, 0)),
            pl.BlockSpec((1, 2, width), lambda b, c: (c // NSA_GROUPS, 0, 0)),
            pl.BlockSpec((1, 2 * width, CMP_HIDDEN), lambda b, c: (c // NSA_GROUPS, 0, 0)),
            pl.BlockSpec((1, CMP_HIDDEN, HEAD_DIM), lambda b, c: (c // NSA_GROUPS, 0, 0)),
        ],
        out_specs=(pl.BlockSpec((1, 1, rows, HEAD_DIM), lambda b, c: (b, c, 0, 0)),
                   pl.BlockSpec((1, 1, HEAD_DIM, rows), lambda b, c: (b, c, 0, 0))),
        scratch_shapes=[pltpu.VMEM((seq, HEAD_DIM), F32)],
        compiler_params=_cparams(("parallel", "parallel")),
        name="nsa_compress",
    )(proj, pe, w1, w2)


def _bias_from_rel(rel, tbl_ref, head):
    max_exact = REL_BUCKETS // 2
    n = jnp.maximum(rel, 0)
    nf = jnp.maximum(n, 1).astype(F32)
    large = max_exact + (jnp.log(nf / max_exact) / math.log(REL_MAX_DIST / max_exact)
                         * (REL_BUCKETS - max_exact)).astype(jnp.int32)
    large = jnp.minimum(large, REL_BUCKETS - 1)
    bucket = jnp.where(n < max_exact, n, large)
    out = jnp.zeros(rel.shape, F32)
    for b in range(REL_BUCKETS):
        out = jnp.where(bucket == b, tbl_ref[b * NSA_HEADS + head], out)
    return out


def _cmp_bias_kernel(tbl_ref, o_ref):
    head, blk = pl.program_id(0), pl.program_id(1)
    rows, cols = o_ref.shape[1], o_ref.shape[2]
    n = lax.broadcasted_iota(jnp.int32, (rows, cols), 0)
    t = blk * cols + lax.broadcasted_iota(jnp.int32, (rows, cols), 1)
    o_ref[0] = _bias_from_rel(t - (n * CMP_STRIDE + CMP_BLOCK - 1), tbl_ref, head)


def _band_bias_kernel(tbl_ref, o_ref):
    head = pl.program_id(0)
    rows, cols = o_ref.shape[1], o_ref.shape[2]
    c = lax.broadcasted_iota(jnp.int32, (rows, cols), 0)
    i = lax.broadcasted_iota(jnp.int32, (rows, cols), 1)
    rel = i - c + QT
    far = tbl_ref[(REL_BUCKETS - 1) * NSA_HEADS + head]
    o_ref[0] = jnp.where(rel >= 0, (_bias_from_rel(rel, tbl_ref, head) - far) * (HEAD_DIM ** 0.5), NEG)


def _bias_tables(tbl_flat, seq):
    cols = 256
    smem = pl.BlockSpec(memory_space=pltpu.SMEM)
    cmp_bias = pl.pallas_call(
        _cmp_bias_kernel,
        out_shape=jax.ShapeDtypeStruct((NSA_HEADS, LANES, seq), F32),
        grid=(NSA_HEADS, seq // cols),
        in_specs=[smem],
        out_specs=pl.BlockSpec((1, LANES, cols), lambda h, r: (h, 0, r)),
        compiler_params=_cparams(("parallel", "parallel")),
        name="cmp_bias",
    )(tbl_flat)
    band = pl.pallas_call(
        _band_bias_kernel,
        out_shape=jax.ShapeDtypeStruct((NSA_HEADS, 2 * QT, QT), F32),
        grid=(NSA_HEADS,),
        in_specs=[smem],
        out_specs=pl.BlockSpec((1, 2 * QT, QT), lambda h: (h, 0, 0)),
        compiler_params=_cparams(("parallel",)),
        name="band_bias",
    )(tbl_flat)
    return cmp_bias, band


def _split3(x):
    hi = x.astype(BF16)
    r1 = x - hi.astype(F32)
    mid = r1.astype(BF16)
    lo = (r1 - mid.astype(F32)).astype(BF16)
    return hi, mid, lo


def _nsa_kernel(q_ref, kc_ref, vct_ref, ks_ref, vs_ref, kw_ref, vw_ref, gate_ref, cbias_ref, band_ref,
                o_ref, kpad_s, vt_s, kpad_w, vt_w, acc_s, acc_s2, acc_w, selb_ref, posb_ref):
    qi = pl.program_id(1)
    hpg, grps = NSA_HPG, NSA_GROUPS
    scale = HEAD_DIM ** -0.5
    n_tiles = vt_s.shape[1] - 1
    seq = n_tiles * QT
    n_sel = seq // SEL_BLOCK
    n_cmp = (seq - CMP_BLOCK) // CMP_STRIDE + 1
    blocks_per_tile = QT // SEL_BLOCK

    @pl.when(qi == 0)
    def _():
        for kpad, vt, k_ref, v_ref in ((kpad_s, vt_s, ks_ref, vs_ref), (kpad_w, vt_w, kw_ref, vw_ref)):
            for g in range(grps):
                kpad[g, :QT, :] = jnp.zeros((QT, HEAD_DIM), BF16)
                kpad[g, QT:, :] = k_ref[g]
                vt[g, 0] = jnp.zeros((HEAD_DIM, QT), BF16)
                for t in range(n_tiles):
                    vt[g, t + 1] = v_ref[g, t * QT:(t + 1) * QT, :].astype(F32).T.astype(BF16)
        neg_rows = jnp.full((PAD_BLOCKS, QT), NEG, F32)
        for g in range(grps):
            selb_ref[g, :PAD_BLOCKS, :] = neg_rows
        posb_ref[:PAD_BLOCKS, :] = neg_rows
        posb_ref[PAD_BLOCKS:, :] = jnp.zeros((n_sel, QT), F32)

    qs = [q_ref[g * hpg:(g + 1) * hpg].reshape(hpg * QT, HEAD_DIM) for g in range(grps)]
    sub = lax.broadcasted_iota(jnp.int32, (QT, QT), 0)
    lane = lax.broadcasted_iota(jnp.int32, (QT, QT), 1)
    valid_c = (qi * QT + lane) - (sub * CMP_STRIDE + CMP_BLOCK - 1) >= 0
    jb = lax.broadcasted_iota(jnp.int32, (n_sel, QT), 0)
    nn = lax.broadcasted_iota(jnp.int32, (n_sel, QT), 1)
    overlap_t = jnp.where((nn * CMP_STRIDE < jb * SEL_BLOCK + SEL_BLOCK)
                          & (nn * CMP_STRIDE + CMP_BLOCK > jb * SEL_BLOCK)
                          & (nn < n_cmp), 1.0, 0.0).astype(BF16)
    cur = (qi * QT + nn) // SEL_BLOCK
    forced = (jb == 0) | ((cur - jb >= 0) & (cur - jb < SEL_LOCAL))

    def scores(kpad, g, tile0, n_keys):
        start = pl.multiple_of((jnp.maximum(tile0, -1) + 1) * QT, QT)
        return _dot_nt(kpad[g, pl.ds(start, n_keys), :], qs[g])

    near = qi - 1
    mid_tile = qi - 3
    far_tile = qi - WINDOW // QT
    raw_w = [(scores(kpad_w, g, near, KCH), scores(kpad_w, g, mid_tile, KCH), scores(kpad_w, g, far_tile, QT))
             for g in range(grps)]
    raw_s = [scores(kpad_s, g, near, KCH) for g in range(grps)]

    o_cmp = []
    for g in range(grps):
        sc = _dot_nt(kc_ref[0, g], qs[g]) * scale
        p_heads = []
        p_sum = None
        for h in range(hpg):
            s = jnp.where(valid_c, sc[:, h * QT:(h + 1) * QT] + cbias_ref[g * hpg + h], NEG)
            mx = jnp.max(s, axis=0, keepdims=True)
            p = jnp.where(valid_c, jnp.exp(s - mx), 0.0)
            p = p / jnp.maximum(jnp.sum(p, axis=0, keepdims=True), 1e-30)
            p_heads.append(p.astype(BF16))
            p_sum = p if p_sum is None else p_sum + p
        o_cmp.append(_dot(vct_ref[0, g], jnp.concatenate(p_heads, axis=1)))

        hi, mid, lo = _split3(p_sum)
        imp = _dot(overlap_t, hi) + _dot(overlap_t, mid) + _dot(overlap_t, lo)
        imp = jnp.where(forced, FORCE_SCORE, imp)
        imp = jnp.where(jb <= cur, imp, -1.0)
        rank = jnp.zeros((n_sel, QT), F32)
        for i in range(n_sel):
            other = jnp.broadcast_to(imp[i:i + 1, :], (n_sel, QT))
            rank = rank + jnp.where(jb > i, jnp.where(other >= imp, 1.0, 0.0), jnp.where(other > imp, 1.0, 0.0))
        selb_ref[g, PAD_BLOCKS:, :] = jnp.where(rank < min(SEL_TOPK, n_sel), 0.0, NEG)

    def row_bias(ref, tile0, n_keys):
        r0 = PAD_BLOCKS + tile0 * blocks_per_tile
        return jnp.concatenate([jnp.broadcast_to(ref[pl.ds(r0 + b, 1), :], (SEL_BLOCK, QT))
                                for b in range(n_keys // SEL_BLOCK)], axis=0)

    def add_shared(s, bias):
        return jnp.concatenate([s[:, h * QT:(h + 1) * QT] + bias for h in range(hpg)], axis=1)

    def add_band(s, bias, g):
        return jnp.concatenate([s[:, h * QT:(h + 1) * QT] + (band_ref[g * hpg + h] + bias) for h in range(hpg)],
                               axis=1)

    def weighted_values(vt, g, tile0, p):
        t0 = jnp.maximum(tile0, -1) + 1
        pb = p.astype(BF16)
        pv = _dot(vt[g, t0], pb[:QT])
        for t in range(1, p.shape[0] // QT):
            pv = pv + _dot(vt[g, t0 + t], pb[t * QT:(t + 1) * QT])
        return pv

    to_exp2 = scale * math.log2(math.e)

    def flash_first(s, vt, g, tile0, acc_ref):
        m = jnp.max(s, axis=0, keepdims=True)
        p = jnp.exp2((s - m) * to_exp2)
        acc_ref[g] = weighted_values(vt, g, tile0, p)
        return m, jnp.sum(p, axis=0, keepdims=True)

    def flash_next(s, vt, g, tile0, acc_ref, m, l):
        m_new = jnp.maximum(m, jnp.max(s, axis=0, keepdims=True))
        alpha = jnp.exp2((m - m_new) * to_exp2)
        p = jnp.exp2((s - m_new) * to_exp2)
        acc_ref[g] = acc_ref[g] * alpha + weighted_values(vt, g, tile0, p)
        return m_new, alpha * l + jnp.sum(p, axis=0, keepdims=True)


    edge = jnp.where(sub > lane, 0.0, NEG)
    stats_w = [None] * grps
    for g in range(grps):
        s = add_band(raw_w[g][0], row_bias(posb_ref, near, KCH), g)
        stats_w[g] = flash_first(s, vt_w, g, near, acc_w)
    for g in range(grps):
        s = add_shared(raw_w[g][1], row_bias(posb_ref, mid_tile, KCH))
        stats_w[g] = flash_next(s, vt_w, g, mid_tile, acc_w, *stats_w[g])
    for g in range(grps):
        s = add_shared(raw_w[g][2], row_bias(posb_ref, far_tile, QT) + edge)
        stats_w[g] = flash_next(s, vt_w, g, far_tile, acc_w, *stats_w[g])

    stats_s = []
    for g in range(grps):
        s = add_band(raw_s[g], row_bias(selb_ref.at[g], near, KCH), g)
        stats_s.extend(flash_first(s, vt_s, g, near, acc_s))

    for g in range(grps):
        acc_s2[g] = jnp.zeros(acc_s2.shape[1:], F32)
        stats_s.extend((jnp.full((1, hpg * QT), NEG, F32), jnp.zeros((1, hpg * QT), F32)))
    n_far = (qi + 2) // 2 - 1

    def sel_body(it, carry):
        tiles = (qi + 1 - 2 * (2 * it + 2), qi + 1 - 2 * (2 * it + 3))
        raw = [[scores(kpad_s, g, t, KCH) for g in range(grps)] for t in tiles]
        out = list(carry)
        for par, (acc_ref, t) in enumerate(zip((acc_s, acc_s2), tiles)):
            for g in range(grps):
                k = 2 * (par * grps + g)
                s = add_shared(raw[par][g], row_bias(selb_ref.at[g], t, KCH))
                out[k], out[k + 1] = flash_next(s, vt_s, g, t, acc_ref, carry[k], carry[k + 1])
        return tuple(out)

    stats_s = lax.fori_loop(0, (n_far + 1) // 2, sel_body, tuple(stats_s))

    for g in range(grps):
        gate = _sigmoid(gate_ref[:, g * LANES:(g + 1) * LANES].astype(F32)).T
        m_a, l_a = stats_s[2 * g], stats_s[2 * g + 1]
        m_b, l_b = stats_s[2 * (grps + g)], stats_s[2 * (grps + g) + 1]
        m_all = jnp.maximum(m_a, m_b)
        w_a = jnp.exp2((m_a - m_all) * to_exp2)
        w_b = jnp.exp2((m_b - m_all) * to_exp2)
        inv_s = 1.0 / jnp.maximum(l_a * w_a + l_b * w_b, 1e-30)
        w_a, w_b = w_a * inv_s, w_b * inv_s
        inv_w = 1.0 / jnp.maximum(stats_w[g][1], 1e-30)
        for h in range(hpg):
            sl = slice(h * QT, (h + 1) * QT)
            o_t = (gate[h:h + 1] * o_cmp[g][:, sl]
                   + gate[hpg + h:hpg + h + 1] * (acc_s[g, :, sl] * w_a[:, sl] + acc_s2[g, :, sl] * w_b[:, sl])
                   + gate[2 * hpg + h:2 * hpg + h + 1] * (acc_w[g, :, sl] * inv_w[:, sl]))
            c0 = (g * hpg + h) * HEAD_DIM
            o_ref[:, c0:c0 + HEAD_DIM] = o_t.T.astype(o_ref.dtype)


def _nsa_attention(proj, cmp_kv, cmp_kv_t, gates, cmp_bias, band, *, batch, seq, kv_chunk0, gate_blk0):
    nq = seq // QT
    grps = NSA_GROUPS
    n_sel = seq // SEL_BLOCK
    rows_c = seq // CMP_STRIDE
    assert WINDOW // QT <= PAD_BLOCKS // (QT // SEL_BLOCK) and WINDOW == 4 * QT and KCH == 2 * QT
    assert kv_chunk0 % grps == 0 and gate_blk0 % grps == 0

    def kv_spec(branch, kv):
        blk = (kv_chunk0 + (branch * 2 + kv) * grps) // grps
        return pl.BlockSpec((grps, seq, HEAD_DIM), lambda b, i: (blk, b, 0))

    return pl.pallas_call(
        _nsa_kernel,
        out_shape=jax.ShapeDtypeStruct((batch * seq, NSA_HEADS * HEAD_DIM), BF16),
        grid=(batch, nq),
        in_specs=[
            pl.BlockSpec((NSA_HEADS, QT, HEAD_DIM), lambda b, i: (0, b * nq + i, 0)),
            pl.BlockSpec((1, grps, rows_c, HEAD_DIM), lambda b, i: (b, 0, 0, 0)),
            pl.BlockSpec((1, grps, HEAD_DIM, rows_c), lambda b, i: (b, 1, 0, 0)),
            kv_spec(1, 0), kv_spec(1, 1), kv_spec(2, 0), kv_spec(2, 1),
            pl.BlockSpec((QT, grps * LANES), lambda b, i: (b * nq + i, gate_blk0 // grps)),
            pl.BlockSpec((NSA_HEADS, LANES, QT), lambda b, i: (0, 0, i)),
            pl.BlockSpec((NSA_HEADS, KCH, QT), lambda b, i: (0, 0, 0)),
        ],
        out_specs=pl.BlockSpec((QT, NSA_HEADS * HEAD_DIM), lambda b, i: (b * nq + i, 0)),
        scratch_shapes=[
            pltpu.VMEM((grps, seq + QT, HEAD_DIM), BF16),
            pltpu.VMEM((grps, nq + 1, HEAD_DIM, QT), BF16),
            pltpu.VMEM((grps, seq + QT, HEAD_DIM), BF16),
            pltpu.VMEM((grps, nq + 1, HEAD_DIM, QT), BF16),
            pltpu.VMEM((grps, HEAD_DIM, NSA_HPG * QT), F32),
            pltpu.VMEM((grps, HEAD_DIM, NSA_HPG * QT), F32),
            pltpu.VMEM((grps, HEAD_DIM, NSA_HPG * QT), F32),
            pltpu.VMEM((grps, PAD_BLOCKS + n_sel, QT), F32),
            pltpu.VMEM((PAD_BLOCKS + n_sel, QT), F32),
        ],
        compiler_params=_cparams(("arbitrary", "arbitrary")),
        name="nsa_attention",
    )(proj, cmp_kv, cmp_kv_t, proj, proj, proj, proj, gates, cmp_bias, band)


def _retention_kernel(q_ref, k_ref, v_ref, g_ref, cos_ref, sin_ref, dmat_ref, xi_ref, zeta_ref,
                      o_ref, state_ref, *, g_chunk):
    @pl.when(pl.program_id(1) == 0)
    def _():
        state_ref[...] = jnp.zeros(state_ref.shape, F32)

    cos, sin = cos_ref[...], sin_ref[...]
    half = RET_QK_DIM // 2
    heads = range(RET_HEADS)
    qb, kr, v, inner, cross = [], [], [], [], []
    for h in heads:
        q = q_ref[h].astype(F32)
        k = k_ref[h].astype(F32)
        qb.append((q * cos + pltpu.roll(q, half, 1) * sin).astype(BF16))
        kr.append((k * cos + pltpu.roll(k, half, 1) * sin) * (RET_QK_DIM ** -0.5))
        v.append(jnp.concatenate([v_ref[2 * h], v_ref[2 * h + 1]], axis=1))
    for h in heads:
        inner.append(_dot_nt(qb[h], kr[h].astype(BF16)))
        cross.append(_dot(qb[h], state_ref[h].astype(BF16)))
    for h in heads:
        kz = (kr[h] * zeta_ref[h]).T.astype(BF16)
        state_ref[h] = state_ref[h] * g_chunk[h] + _dot(kz, v[h])
    for h in heads:
        o = _dot((inner[h] * dmat_ref[h]).astype(BF16), v[h]) + cross[h] * xi_ref[h][:, 0:1]
        mu = jnp.mean(o, axis=-1, keepdims=True)
        var = jnp.mean(jnp.square(o - mu), axis=-1, keepdims=True)
        on = (o - mu) * lax.rsqrt(var + EPS)
        gate = g_ref[:, h * RET_V_DIM:(h + 1) * RET_V_DIM].astype(F32)
        gate = gate * (1.0 / (1.0 + jnp.exp(-gate)))
        o_ref[:, h * RET_V_DIM:(h + 1) * RET_V_DIM] = (on * gate).astype(o_ref.dtype)


def _retention(proj, gates, *, batch, seq, q_blk, k_blk, v_blk):
    c = RET_CHUNK
    n = seq // c
    heads = RET_HEADS
    dk = RET_QK_DIM
    theta = 1.0 / (10000.0 ** np.linspace(0.0, 1.0, dk // 2))
    ang = np.arange(seq)[:, None] * theta[None, :]
    cos = np.concatenate([np.cos(ang), np.cos(ang)], axis=1)
    sin = np.concatenate([-np.sin(ang), np.sin(ang)], axis=1)
    log_g = np.log(1.0 - np.exp2(-5.0 - np.arange(heads)))
    j = np.arange(c)
    diff = j[:, None] - j[None, :]
    dmat = np.where(diff >= 0, np.exp(np.maximum(diff, 0)[None] * log_g[:, None, None]), 0.0)
    xi = np.broadcast_to(np.exp((j + 1.0)[None, :] * log_g[:, None])[:, :, None], (heads, c, LANES))
    zeta = np.broadcast_to(np.exp((c - 1.0 - j)[None, :] * log_g[:, None])[:, :, None], (heads, c, LANES))
    g_chunk = tuple(float(v) for v in np.exp(c * log_g))
    consts = [jnp.asarray(a, F32) for a in (cos, sin, dmat, xi, zeta)]

    full3 = lambda shape: pl.BlockSpec(shape, lambda b, i: (0, 0, 0))
    return pl.pallas_call(
        functools.partial(_retention_kernel, g_chunk=g_chunk),
        out_shape=jax.ShapeDtypeStruct((batch * seq, heads * RET_V_DIM), BF16),
        grid=(batch, n),
        in_specs=[
            pl.BlockSpec((heads, c, LANES), lambda b, i: (q_blk, b * n + i, 0)),
            pl.BlockSpec((heads, c, LANES), lambda b, i: (k_blk, b * n + i, 0)),
            pl.BlockSpec((2 * heads, c, LANES), lambda b, i: (v_blk, b * n + i, 0)),
            pl.BlockSpec((c, heads * RET_V_DIM), lambda b, i: (b * n + i, 0)),
            pl.BlockSpec((c, dk), lambda b, i: (i, 0)),
            pl.BlockSpec((c, dk), lambda b, i: (i, 0)),
            full3((heads, c, c)), full3((heads, c, LANES)), full3((heads, c, LANES)),
        ],
        out_specs=pl.BlockSpec((c, heads * RET_V_DIM), lambda b, i: (b * n + i, 0)),
        scratch_shapes=[pltpu.VMEM((heads, dk, RET_V_DIM), F32)],
        compiler_params=_cparams(("parallel", "arbitrary")),
        name="retention",
    )(proj, proj, proj, gates, *consts)


def _mem_attn_kernel(q_ref, k_ref, v_ref, o_ref):
    scale = MEM_HEAD_DIM ** -0.5
    scores = [(_dot_nt(q_ref[2 * h], k_ref[2 * h]) + _dot_nt(q_ref[2 * h + 1], k_ref[2 * h + 1])) * scale
              for h in range(MEM_HEADS)]
    for h in range(MEM_HEADS):
        s = scores[h]
        mx = jnp.max(s, axis=-1, keepdims=True)
        p = jnp.exp(s - mx)
        p = (p / jnp.sum(p, axis=-1, keepdims=True)).astype(BF16)
        for half in range(2):
            c0 = h * MEM_HEAD_DIM + half * LANES
            o_ref[:, c0:c0 + LANES] = _dot(p, v_ref[2 * h + half]).astype(o_ref.dtype)


def _mem_attention(proj, mkv, *, batch, seq, mem_len, q_blk, tq):
    nq = seq // tq
    chunks = MEM_HEADS * MEM_HEAD_DIM // LANES
    return pl.pallas_call(
        _mem_attn_kernel,
        out_shape=jax.ShapeDtypeStruct((batch * seq, MEM_HEADS * MEM_HEAD_DIM), BF16),
        grid=(batch, nq),
        in_specs=[
            pl.BlockSpec((chunks, tq, LANES), lambda b, i: (q_blk, b * nq + i, 0)),
            pl.BlockSpec((chunks, mem_len, LANES), lambda b, i: (0, b, 0)),
            pl.BlockSpec((chunks, mem_len, LANES), lambda b, i: (1, b, 0)),
        ],
        out_specs=pl.BlockSpec((tq, MEM_HEADS * MEM_HEAD_DIM), lambda b, i: (b * nq + i, 0)),
        compiler_params=_cparams(("parallel", "parallel")),
        name="mem_attention",
    )(proj, mkv, mkv)


def _sigmoid(x):
    return 1.0 / (1.0 + jnp.exp(-x))


def _merge_kernel(ya_ref, yb_ref, yc_ref, wa_ref, wb_ref, wc_ref, ga_ref, gb_ref, gc_ref, o_ref):
    out = _sigmoid(ga_ref[...].astype(F32)) * _dot(ya_ref[...], wa_ref[...])
    out = out + _sigmoid(gb_ref[...].astype(F32)) * _dot(yb_ref[...], wb_ref[...])
    out = out + _sigmoid(gc_ref[...].astype(F32)) * _dot(yc_ref[...], wc_ref[...])
    o_ref[...] = out.astype(o_ref.dtype)


def _merge(ya, yb, yc, wa, wb, wc, gates, *, d_model, gate_col0, tm, tn):
    m, kdim = ya.shape
    y_spec = pl.BlockSpec((tm, kdim), lambda i, j: (i, 0))
    w_spec = pl.BlockSpec((kdim, tn), lambda i, j: (0, j))

    def g_spec(branch):
        base = (gate_col0 + branch * d_model) // tn
        return pl.BlockSpec((tm, tn), lambda i, j: (i, base + j))

    return pl.pallas_call(
        _merge_kernel,
        out_shape=jax.ShapeDtypeStruct((m, d_model), BF16),
        grid=(m // tm, d_model // tn),
        in_specs=[y_spec, y_spec, y_spec, w_spec, w_spec, w_spec, g_spec(0), g_spec(1), g_spec(2)],
        out_specs=pl.BlockSpec((tm, tn), lambda i, j: (i, j)),
        compiler_params=_cparams(("parallel", "parallel")),
        name="branch_merge",
    )(ya, yb, yc, wa, wb, wc, gates, gates, gates)


def _out_proj_kernel(y_ref, w_ref, g_ref, x_ref, o_ref):
    o_ref[...] = x_ref[...] + _rms(_dot(y_ref[...], w_ref[...]), g_ref[...])


def _out_proj(y, w, g, x2d, *, tm):
    m, d = x2d.shape
    return pl.pallas_call(
        _out_proj_kernel,
        out_shape=jax.ShapeDtypeStruct((m, d), F32),
        grid=(m // tm,),
        in_specs=[
            pl.BlockSpec((tm, d), lambda i: (i, 0)),
            pl.BlockSpec((d, d), lambda i: (0, 0)),
            pl.BlockSpec((1, d), lambda i: (0, 0)),
            pl.BlockSpec((tm, d), lambda i: (i, 0)),
        ],
        out_specs=pl.BlockSpec((tm, d), lambda i: (i, 0)),
        compiler_params=_cparams(("parallel",)),
        name="out_proj",
    )(y, w, g, x2d)


def _gelu_tanh(x):
    return 0.5 * x * (1.0 + jnp.tanh(math.sqrt(2.0 / math.pi) * (x + 0.044715 * (x * x * x))))


def _ffn_kernel(x_ref, halo_ref, gpre_ref, wg_ref, wv_ref, cwg_ref, cwv_ref, cbg_ref, cbv_ref,
                wd_ref, gpost_ref, o_ref, h_ref, acc_ref, *, seq):
    i, j = pl.program_id(0), pl.program_id(1)
    tm = x_ref.shape[0]

    @pl.when(j == 0)
    def _():
        h_ref[HALO:, :] = _rms(x_ref[...], gpre_ref[...]).astype(BF16)
        keep = jnp.where((i * tm) % seq == 0, 0.0, 1.0)
        h_ref[:HALO, :] = (_rms(halo_ref[...], gpre_ref[...]) * keep).astype(BF16)
        acc_ref[...] = jnp.zeros(acc_ref.shape, F32)

    h = h_ref[...]

    def conv(u, w_ref, b_ref):
        w = w_ref[...]
        y = b_ref[...] + w[2:3] * u[HALO:]
        y = y + w[1:2] * pltpu.roll(u, 1, 0)[HALO:]
        return y + w[0:1] * pltpu.roll(u, 2, 0)[HALO:]

    yg = conv(_dot(h, wg_ref[...]), cwg_ref, cbg_ref)
    yv = conv(_dot(h, wv_ref[...]), cwv_ref, cbv_ref)
    act = (_gelu_tanh(yg) * yv).astype(BF16)
    acc_ref[...] += _dot(act, wd_ref[...])

    @pl.when(j == pl.num_programs(1) - 1)
    def _():
        o_ref[...] = x_ref[...] + _rms(acc_ref[...], gpost_ref[...])


def _ffn(x2d, gpre, w_up, conv_w, conv_b, w_down, gpost, *, seq, tm, tn):
    m, d = x2d.shape
    d_ff = w_down.shape[0]
    nj = d_ff // tn
    halo_blocks = tm // HALO
    return pl.pallas_call(
        functools.partial(_ffn_kernel, seq=seq),
        out_shape=jax.ShapeDtypeStruct((m, d), F32),
        grid=(m // tm, nj),
        in_specs=[
            pl.BlockSpec((tm, d), lambda i, j: (i, 0)),
            pl.BlockSpec((HALO, d), lambda i, j: (jnp.maximum(i * halo_blocks - 1, 0), 0)),
            pl.BlockSpec((1, d), lambda i, j: (0, 0)),
            pl.BlockSpec((d, tn), lambda i, j: (0, j)),
            pl.BlockSpec((d, tn), lambda i, j: (0, nj + j)),
            pl.BlockSpec((CONV_WIDTH, tn), lambda i, j: (0, j)),
            pl.BlockSpec((CONV_WIDTH, tn), lambda i, j: (0, nj + j)),
            pl.BlockSpec((1, tn), lambda i, j: (0, j)),
            pl.BlockSpec((1, tn), lambda i, j: (0, nj + j)),
            pl.BlockSpec((tn, d), lambda i, j: (j, 0)),
            pl.BlockSpec((1, d), lambda i, j: (0, 0)),
        ],
        out_specs=pl.BlockSpec((tm, d), lambda i, j: (i, 0)),
        scratch_shapes=[pltpu.VMEM((tm + HALO, d), BF16), pltpu.VMEM((tm, d), F32)],
        compiler_params=_cparams(("parallel", "arbitrary")),
        name="conv_ffn",
    )(x2d, x2d, gpre, w_up, w_up, conv_w, conv_w, conv_b, conv_b, w_down, gpost)


def _pick(n, prefs):
    for t in prefs:
        if n % t == 0:
            return t
    raise ValueError(f"no tile in {prefs} divides {n}")


class _Tiles(NamedTuple):
    proj_rows: int
    proj_cols: int
    gate_cols: int
    merge_cols: int
    out_rows: int
    ffn_rows: int
    ffn_cols: int
    mem_q_rows: int


def _tiles(m):
    return _Tiles(proj_rows=_pick(m, (1024, 512, 256, 128)), proj_cols=2 * MXU_WIDTH, gate_cols=3 * MXU_WIDTH,
                  merge_cols=4 * MXU_WIDTH,
                  out_rows=_pick(m, (512, 256, 128)), ffn_rows=_pick(m, (512, 256, 128)), ffn_cols=2 * MXU_WIDTH,
                  mem_q_rows=512)


def kernel(x, mem, w_in, cmp_pe_k, cmp_w1_k, cmp_w2_k, cmp_pe_v, cmp_w1_v, cmp_w2_v, rel_bias, w_mem_kv,
           w_br_nsa, w_br_ret, w_br_mem, w_o, w_up, conv_w, conv_b, w_down, g_pre_mix, g_post_mix, g_mem,
           g_pre_ffn, g_post_ffn):
    batch, seq, d_model = x.shape
    mem_len = mem.shape[1]
    depth = w_in.shape[0]
    m = batch * seq
    grps, hpg, dh = NSA_GROUPS, NSA_HPG, HEAD_DIM

    nsa_q = NSA_HEADS * dh
    nsa_kv = 3 * 2 * grps * dh
    nsa_gates = 3 * NSA_HEADS
    ret_qk = RET_HEADS * RET_QK_DIM
    ret_v = RET_HEADS * RET_V_DIM
    mem_q = MEM_HEADS * MEM_HEAD_DIM
    splits = (nsa_q, nsa_kv, nsa_gates, ret_qk, ret_qk, ret_v, ret_v, mem_q, 3 * d_model)
    offs = np.concatenate([[0], np.cumsum(splits)])
    o_nq, o_nkv, o_ng, o_rq, o_rk, o_rv, o_rg, o_mq, o_bg = (int(v) for v in offs[:-1])

    q_chunk0 = 0
    rv_chunk0 = q_chunk0 + nsa_q // LANES
    mq_chunk0 = rv_chunk0 + ret_v // LANES
    rq_chunk0 = mq_chunk0 + mem_q // LANES
    rk_chunk0 = rq_chunk0 + ret_qk // LANES
    kv_chunk0 = rk_chunk0 + ret_qk // LANES
    bg_col0 = ret_v
    ng_col0 = bg_col0 + 3 * d_model
    tiles = _tiles(m)
    gate_cols = -(-(ng_col0 + grps * LANES) // tiles.gate_cols) * tiles.gate_cols

    tbl_flat = rel_bias.reshape(-1)
    cmp_bias, band = _bias_tables(tbl_flat, seq)
    x2d = x.reshape(m, d_model)

    for l in range(depth):
        w = w_in[l]
        cols = lambda o, n: w[:, o:o + n]
        w_attn = jnp.concatenate([cols(o_nq, nsa_q), cols(o_rv, ret_v), cols(o_mq, mem_q), cols(o_rq, ret_qk),
                                  cols(o_rk, ret_qk), cols(o_nkv, nsa_kv)], axis=1).astype(BF16)
        ng = cols(o_ng, nsa_gates).reshape(d_model, 3, grps, hpg).transpose(0, 2, 1, 3).reshape(d_model, grps, 3 * hpg)
        ng = jnp.pad(ng, ((0, 0), (0, 0), (0, LANES - 3 * hpg))).reshape(d_model, grps * LANES)
        w_gate = jnp.concatenate([cols(o_rg, ret_v), cols(o_bg, 3 * d_model), ng], axis=1)
        w_gate = jnp.pad(w_gate, ((0, 0), (0, gate_cols - w_gate.shape[1]))).astype(BF16)

        g_pre = g_pre_mix[l].reshape(1, d_model)
        proj, h_mix = _norm_proj(x2d, g_pre, w_attn, tm=tiles.proj_rows, tn=tiles.proj_cols, name="in_proj_heads")
        gates = _proj(h_mix, w_gate, tm=tiles.proj_rows, tn=tiles.gate_cols, name="in_proj_gates")

        pe = jnp.stack([cmp_pe_k[l], cmp_pe_v[l]]).reshape(2, 2, CMP_STRIDE * dh)
        w1 = jnp.stack([cmp_w1_k[l], cmp_w1_v[l]]).astype(BF16)
        w2 = jnp.stack([cmp_w2_k[l], cmp_w2_v[l]]).astype(BF16)
        cmp_kv, cmp_kv_t = _compress(proj, kv_chunk0, pe, w1, w2, batch=batch, seq=seq)
        y_a = _nsa_attention(proj, cmp_kv, cmp_kv_t, gates, cmp_bias, band, batch=batch, seq=seq,
                             kv_chunk0=kv_chunk0, gate_blk0=ng_col0 // LANES)

        y_b = _retention(proj, gates, batch=batch, seq=seq, q_blk=rq_chunk0 // RET_HEADS,
                         k_blk=rk_chunk0 // RET_HEADS, v_blk=rv_chunk0 // (2 * RET_HEADS))

        mem2d = mem.reshape(batch * mem_len, d_model)
        mkv, _ = _norm_proj(mem2d, g_mem[l].reshape(1, d_model), w_mem_kv[l].astype(BF16),
                            tm=_tiles(batch * mem_len).proj_rows, tn=tiles.proj_cols, name="mem_kv_proj")
        y_c = _mem_attention(proj, mkv, batch=batch, seq=seq, mem_len=mem_len,
                             q_blk=mq_chunk0 // (mem_q // LANES), tq=tiles.mem_q_rows)

        merged = _merge(y_a, y_b, y_c, w_br_nsa[l].astype(BF16), w_br_ret[l].astype(BF16),
                        w_br_mem[l].astype(BF16), gates, d_model=d_model, gate_col0=bg_col0,
                        tm=tiles.proj_rows, tn=tiles.merge_cols)
        x2d = _out_proj(merged, w_o[l].astype(BF16), g_post_mix[l].reshape(1, d_model), x2d, tm=tiles.out_rows)

        x2d = _ffn(x2d, g_pre_ffn[l].reshape(1, d_model), w_up[l].astype(BF16), conv_w[l],
                   conv_b[l].reshape(1, -1), w_down[l].astype(BF16), g_post_ffn[l].reshape(1, d_model),
                   seq=seq, tm=tiles.ffn_rows, tn=tiles.ffn_cols)
    return x2d.reshape(batch, seq, d_model)
```

```python
import functools
import math
from typing import NamedTuple

import jax
import jax.numpy as jnp
import numpy as np
from jax import lax
from jax.experimental import pallas as pl
from jax.experimental.pallas import tpu as pltpu

F32 = jnp.float32
BF16 = jnp.bfloat16

LANES = 128
MXU_WIDTH = 256
EPS = 1e-6
NEG = -1e30

NSA_HEADS = 8
NSA_GROUPS = 2
NSA_HPG = NSA_HEADS // NSA_GROUPS
HEAD_DIM = 128
CMP_BLOCK = 32
CMP_STRIDE = 16
CMP_HIDDEN = 256
SEL_BLOCK = 64
SEL_TOPK = 16
SEL_LOCAL = 2
FORCE_SCORE = 1e4
WINDOW = 512
RET_HEADS = 4
RET_QK_DIM = 128
RET_V_DIM = 256
RET_CHUNK = 128
MEM_HEADS = 4
MEM_HEAD_DIM = 256
REL_BUCKETS = 32
REL_MAX_DIST = 128
CONV_WIDTH = 3

QT = 128
KCH = 2 * QT
PAD_BLOCKS = 8
HALO = 16

VMEM_LIMIT = 56 * 1024 * 1024


def _cparams(sem, vmem=VMEM_LIMIT):
    return pltpu.CompilerParams(dimension_semantics=sem, vmem_limit_bytes=vmem)


def _dot(a, b):
    return jnp.dot(a, b, preferred_element_type=F32)


def _dot_nt(a, b):
    return lax.dot_general(a, b, (((1,), (1,)), ((), ())), preferred_element_type=F32)


def _rms(x, g):
    ms = jnp.mean(x * x, axis=-1, keepdims=True)
    return x * lax.rsqrt(ms + EPS) * g


def _norm_proj_kernel(x_ref, g_ref, w_ref, o_ref, h_ref):
    @pl.when(pl.program_id(1) == 0)
    def _():
        h_ref[...] = _rms(x_ref[...], g_ref[...]).astype(BF16)

    acc = _dot(h_ref[...], w_ref[...])
    for c in range(acc.shape[1] // LANES):
        o_ref[c] = acc[:, c * LANES:(c + 1) * LANES].astype(o_ref.dtype)


def _norm_proj(x2d, g, w, *, tm, tn, name):
    m, d = x2d.shape
    n = w.shape[1]
    return pl.pallas_call(
        _norm_proj_kernel,
        out_shape=(jax.ShapeDtypeStruct((n // LANES, m, LANES), BF16), jax.ShapeDtypeStruct((m, d), BF16)),
        grid=(m // tm, n // tn),
        in_specs=[
            pl.BlockSpec((tm, d), lambda i, j: (i, 0)),
            pl.BlockSpec((1, d), lambda i, j: (0, 0)),
            pl.BlockSpec((d, tn), lambda i, j: (0, j)),
        ],
        out_specs=(pl.BlockSpec((tn // LANES, tm, LANES), lambda i, j: (j, i, 0)),
                   pl.BlockSpec((tm, d), lambda i, j: (i, 0))),
        compiler_params=_cparams(("parallel", "arbitrary")),
        name=name,
    )(x2d, g, w)


def _proj_kernel(h_ref, w_ref, o_ref):
    o_ref[...] = _dot(h_ref[...], w_ref[...]).astype(o_ref.dtype)


def _proj(h, w, *, tm, tn, name):
    m, d = h.shape
    n = w.shape[1]
    return pl.pallas_call(
        _proj_kernel,
        out_shape=jax.ShapeDtypeStruct((m, n), BF16),
        grid=(m // tm, n // tn),
        in_specs=[pl.BlockSpec((tm, d), lambda i, j: (i, 0)), pl.BlockSpec((d, tn), lambda i, j: (0, j))],
        out_specs=pl.BlockSpec((tm, tn), lambda i, j: (i, j)),
        compiler_params=_cparams(("parallel", "parallel")),
        name=name,
    )(h, w)


def _compress_kernel(x_ref, pe_ref, w1_ref, w2_ref, o_ref, ot_ref, xf_ref):
    half = CMP_STRIDE * HEAD_DIM
    xf_ref[...] = x_ref[0].astype(F32)
    rows = xf_ref.shape[0] // CMP_STRIDE
    x = jnp.concatenate([xf_ref[pl.ds(r, rows, stride=CMP_STRIDE), :] for r in range(CMP_STRIDE)], axis=1)
    pe = pe_ref[0]
    a = _dot((x + pe[0:1]).astype(BF16), w1_ref[0, :half, :])
    b = _dot((x + pe[1:2]).astype(BF16), w1_ref[0, half:, :])
    hid = a + pltpu.roll(b, rows - 1, 0)
    hid = hid * (1.0 / (1.0 + jnp.exp(-hid)))
    out = _dot(hid.astype(BF16), w2_ref[0])
    o_ref[0, 0] = out.astype(o_ref.dtype)
    ot_ref[0, 0] = out.T.astype(ot_ref.dtype)


def _compress(proj, kv_chunk0, pe, w1, w2, *, batch, seq):
    rows = seq // CMP_STRIDE
    width = CMP_STRIDE * HEAD_DIM
    n_out = 2 * NSA_GROUPS
    return pl.pallas_call(
        _compress_kernel,
        out_shape=(jax.ShapeDtypeStruct((batch, n_out, rows, HEAD_DIM), BF16),
                   jax.ShapeDtypeStruct((batch, n_out, HEAD_DIM, rows), BF16)),
        grid=(batch, n_out),
        in_specs=[
            pl.BlockSpec((1, seq, HEAD_DIM), lambda b, c: (kv_chunk0 + c, b, 0)),
            pl.BlockSpec((1, 2, width), lambda b, c: (c // NSA_GROUPS, 0, 0)),
            pl.BlockSpec((1, 2 * width, CMP_HIDDEN), lambda b, c: (c // NSA_GROUPS, 0, 0)),
            pl.BlockSpec((1, CMP_HIDDEN, HEAD_DIM), lambda b, c: (c // NSA_GROUPS, 0, 0)),
        ],
        out_specs=(pl.BlockSpec((1, 1, rows, HEAD_DIM), lambda b, c: (b, c, 0, 0)),
                   pl.BlockSpec((1, 1, HEAD_DIM, rows), lambda b, c: (b, c, 0, 0))),
        scratch_shapes=[pltpu.VMEM((seq, HEAD_DIM), F32)],
        compiler_params=_cparams(("parallel", "parallel")),
        name="nsa_compress",
    )(proj, pe, w1, w2)


def _bias_from_rel(rel, tbl_ref, head):
    max_exact = REL_BUCKETS // 2
    n = jnp.maximum(rel, 0)
    nf = jnp.maximum(n, 1).astype(F32)
    large = max_exact + (jnp.log(nf / max_exact) / math.log(REL_MAX_DIST / max_exact)
                         * (REL_BUCKETS - max_exact)).astype(jnp.int32)
    large = jnp.minimum(large, REL_BUCKETS - 1)
    bucket = jnp.where(n < max_exact, n, large)
    out = jnp.zeros(rel.shape, F32)
    for b in range(REL_BUCKETS):
        out = jnp.where(bucket == b, tbl_ref[b * NSA_HEADS + head], out)
    return out


def _cmp_bias_kernel(tbl_ref, o_ref):
    head, blk = pl.program_id(0), pl.program_id(1)
    rows, cols = o_ref.shape[1], o_ref.shape[2]
    n = lax.broadcasted_iota(jnp.int32, (rows, cols), 0)
    t = blk * cols + lax.broadcasted_iota(jnp.int32, (rows, cols), 1)
    o_ref[0] = _bias_from_rel(t - (n * CMP_STRIDE + CMP_BLOCK - 1), tbl_ref, head)


def _band_bias_kernel(tbl_ref, o_ref):
    head = pl.program_id(0)
    rows, cols = o_ref.shape[1], o_ref.shape[2]
    c = lax.broadcasted_iota(jnp.int32, (rows, cols), 0)
    i = lax.broadcasted_iota(jnp.int32, (rows, cols), 1)
    rel = i - c + QT
    far = tbl_ref[(REL_BUCKETS - 1) * NSA_HEADS + head]
    o_ref[0] = jnp.where(rel >= 0, (_bias_from_rel(rel, tbl_ref, head) - far) * (HEAD_DIM ** 0.5), NEG)


def _bias_tables(tbl_flat, seq):
    cols = 256
    smem = pl.BlockSpec(memory_space=pltpu.SMEM)
    cmp_bias = pl.pallas_call(
        _cmp_bias_kernel,
        out_shape=jax.ShapeDtypeStruct((NSA_HEADS, LANES, seq), F32),
        grid=(NSA_HEADS, seq // cols),
        in_specs=[smem],
        out_specs=pl.BlockSpec((1, LANES, cols), lambda h, r: (h, 0, r)),
        compiler_params=_cparams(("parallel", "parallel")),
        name="cmp_bias",
    )(tbl_flat)
    band = pl.pallas_call(
        _band_bias_kernel,
        out_shape=jax.ShapeDtypeStruct((NSA_HEADS, 2 * QT, QT), F32),
        grid=(NSA_HEADS,),
        in_specs=[smem],
        out_specs=pl.BlockSpec((1, 2 * QT, QT), lambda h: (h, 0, 0)),
        compiler_params=_cparams(("parallel",)),
        name="band_bias",
    )(tbl_flat)
    return cmp_bias, band


def _split3(x):
    hi = x.astype(BF16)
    r1 = x - hi.astype(F32)
    mid = r1.astype(BF16)
    lo = (r1 - mid.astype(F32)).astype(BF16)
    return hi, mid, lo


def _nsa_kernel(q_ref, kc_ref, vct_ref, ks_ref, vs_ref, kw_ref, vw_ref, gate_ref, cbias_ref, band_ref,
                o_ref, kpad_s, vt_s, kpad_w, vt_w, acc_s, acc_s2, acc_w, selb_ref, posb_ref):
    qi = pl.program_id(1)
    hpg, grps = NSA_HPG, NSA_GROUPS
    scale = HEAD_DIM ** -0.5
    n_tiles = vt_s.shape[1] - 1
    seq = n_tiles * QT
    n_sel = seq // SEL_BLOCK
    n_cmp = (seq - CMP_BLOCK) // CMP_STRIDE + 1
    blocks_per_tile = QT // SEL_BLOCK

    @pl.when(qi == 0)
    def _():
        for kpad, vt, k_ref, v_ref in ((kpad_s, vt_s, ks_ref, vs_ref), (kpad_w, vt_w, kw_ref, vw_ref)):
            for g in range(grps):
                kpad[g, :QT, :] = jnp.zeros((QT, HEAD_DIM), BF16)
                kpad[g, QT:, :] = k_ref[g]
                vt[g, 0] = jnp.zeros((HEAD_DIM, QT), BF16)
                for t in range(n_tiles):
                    vt[g, t + 1] = v_ref[g, t * QT:(t + 1) * QT, :].astype(F32).T.astype(BF16)
        neg_rows = jnp.full((PAD_BLOCKS, QT), NEG, F32)
        for g in range(grps):
            selb_ref[g, :PAD_BLOCKS, :] = neg_rows
        posb_ref[:PAD_BLOCKS, :] = neg_rows
        posb_ref[PAD_BLOCKS:, :] = jnp.zeros((n_sel, QT), F32)

    qs = [q_ref[g * hpg:(g + 1) * hpg].reshape(hpg * QT, HEAD_DIM) for g in range(grps)]
    sub = lax.broadcasted_iota(jnp.int32, (QT, QT), 0)
    lane = lax.broadcasted_iota(jnp.int32, (QT, QT), 1)
    valid_c = (qi * QT + lane) - (sub * CMP_STRIDE + CMP_BLOCK - 1) >= 0
    jb = lax.broadcasted_iota(jnp.int32, (n_sel, QT), 0)
    nn = lax.broadcasted_iota(jnp.int32, (n_sel, QT), 1)
    overlap_t = jnp.where((nn * CMP_STRIDE < jb * SEL_BLOCK + SEL_BLOCK)
                          & (nn * CMP_STRIDE + CMP_BLOCK > jb * SEL_BLOCK)
                          & (nn < n_cmp), 1.0, 0.0).astype(BF16)
    cur = (qi * QT + nn) // SEL_BLOCK
    forced = (jb == 0) | ((cur - jb >= 0) & (cur - jb < SEL_LOCAL))

    def scores(kpad, g, tile0, n_keys):
        start = pl.multiple_of((jnp.maximum(tile0, -1) + 1) * QT, QT)
        return _dot_nt(kpad[g, pl.ds(start, n_keys), :], qs[g])

    near = qi - 1
    mid_tile = qi - 3
    far_tile = qi - WINDOW // QT
    raw_w = [(scores(kpad_w, g, near, KCH), scores(kpad_w, g, mid_tile, KCH), scores(kpad_w, g, far_tile, QT))
             for g in range(grps)]
    raw_s = [scores(kpad_s, g, near, KCH) for g in range(grps)]

    o_cmp = []
    for g in range(grps):
        sc = _dot_nt(kc_ref[0, g], qs[g]) * scale
        p_heads = []
        p_sum = None
        for h in range(hpg):
            s = jnp.where(valid_c, sc[:, h * QT:(h + 1) * QT] + cbias_ref[g * hpg + h], NEG)
            mx = jnp.max(s, axis=0, keepdims=True)
            p = jnp.where(valid_c, jnp.exp(s - mx), 0.0)
            p = p / jnp.maximum(jnp.sum(p, axis=0, keepdims=True), 1e-30)
            p_heads.append(p.astype(BF16))
            p_sum = p if p_sum is None else p_sum + p
        o_cmp.append(_dot(vct_ref[0, g], jnp.concatenate(p_heads, axis=1)))

        hi, mid, lo = _split3(p_sum)
        imp = _dot(overlap_t, hi) + _dot(overlap_t, mid) + _dot(overlap_t, lo)
        imp = jnp.where(forced, FORCE_SCORE, imp)
        imp = jnp.where(jb <= cur, imp, -1.0)
        rank = jnp.zeros((n_sel, QT), F32)
        for i in range(n_sel):
            other = jnp.broadcast_to(imp[i:i + 1, :], (n_sel, QT))
            rank = rank + jnp.where(jb > i, jnp.where(other >= imp, 1.0, 0.0), jnp.where(other > imp, 1.0, 0.0))
        selb_ref[g, PAD_BLOCKS:, :] = jnp.where(rank < min(SEL_TOPK, n_sel), 0.0, NEG)

    def row_bias(ref, tile0, n_keys):
        r0 = PAD_BLOCKS + tile0 * blocks_per_tile
        return jnp.concatenate([jnp.broadcast_to(ref[pl.ds(r0 + b, 1), :], (SEL_BLOCK, QT))
                                for b in range(n_keys // SEL_BLOCK)], axis=0)

    def add_shared(s, bias):
        return jnp.concatenate([s[:, h * QT:(h + 1) * QT] + bias for h in range(hpg)], axis=1)

    def add_band(s, bias, g):
        return jnp.concatenate([s[:, h * QT:(h + 1) * QT] + (band_ref[g * hpg + h] + bias) for h in range(hpg)],
                               axis=1)

    def weighted_values(vt, g, tile0, p):
        t0 = jnp.maximum(tile0, -1) + 1
        pb = p.astype(BF16)
        pv = _dot(vt[g, t0], pb[:QT])
        for t in range(1, p.shape[0] // QT):
            pv = pv + _dot(vt[g, t0 + t], pb[t * QT:(t + 1) * QT])
        return pv

    to_exp2 = scale * math.log2(math.e)

    def flash_first(s, vt, g, tile0, acc_ref):
        m = jnp.max(s, axis=0, keepdims=True)
        p = jnp.exp2((s - m) * to_exp2)
        acc_ref[g] = weighted_values(vt, g, tile0, p)
        return m, jnp.sum(p, axis=0, keepdims=True)

    def flash_next(s, vt, g, tile0, acc_ref, m, l):
        m_new = jnp.maximum(m, jnp.max(s, axis=0, keepdims=True))
        alpha = jnp.exp2((m - m_new) * to_exp2)
        p = jnp.exp2((s - m_new) * to_exp2)
        acc_ref[g] = acc_ref[g] * alpha + weighted_values(vt, g, tile0, p)
        return m_new, alpha * l + jnp.sum(p, axis=0, keepdims=True)


    edge = jnp.where(sub > lane, 0.0, NEG)
    stats_w = [None] * grps
    for g in range(grps):
        s = add_band(raw_w[g][0], row_bias(posb_ref, near, KCH), g)
        stats_w[g] = flash_first(s, vt_w, g, near, acc_w)
    for g in range(grps):
        s = add_shared(raw_w[g][1], row_bias(posb_ref, mid_tile, KCH))
        stats_w[g] = flash_next(s, vt_w, g, mid_tile, acc_w, *stats_w[g])
    for g in range(grps):
        s = add_shared(raw_w[g][2], row_bias(posb_ref, far_tile, QT) + edge)
        stats_w[g] = flash_next(s, vt_w, g, far_tile, acc_w, *stats_w[g])

    stats_s = []
    for g in range(grps):
        s = add_band(raw_s[g], row_bias(selb_ref.at[g], near, KCH), g)
        stats_s.extend(flash_first(s, vt_s, g, near, acc_s))

    for g in range(grps):
        acc_s2[g] = jnp.zeros(acc_s2.shape[1:], F32)
        stats_s.extend((jnp.full((1, hpg * QT), NEG, F32), jnp.zeros((1, hpg * QT), F32)))
    n_far = (qi + 2) // 2 - 1

    def sel_body(it, carry):
        tiles = (qi + 1 - 2 * (2 * it + 2), qi + 1 - 2 * (2 * it + 3))
        raw = [[scores(kpad_s, g, t, KCH) for g in range(grps)] for t in tiles]
        out = list(carry)
        for par, (acc_ref, t) in enumerate(zip((acc_s, acc_s2), tiles)):
            for g in range(grps):
                k = 2 * (par * grps + g)
                s = add_shared(raw[par][g], row_bias(selb_ref.at[g], t, KCH))
                out[k], out[k + 1] = flash_next(s, vt_s, g, t, acc_ref, carry[k], carry[k + 1])
        return tuple(out)

    stats_s = lax.fori_loop(0, (n_far + 1) // 2, sel_body, tuple(stats_s))

    for g in range(grps):
        gate = _sigmoid(gate_ref[:, g * LANES:(g + 1) * LANES].astype(F32)).T
        m_a, l_a = stats_s[2 * g], stats_s[2 * g + 1]
        m_b, l_b = stats_s[2 * (grps + g)], stats_s[2 * (grps + g) + 1]
        m_all = jnp.maximum(m_a, m_b)
        w_a = jnp.exp2((m_a - m_all) * to_exp2)
        w_b = jnp.exp2((m_b - m_all) * to_exp2)
        inv_s = 1.0 / jnp.maximum(l_a * w_a + l_b * w_b, 1e-30)
        w_a, w_b = w_a * inv_s, w_b * inv_s
        inv_w = 1.0 / jnp.maximum(stats_w[g][1], 1e-30)
        for h in range(hpg):
            sl = slice(h * QT, (h + 1) * QT)
            o_t = (gate[h:h + 1] * o_cmp[g][:, sl]
                   + gate[hpg + h:hpg + h + 1] * (acc_s[g, :, sl] * w_a[:, sl] + acc_s2[g, :, sl] * w_b[:, sl])
                   + gate[2 * hpg + h:2 * hpg + h + 1] * (acc_w[g, :, sl] * inv_w[:, sl]))
            c0 = (g * hpg + h) * HEAD_DIM
            o_ref[:, c0:c0 + HEAD_DIM] = o_t.T.astype(o_ref.dtype)


def _nsa_attention(proj, cmp_kv, cmp_kv_t, gates, cmp_bias, band, *, batch, seq, kv_chunk0, gate_blk0):
    nq = seq // QT
    grps = NSA_GROUPS
    n_sel = seq // SEL_BLOCK
    rows_c = seq // CMP_STRIDE
    assert WINDOW // QT <= PAD_BLOCKS // (QT // SEL_BLOCK) and WINDOW == 4 * QT and KCH == 2 * QT
    assert kv_chunk0 % grps == 0 and gate_blk0 % grps == 0

    def kv_spec(branch, kv):
        blk = (kv_chunk0 + (branch * 2 + kv) * grps) // grps
        return pl.BlockSpec((grps, seq, HEAD_DIM), lambda b, i: (blk, b, 0))

    return pl.pallas_call(
        _nsa_kernel,
        out_shape=jax.ShapeDtypeStruct((batch * seq, NSA_HEADS * HEAD_DIM), BF16),
        grid=(batch, nq),
        in_specs=[
            pl.BlockSpec((NSA_HEADS, QT, HEAD_DIM), lambda b, i: (0, b * nq + i, 0)),
            pl.BlockSpec((1, grps, rows_c, HEAD_DIM), lambda b, i: (b, 0, 0, 0)),
            pl.BlockSpec((1, grps, HEAD_DIM, rows_c), lambda b, i: (b, 1, 0, 0)),
            kv_spec(1, 0), kv_spec(1, 1), kv_spec(2, 0), kv_spec(2, 1),
            pl.BlockSpec((QT, grps * LANES), lambda b, i: (b * nq + i, gate_blk0 // grps)),
            pl.BlockSpec((NSA_HEADS, LANES, QT), lambda b, i: (0, 0, i)),
            pl.BlockSpec((NSA_HEADS, KCH, QT), lambda b, i: (0, 0, 0)),
        ],
        out_specs=pl.BlockSpec((QT, NSA_HEADS * HEAD_DIM), lambda b, i: (b * nq + i, 0)),
        scratch_shapes=[
            pltpu.VMEM((grps, seq + QT, HEAD_DIM), BF16),
            pltpu.VMEM((grps, nq + 1, HEAD_DIM, QT), BF16),
            pltpu.VMEM((grps, seq + QT, HEAD_DIM), BF16),
            pltpu.VMEM((grps, nq + 1, HEAD_DIM, QT), BF16),
            pltpu.VMEM((grps, HEAD_DIM, NSA_HPG * QT), F32),
            pltpu.VMEM((grps, HEAD_DIM, NSA_HPG * QT), F32),
            pltpu.VMEM((grps, HEAD_DIM, NSA_HPG * QT), F32),
            pltpu.VMEM((grps, PAD_BLOCKS + n_sel, QT), F32),
            pltpu.VMEM((PAD_BLOCKS + n_sel, QT), F32),
        ],
        compiler_params=_cparams(("arbitrary", "arbitrary")),
        name="nsa_attention",
    )(proj, cmp_kv, cmp_kv_t, proj, proj, proj, proj, gates, cmp_bias, band)


def _retention_kernel(q_ref, k_ref, v_ref, g_ref, cos_ref, sin_ref, dmat_ref, xi_ref, zeta_ref,
                      o_ref, state_ref, *, g_chunk):
    @pl.when(pl.program_id(1) == 0)
    def _():
        state_ref[...] = jnp.zeros(state_ref.shape, F32)

    cos, sin = cos_ref[...], sin_ref[...]
    half = RET_QK_DIM // 2
    heads = range(RET_HEADS)
    qb, kr, v, inner, cross = [], [], [], [], []
    for h in heads:
        q = q_ref[h].astype(F32)
        k = k_ref[h].astype(F32)
        qb.append((q * cos + pltpu.roll(q, half, 1) * sin).astype(BF16))
        kr.append((k * cos + pltpu.roll(k, half, 1) * sin) * (RET_QK_DIM ** -0.5))
        v.append(jnp.concatenate([v_ref[2 * h], v_ref[2 * h + 1]], axis=1))
    for h in heads:
        inner.append(_dot_nt(qb[h], kr[h].astype(BF16)))
        cross.append(_dot(qb[h], state_ref[h].astype(BF16)))
    for h in heads:
        kz = (kr[h] * zeta_ref[h]).T.astype(BF16)
        state_ref[h] = state_ref[h] * g_chunk[h] + _dot(kz, v[h])
    for h in heads:
        o = _dot((inner[h] * dmat_ref[h]).astype(BF16), v[h]) + cross[h] * xi_ref[h][:, 0:1]
        mu = jnp.mean(o, axis=-1, keepdims=True)
        var = jnp.mean(jnp.square(o - mu), axis=-1, keepdims=True)
        on = (o - mu) * lax.rsqrt(var + EPS)
        gate = g_ref[:, h * RET_V_DIM:(h + 1) * RET_V_DIM].astype(F32)
        gate = gate * (1.0 / (1.0 + jnp.exp(-gate)))
        o_ref[:, h * RET_V_DIM:(h + 1) * RET_V_DIM] = (on * gate).astype(o_ref.dtype)


def _retention(proj, gates, *, batch, seq, q_blk, k_blk, v_blk):
    c = RET_CHUNK
    n = seq // c
    heads = RET_HEADS
    dk = RET_QK_DIM
    theta = 1.0 / (10000.0 ** np.linspace(0.0, 1.0, dk // 2))
    ang = np.arange(seq)[:, None] * theta[None, :]
    cos = np.concatenate([np.cos(ang), np.cos(ang)], axis=1)
    sin = np.concatenate([-np.sin(ang), np.sin(ang)], axis=1)
    log_g = np.log(1.0 - np.exp2(-5.0 - np.arange(heads)))
    j = np.arange(c)
    diff = j[:, None] - j[None, :]
    dmat = np.where(diff >= 0, np.exp(np.maximum(diff, 0)[None] * log_g[:, None, None]), 0.0)
    xi = np.broadcast_to(np.exp((j + 1.0)[None, :] * log_g[:, None])[:, :, None], (heads, c, LANES))
    zeta = np.broadcast_to(np.exp((c - 1.0 - j)[None, :] * log_g[:, None])[:, :, None], (heads, c, LANES))
    g_chunk = tuple(float(v) for v in np.exp(c * log_g))
    consts = [jnp.asarray(a, F32) for a in (cos, sin, dmat, xi, zeta)]

    full3 = lambda shape: pl.BlockSpec(shape, lambda b, i: (0, 0, 0))
    return pl.pallas_call(
        functools.partial(_retention_kernel, g_chunk=g_chunk),
        out_shape=jax.ShapeDtypeStruct((batch * seq, heads * RET_V_DIM), BF16),
        grid=(batch, n),
        in_specs=[
            pl.BlockSpec((heads, c, LANES), lambda b, i: (q_blk, b * n + i, 0)),
            pl.BlockSpec((heads, c, LANES), lambda b, i: (k_blk, b * n + i, 0)),
            pl.BlockSpec((2 * heads, c, LANES), lambda b, i: (v_blk, b * n + i, 0)),
            pl.BlockSpec((c, heads * RET_V_DIM), lambda b, i: (b * n + i, 0)),
            pl.BlockSpec((c, dk), lambda b, i: (i, 0)),
            pl.BlockSpec((c, dk), lambda b, i: (i, 0)),
            full3((heads, c, c)), full3((heads, c, LANES)), full3((heads, c, LANES)),
        ],
        out_specs=pl.BlockSpec((c, heads * RET_V_DIM), lambda b, i: (b * n + i, 0)),
        scratch_shapes=[pltpu.VMEM((heads, dk, RET_V_DIM), F32)],
        compiler_params=_cparams(("parallel", "arbitrary")),
        name="retention",
    )(proj, proj, proj, gates, *consts)


def _mem_attn_kernel(q_ref, k_ref, v_ref, o_ref):
    scale = MEM_HEAD_DIM ** -0.5
    scores = [(_dot_nt(q_ref[2 * h], k_ref[2 * h]) + _dot_nt(q_ref[2 * h + 1], k_ref[2 * h + 1])) * scale
              for h in range(MEM_HEADS)]
    for h in range(MEM_HEADS):
        s = scores[h]
        mx = jnp.max(s, axis=-1, keepdims=True)
        p = jnp.exp(s - mx)
        p = (p / jnp.sum(p, axis=-1, keepdims=True)).astype(BF16)
        for half in range(2):
            c0 = h * MEM_HEAD_DIM + half * LANES
            o_ref[:, c0:c0 + LANES] = _dot(p, v_ref[2 * h + half]).astype(o_ref.dtype)


def _mem_attention(proj, mkv, *, batch, seq, mem_len, q_blk, tq):
    nq = seq // tq
    chunks = MEM_HEADS * MEM_HEAD_DIM // LANES
    return pl.pallas_call(
        _mem_attn_kernel,
        out_shape=jax.ShapeDtypeStruct((batch * seq, MEM_HEADS * MEM_HEAD_DIM), BF16),
        grid=(batch, nq),
        in_specs=[
            pl.BlockSpec((chunks, tq, LANES), lambda b, i: (q_blk, b * nq + i, 0)),
            pl.BlockSpec((chunks, mem_len, LANES), lambda b, i: (0, b, 0)),
            pl.BlockSpec((chunks, mem_len, LANES), lambda b, i: (1, b, 0)),
        ],
        out_specs=pl.BlockSpec((tq, MEM_HEADS * MEM_HEAD_DIM), lambda b, i: (b * nq + i, 0)),
        compiler_params=_cparams(("parallel", "parallel")),
        name="mem_attention",
    )(proj, mkv, mkv)


def _sigmoid(x):
    return 1.0 / (1.0 + jnp.exp(-x))


def _merge_kernel(ya_ref, yb_ref, yc_ref, wa_ref, wb_ref, wc_ref, ga_ref, gb_ref, gc_ref, o_ref):
    out = _sigmoid(ga_ref[...].astype(F32)) * _dot(ya_ref[...], wa_ref[...])
    out = out + _sigmoid(gb_ref[...].astype(F32)) * _dot(yb_ref[...], wb_ref[...])
    out = out + _sigmoid(gc_ref[...].astype(F32)) * _dot(yc_ref[...], wc_ref[...])
    o_ref[...] = out.astype(o_ref.dtype)


def _merge(ya, yb, yc, wa, wb, wc, gates, *, d_model, gate_col0, tm, tn):
    m, kdim = ya.shape
    y_spec = pl.BlockSpec((tm, kdim), lambda i, j: (i, 0))
    w_spec = pl.BlockSpec((kdim, tn), lambda i, j: (0, j))

    def g_spec(branch):
        base = (gate_col0 + branch * d_model) // tn
        return pl.BlockSpec((tm, tn), lambda i, j: (i, base + j))

    return pl.pallas_call(
        _merge_kernel,
        out_shape=jax.ShapeDtypeStruct((m, d_model), BF16),
        grid=(m // tm, d_model // tn),
        in_specs=[y_spec, y_spec, y_spec, w_spec, w_spec, w_spec, g_spec(0), g_spec(1), g_spec(2)],
        out_specs=pl.BlockSpec((tm, tn), lambda i, j: (i, j)),
        compiler_params=_cparams(("parallel", "parallel")),
        name="branch_merge",
    )(ya, yb, yc, wa, wb, wc, gates, gates, gates)


def _out_proj_kernel(y_ref, w_ref, g_ref, x_ref, o_ref):
    o_ref[...] = x_ref[...] + _rms(_dot(y_ref[...], w_ref[...]), g_ref[...])


def _out_proj(y, w, g, x2d, *, tm):
    m, d = x2d.shape
    return pl.pallas_call(
        _out_proj_kernel,
        out_shape=jax.ShapeDtypeStruct((m, d), F32),
        grid=(m // tm,),
        in_specs=[
            pl.BlockSpec((tm, d), lambda i: (i, 0)),
            pl.BlockSpec((d, d), lambda i: (0, 0)),
            pl.BlockSpec((1, d), lambda i: (0, 0)),
            pl.BlockSpec((tm, d), lambda i: (i, 0)),
        ],
        out_specs=pl.BlockSpec((tm, d), lambda i: (i, 0)),
        compiler_params=_cparams(("parallel",)),
        name="out_proj",
    )(y, w, g, x2d)


def _gelu_tanh(x):
    return 0.5 * x * (1.0 + jnp.tanh(math.sqrt(2.0 / math.pi) * (x + 0.044715 * (x * x * x))))


def _ffn_kernel(x_ref, halo_ref, gpre_ref, wg_ref, wv_ref, cwg_ref, cwv_ref, cbg_ref, cbv_ref,
                wd_ref, gpost_ref, o_ref, h_ref, acc_ref, *, seq):
    i, j = pl.program_id(0), pl.program_id(1)
    tm = x_ref.shape[0]

    @pl.when(j == 0)
    def _():
        h_ref[HALO:, :] = _rms(x_ref[...], gpre_ref[...]).astype(BF16)
        keep = jnp.where((i * tm) % seq == 0, 0.0, 1.0)
        h_ref[:HALO, :] = (_rms(halo_ref[...], gpre_ref[...]) * keep).astype(BF16)
        acc_ref[...] = jnp.zeros(acc_ref.shape, F32)

    h = h_ref[...]

    def conv(u, w_ref, b_ref):
        w = w_ref[...]
        y = b_ref[...] + w[2:3] * u[HALO:]
        y = y + w[1:2] * pltpu.roll(u, 1, 0)[HALO:]
        return y + w[0:1] * pltpu.roll(u, 2, 0)[HALO:]

    yg = conv(_dot(h, wg_ref[...]), cwg_ref, cbg_ref)
    yv = conv(_dot(h, wv_ref[...]), cwv_ref, cbv_ref)
    act = (_gelu_tanh(yg) * yv).astype(BF16)
    acc_ref[...] += _dot(act, wd_ref[...])

    @pl.when(j == pl.num_programs(1) - 1)
    def _():
        o_ref[...] = x_ref[...] + _rms(acc_ref[...], gpost_ref[...])


def _ffn(x2d, gpre, w_up, conv_w, conv_b, w_down, gpost, *, seq, tm, tn):
    m, d = x2d.shape
    d_ff = w_down.shape[0]
    nj = d_ff // tn
    halo_blocks = tm // HALO
    return pl.pallas_call(
        functools.partial(_ffn_kernel, seq=seq),
        out_shape=jax.ShapeDtypeStruct((m, d), F32),
        grid=(m // tm, nj),
        in_specs=[
            pl.BlockSpec((tm, d), lambda i, j: (i, 0)),
            pl.BlockSpec((HALO, d), lambda i, j: (jnp.maximum(i * halo_blocks - 1, 0), 0)),
            pl.BlockSpec((1, d), lambda i, j: (0, 0)),
            pl.BlockSpec((d, tn), lambda i, j: (0, j)),
            pl.BlockSpec((d, tn), lambda i, j: (0, nj + j)),
            pl.BlockSpec((CONV_WIDTH, tn), lambda i, j: (0, j)),
            pl.BlockSpec((CONV_WIDTH, tn), lambda i, j: (0, nj + j)),
            pl.BlockSpec((1, tn), lambda i, j: (0, j)),
            pl.BlockSpec((1, tn), lambda i, j: (0, nj + j)),
            pl.BlockSpec((tn, d), lambda i, j: (j, 0)),
            pl.BlockSpec((1, d), lambda i, j: (0, 0)),
        ],
        out_specs=pl.BlockSpec((tm, d), lambda i, j: (i, 0)),
        scratch_shapes=[pltpu.VMEM((tm + HALO, d), BF16), pltpu.VMEM((tm, d), F32)],
        compiler_params=_cparams(("parallel", "arbitrary")),
        name="conv_ffn",
    )(x2d, x2d, gpre, w_up, w_up, conv_w, conv_w, conv_b, conv_b, w_down, gpost)


def _pick(n, prefs):
    for t in prefs:
        if n % t == 0:
            return t
    raise ValueError(f"no tile in {prefs} divides {n}")


class _Tiles(NamedTuple):
    proj_rows: int
    proj_cols: int
    heads_rows: int
    heads_cols: int
    gate_cols: int
    merge_cols: int
    out_rows: int
    ffn_rows: int
    ffn_cols: int
    mem_q_rows: int


def _tiles(m):
    return _Tiles(proj_rows=_pick(m, (1024, 512, 256, 128)), proj_cols=2 * MXU_WIDTH,
                  heads_rows=_pick(m, (512, 256, 128)), heads_cols=11 * MXU_WIDTH, gate_cols=3 * MXU_WIDTH,
                  merge_cols=4 * MXU_WIDTH,
                  out_rows=_pick(m, (512, 256, 128)), ffn_rows=_pick(m, (512, 256, 128)), ffn_cols=2 * MXU_WIDTH,
                  mem_q_rows=512)


def kernel(x, mem, w_in, cmp_pe_k, cmp_w1_k, cmp_w2_k, cmp_pe_v, cmp_w1_v, cmp_w2_v, rel_bias, w_mem_kv,
           w_br_nsa, w_br_ret, w_br_mem, w_o, w_up, conv_w, conv_b, w_down, g_pre_mix, g_post_mix, g_mem,
           g_pre_ffn, g_post_ffn):
    batch, seq, d_model = x.shape
    mem_len = mem.shape[1]
    depth = w_in.shape[0]
    m = batch * seq
    grps, hpg, dh = NSA_GROUPS, NSA_HPG, HEAD_DIM

    nsa_q = NSA_HEADS * dh
    nsa_kv = 3 * 2 * grps * dh
    nsa_gates = 3 * NSA_HEADS
    ret_qk = RET_HEADS * RET_QK_DIM
    ret_v = RET_HEADS * RET_V_DIM
    mem_q = MEM_HEADS * MEM_HEAD_DIM
    splits = (nsa_q, nsa_kv, nsa_gates, ret_qk, ret_qk, ret_v, ret_v, mem_q, 3 * d_model)
    offs = np.concatenate([[0], np.cumsum(splits)])
    o_nq, o_nkv, o_ng, o_rq, o_rk, o_rv, o_rg, o_mq, o_bg = (int(v) for v in offs[:-1])

    q_chunk0 = 0
    rv_chunk0 = q_chunk0 + nsa_q // LANES
    mq_chunk0 = rv_chunk0 + ret_v // LANES
    rq_chunk0 = mq_chunk0 + mem_q // LANES
    rk_chunk0 = rq_chunk0 + ret_qk // LANES
    kv_chunk0 = rk_chunk0 + ret_qk // LANES
    bg_col0 = ret_v
    ng_col0 = bg_col0 + 3 * d_model
    tiles = _tiles(m)
    gate_cols = -(-(ng_col0 + grps * LANES) // tiles.gate_cols) * tiles.gate_cols

    tbl_flat = rel_bias.reshape(-1)
    cmp_bias, band = _bias_tables(tbl_flat, seq)
    x2d = x.reshape(m, d_model)

    for l in range(depth):
        w = w_in[l]
        cols = lambda o, n: w[:, o:o + n]
        w_attn = jnp.concatenate([cols(o_nq, nsa_q), cols(o_rv, ret_v), cols(o_mq, mem_q), cols(o_rq, ret_qk),
                                  cols(o_rk, ret_qk), cols(o_nkv, nsa_kv)], axis=1).astype(BF16)
        ng = cols(o_ng, nsa_gates).reshape(d_model, 3, grps, hpg).transpose(0, 2, 1, 3).reshape(d_model, grps, 3 * hpg)
        ng = jnp.pad(ng, ((0, 0), (0, 0), (0, LANES - 3 * hpg))).reshape(d_model, grps * LANES)
        w_gate = jnp.concatenate([cols(o_rg, ret_v), cols(o_bg, 3 * d_model), ng], axis=1)
        w_gate = jnp.pad(w_gate, ((0, 0), (0, gate_cols - w_gate.shape[1]))).astype(BF16)

        g_pre = g_pre_mix[l].reshape(1, d_model)
        proj, h_mix = _norm_proj(x2d, g_pre, w_attn, tm=tiles.heads_rows, tn=tiles.heads_cols,
                                 name="in_proj_heads")
        gates = _proj(h_mix, w_gate, tm=tiles.proj_rows, tn=tiles.gate_cols, name="in_proj_gates")

        pe = jnp.stack([cmp_pe_k[l], cmp_pe_v[l]]).reshape(2, 2, CMP_STRIDE * dh)
        w1 = jnp.stack([cmp_w1_k[l], cmp_w1_v[l]]).astype(BF16)
        w2 = jnp.stack([cmp_w2_k[l], cmp_w2_v[l]]).astype(BF16)
        cmp_kv, cmp_kv_t = _compress(proj, kv_chunk0, pe, w1, w2, batch=batch, seq=seq)
        y_a = _nsa_attention(proj, cmp_kv, cmp_kv_t, gates, cmp_bias, band, batch=batch, seq=seq,
                             kv_chunk0=kv_chunk0, gate_blk0=ng_col0 // LANES)

        y_b = _retention(proj, gates, batch=batch, seq=seq, q_blk=rq_chunk0 // RET_HEADS,
                         k_blk=rk_chunk0 // RET_HEADS, v_blk=rv_chunk0 // (2 * RET_HEADS))

        mem2d = mem.reshape(batch * mem_len, d_model)
        mkv, _ = _norm_proj(mem2d, g_mem[l].reshape(1, d_model), w_mem_kv[l].astype(BF16),
                            tm=_tiles(batch * mem_len).proj_rows, tn=tiles.proj_cols, name="mem_kv_proj")
        y_c = _mem_attention(proj, mkv, batch=batch, seq=seq, mem_len=mem_len,
                             q_blk=mq_chunk0 // (mem_q // LANES), tq=tiles.mem_q_rows)

        merged = _merge(y_a, y_b, y_c, w_br_nsa[l].astype(BF16), w_br_ret[l].astype(BF16),
                        w_br_mem[l].astype(BF16), gates, d_model=d_model, gate_col0=bg_col0,
                        tm=tiles.proj_rows, tn=tiles.merge_cols)
        x2d = _out_proj(merged, w_o[l].astype(BF16), g_post_mix[l].reshape(1, d_model), x2d, tm=tiles.out_rows)

        x2d = _ffn(x2d, g_pre_ffn[l].reshape(1, d_model), w_up[l].astype(BF16), conv_w[l],
                   conv_b[l].reshape(1, -1), w_down[l].astype(BF16), g_post_ffn[l].reshape(1, d_model),
                   seq=seq, tm=tiles.ffn_rows, tn=tiles.ffn_cols)
    return x2d.reshape(batch, seq, d_model)
```

```python
import functools
import math
from typing import NamedTuple

import jax
import jax.numpy as jnp
import numpy as np
from jax import lax
from jax.experimental import pallas as pl
from jax.experimental.pallas import tpu as pltpu

F32 = jnp.float32
BF16 = jnp.bfloat16

LANES = 128
MXU_WIDTH = 256
EPS = 1e-6
NEG = -1e30

NSA_HEADS = 8
NSA_GROUPS = 2
NSA_HPG = NSA_HEADS // NSA_GROUPS
HEAD_DIM = 128
CMP_BLOCK = 32
CMP_STRIDE = 16
CMP_HIDDEN = 256
SEL_BLOCK = 64
SEL_TOPK = 16
SEL_LOCAL = 2
FORCE_SCORE = 1e4
WINDOW = 512
RET_HEADS = 4
RET_QK_DIM = 128
RET_V_DIM = 256
RET_CHUNK = 128
MEM_HEADS = 4
MEM_HEAD_DIM = 256
REL_BUCKETS = 32
REL_MAX_DIST = 128
CONV_WIDTH = 3

QT = 128
KCH = 2 * QT
PAD_BLOCKS = 8
HALO = 16

VMEM_LIMIT = 56 * 1024 * 1024


def _cparams(sem, vmem=VMEM_LIMIT):
    return pltpu.CompilerParams(dimension_semantics=sem, vmem_limit_bytes=vmem)


def _dot(a, b):
    return jnp.dot(a, b, preferred_element_type=F32)


def _dot_nt(a, b):
    return lax.dot_general(a, b, (((1,), (1,)), ((), ())), preferred_element_type=F32)


def _rms(x, g):
    ms = jnp.mean(x * x, axis=-1, keepdims=True)
    return x * lax.rsqrt(ms + EPS) * g


def _norm_proj_kernel(x_ref, g_ref, w_ref, o_ref, h_ref):
    @pl.when(pl.program_id(1) == 0)
    def _():
        h_ref[...] = _rms(x_ref[...], g_ref[...]).astype(BF16)

    acc = _dot(h_ref[...], w_ref[...])
    for c in range(acc.shape[1] // LANES):
        o_ref[c] = acc[:, c * LANES:(c + 1) * LANES].astype(o_ref.dtype)


def _norm_proj(x2d, g, w, *, tm, tn, name):
    m, d = x2d.shape
    n = w.shape[1]
    return pl.pallas_call(
        _norm_proj_kernel,
        out_shape=(jax.ShapeDtypeStruct((n // LANES, m, LANES), BF16), jax.ShapeDtypeStruct((m, d), BF16)),
        grid=(m // tm, n // tn),
        in_specs=[
            pl.BlockSpec((tm, d), lambda i, j: (i, 0)),
            pl.BlockSpec((1, d), lambda i, j: (0, 0)),
            pl.BlockSpec((d, tn), lambda i, j: (0, j)),
        ],
        out_specs=(pl.BlockSpec((tn // LANES, tm, LANES), lambda i, j: (j, i, 0)),
                   pl.BlockSpec((tm, d), lambda i, j: (i, 0))),
        compiler_params=_cparams(("parallel", "arbitrary")),
        name=name,
    )(x2d, g, w)


def _proj_kernel(h_ref, w_ref, o_ref):
    o_ref[...] = _dot(h_ref[...], w_ref[...]).astype(o_ref.dtype)


def _proj(h, w, *, tm, tn, name):
    m, d = h.shape
    n = w.shape[1]
    return pl.pallas_call(
        _proj_kernel,
        out_shape=jax.ShapeDtypeStruct((m, n), BF16),
        grid=(m // tm, n // tn),
        in_specs=[pl.BlockSpec((tm, d), lambda i, j: (i, 0)), pl.BlockSpec((d, tn), lambda i, j: (0, j))],
        out_specs=pl.BlockSpec((tm, tn), lambda i, j: (i, j)),
        compiler_params=_cparams(("parallel", "parallel")),
        name=name,
    )(h, w)


def _compress_kernel(x_ref, pe_ref, w1_ref, w2_ref, o_ref, ot_ref, xf_ref):
    half = CMP_STRIDE * HEAD_DIM
    xf_ref[...] = x_ref[0].astype(F32)
    rows = xf_ref.shape[0] // CMP_STRIDE
    x = jnp.concatenate([xf_ref[pl.ds(r, rows, stride=CMP_STRIDE), :] for r in range(CMP_STRIDE)], axis=1)
    pe = pe_ref[0]
    a = _dot((x + pe[0:1]).astype(BF16), w1_ref[0, :half, :])
    b = _dot((x + pe[1:2]).astype(BF16), w1_ref[0, half:, :])
    hid = a + pltpu.roll(b, rows - 1, 0)
    hid = hid * (1.0 / (1.0 + jnp.exp(-hid)))
    out = _dot(hid.astype(BF16), w2_ref[0])
    o_ref[0, 0] = out.astype(o_ref.dtype)
    ot_ref[0, 0] = out.T.astype(ot_ref.dtype)


def _compress(proj, kv_chunk0, pe, w1, w2, *, batch, seq):
    rows = seq // CMP_STRIDE
    width = CMP_STRIDE * HEAD_DIM
    n_out = 2 * NSA_GROUPS
    return pl.pallas_call(
        _compress_kernel,
        out_shape=(jax.ShapeDtypeStruct((batch, n_out, rows, HEAD_DIM), BF16),
                   jax.ShapeDtypeStruct((batch, n_out, HEAD_DIM, rows), BF16)),
        grid=(batch, n_out),
        in_specs=[
            pl.BlockSpec((1, seq, HEAD_DIM), lambda b, c: (kv_chunk0 + c, b, 0)),
            pl.BlockSpec((1, 2, width), lambda b, c: (c // NSA_GROUPS, 0, 0)),
            pl.BlockSpec((1, 2 * width, CMP_HIDDEN), lambda b, c: (c // NSA_GROUPS, 0, 0)),
            pl.BlockSpec((1, CMP_HIDDEN, HEAD_DIM), lambda b, c: (c // NSA_GROUPS, 0, 0)),
        ],
        out_specs=(pl.BlockSpec((1, 1, rows, HEAD_DIM), lambda b, c: (b, c, 0, 0)),
                   pl.BlockSpec((1, 1, HEAD_DIM, rows), lambda b, c: (b, c, 0, 0))),
        scratch_shapes=[pltpu.VMEM((seq, HEAD_DIM), F32)],
        compiler_params=_cparams(("parallel", "parallel")),
        name="nsa_compress",
    )(proj, pe, w1, w2)


def _bias_from_rel(rel, tbl_ref, head):
    max_exact = REL_BUCKETS // 2
    n = jnp.maximum(rel, 0)
    nf = jnp.maximum(n, 1).astype(F32)
    large = max_exact + (jnp.log(nf / max_exact) / math.log(REL_MAX_DIST / max_exact)
                         * (REL_BUCKETS - max_exact)).astype(jnp.int32)
    large = jnp.minimum(large, REL_BUCKETS - 1)
    bucket = jnp.where(n < max_exact, n, large)
    out = jnp.zeros(rel.shape, F32)
    for b in range(REL_BUCKETS):
        out = jnp.where(bucket == b, tbl_ref[b * NSA_HEADS + head], out)
    return out


def _cmp_bias_kernel(tbl_ref, o_ref):
    head, blk = pl.program_id(0), pl.program_id(1)
    rows, cols = o_ref.shape[1], o_ref.shape[2]
    n = lax.broadcasted_iota(jnp.int32, (rows, cols), 0)
    t = blk * cols + lax.broadcasted_iota(jnp.int32, (rows, cols), 1)
    o_ref[0] = _bias_from_rel(t - (n * CMP_STRIDE + CMP_BLOCK - 1), tbl_ref, head)


def _band_bias_kernel(tbl_ref, o_ref):
    head = pl.program_id(0)
    rows, cols = o_ref.shape[1], o_ref.shape[2]
    c = lax.broadcasted_iota(jnp.int32, (rows, cols), 0)
    i = lax.broadcasted_iota(jnp.int32, (rows, cols), 1)
    rel = i - c + QT
    far = tbl_ref[(REL_BUCKETS - 1) * NSA_HEADS + head]
    o_ref[0] = jnp.where(rel >= 0, (_bias_from_rel(rel, tbl_ref, head) - far) * (HEAD_DIM ** 0.5), NEG)


def _bias_tables(tbl_flat, seq):
    cols = 256
    smem = pl.BlockSpec(memory_space=pltpu.SMEM)
    cmp_bias = pl.pallas_call(
        _cmp_bias_kernel,
        out_shape=jax.ShapeDtypeStruct((NSA_HEADS, LANES, seq), F32),
        grid=(NSA_HEADS, seq // cols),
        in_specs=[smem],
        out_specs=pl.BlockSpec((1, LANES, cols), lambda h, r: (h, 0, r)),
        compiler_params=_cparams(("parallel", "parallel")),
        name="cmp_bias",
    )(tbl_flat)
    band = pl.pallas_call(
        _band_bias_kernel,
        out_shape=jax.ShapeDtypeStruct((NSA_HEADS, 2 * QT, QT), F32),
        grid=(NSA_HEADS,),
        in_specs=[smem],
        out_specs=pl.BlockSpec((1, 2 * QT, QT), lambda h: (h, 0, 0)),
        compiler_params=_cparams(("parallel",)),
        name="band_bias",
    )(tbl_flat)
    return cmp_bias, band


def _split3(x):
    hi = x.astype(BF16)
    r1 = x - hi.astype(F32)
    mid = r1.astype(BF16)
    lo = (r1 - mid.astype(F32)).astype(BF16)
    return hi, mid, lo


def _nsa_kernel(q_ref, kc_ref, vct_ref, ks_ref, vs_ref, kw_ref, vw_ref, gate_ref, cbias_ref, band_ref,
                o_ref, kpad_s, vt_s, kpad_w, vt_w, acc_s, acc_s2, acc_w, selb_ref, posb_ref):
    qi = pl.program_id(1)
    hpg, grps = NSA_HPG, NSA_GROUPS
    scale = HEAD_DIM ** -0.5
    n_tiles = vt_s.shape[1] - 1
    seq = n_tiles * QT
    n_sel = seq // SEL_BLOCK
    n_cmp = (seq - CMP_BLOCK) // CMP_STRIDE + 1
    blocks_per_tile = QT // SEL_BLOCK

    @pl.when(qi == 0)
    def _():
        for kpad, vt, k_ref, v_ref in ((kpad_s, vt_s, ks_ref, vs_ref), (kpad_w, vt_w, kw_ref, vw_ref)):
            for g in range(grps):
                kpad[g, :QT, :] = jnp.zeros((QT, HEAD_DIM), BF16)
                kpad[g, QT:, :] = k_ref[g]
                vt[g, 0] = jnp.zeros((HEAD_DIM, QT), BF16)
                for t in range(n_tiles):
                    vt[g, t + 1] = v_ref[g, t * QT:(t + 1) * QT, :].astype(F32).T.astype(BF16)
        neg_rows = jnp.full((PAD_BLOCKS, QT), NEG, F32)
        for g in range(grps):
            selb_ref[g, :PAD_BLOCKS, :] = neg_rows
        posb_ref[:PAD_BLOCKS, :] = neg_rows
        posb_ref[PAD_BLOCKS:, :] = jnp.zeros((n_sel, QT), F32)

    qs = [q_ref[g * hpg:(g + 1) * hpg].reshape(hpg * QT, HEAD_DIM) for g in range(grps)]
    sub = lax.broadcasted_iota(jnp.int32, (QT, QT), 0)
    lane = lax.broadcasted_iota(jnp.int32, (QT, QT), 1)
    valid_c = (qi * QT + lane) - (sub * CMP_STRIDE + CMP_BLOCK - 1) >= 0
    jb = lax.broadcasted_iota(jnp.int32, (n_sel, QT), 0)
    nn = lax.broadcasted_iota(jnp.int32, (n_sel, QT), 1)
    overlap_t = jnp.where((nn * CMP_STRIDE < jb * SEL_BLOCK + SEL_BLOCK)
                          & (nn * CMP_STRIDE + CMP_BLOCK > jb * SEL_BLOCK)
                          & (nn < n_cmp), 1.0, 0.0).astype(BF16)
    cur = (qi * QT + nn) // SEL_BLOCK
    forced = (jb == 0) | ((cur - jb >= 0) & (cur - jb < SEL_LOCAL))

    def scores(kpad, g, tile0, n_keys):
        start = pl.multiple_of((jnp.maximum(tile0, -1) + 1) * QT, QT)
        return _dot_nt(kpad[g, pl.ds(start, n_keys), :], qs[g])

    near = qi - 1
    mid_tile = qi - 3
    far_tile = qi - WINDOW // QT
    raw_w = [(scores(kpad_w, g, near, KCH), scores(kpad_w, g, mid_tile, KCH), scores(kpad_w, g, far_tile, QT))
             for g in range(grps)]
    raw_s = [scores(kpad_s, g, near, KCH) for g in range(grps)]

    o_cmp = []
    for g in range(grps):
        sc = _dot_nt(kc_ref[0, g], qs[g]) * scale
        p_heads = []
        p_sum = None
        for h in range(hpg):
            s = jnp.where(valid_c, sc[:, h * QT:(h + 1) * QT] + cbias_ref[g * hpg + h], NEG)
            mx = jnp.max(s, axis=0, keepdims=True)
            p = jnp.where(valid_c, jnp.exp(s - mx), 0.0)
            p = p / jnp.maximum(jnp.sum(p, axis=0, keepdims=True), 1e-30)
            p_heads.append(p.astype(BF16))
            p_sum = p if p_sum is None else p_sum + p
        o_cmp.append(_dot(vct_ref[0, g], jnp.concatenate(p_heads, axis=1)))

        hi, mid, lo = _split3(p_sum)
        imp = _dot(overlap_t, hi) + _dot(overlap_t, mid) + _dot(overlap_t, lo)
        imp = jnp.where(forced, FORCE_SCORE, imp)
        imp = jnp.where(jb <= cur, imp, -1.0)
        rank = jnp.zeros((n_sel, QT), F32)
        for i in range(n_sel):
            other = jnp.broadcast_to(imp[i:i + 1, :], (n_sel, QT))
            rank = rank + jnp.where(jb > i, jnp.where(other >= imp, 1.0, 0.0), jnp.where(other > imp, 1.0, 0.0))
        selb_ref[g, PAD_BLOCKS:, :] = jnp.where(rank < min(SEL_TOPK, n_sel), 0.0, NEG)

    def row_bias(ref, tile0, n_keys):
        r0 = PAD_BLOCKS + tile0 * blocks_per_tile
        return jnp.concatenate([jnp.broadcast_to(ref[pl.ds(r0 + b, 1), :], (SEL_BLOCK, QT))
                                for b in range(n_keys // SEL_BLOCK)], axis=0)

    def add_shared(s, bias):
        return jnp.concatenate([s[:, h * QT:(h + 1) * QT] + bias for h in range(hpg)], axis=1)

    def add_band(s, bias, g):
        return jnp.concatenate([s[:, h * QT:(h + 1) * QT] + (band_ref[g * hpg + h] + bias) for h in range(hpg)],
                               axis=1)

    def weighted_values(vt, g, tile0, p):
        t0 = jnp.maximum(tile0, -1) + 1
        pb = p.astype(BF16)
        pv = _dot(vt[g, t0], pb[:QT])
        for t in range(1, p.shape[0] // QT):
            pv = pv + _dot(vt[g, t0 + t], pb[t * QT:(t + 1) * QT])
        return pv

    to_exp2 = scale * math.log2(math.e)

    def flash_first(s, vt, g, tile0, acc_ref):
        m = jnp.max(s, axis=0, keepdims=True)
        p = jnp.exp2((s - m) * to_exp2)
        acc_ref[g] = weighted_values(vt, g, tile0, p)
        return m, jnp.sum(p, axis=0, keepdims=True)

    def flash_next(s, vt, g, tile0, acc_ref, m, l):
        m_new = jnp.maximum(m, jnp.max(s, axis=0, keepdims=True))
        alpha = jnp.exp2((m - m_new) * to_exp2)
        p = jnp.exp2((s - m_new) * to_exp2)
        acc_ref[g] = acc_ref[g] * alpha + weighted_values(vt, g, tile0, p)
        return m_new, alpha * l + jnp.sum(p, axis=0, keepdims=True)


    edge = jnp.where(sub > lane, 0.0, NEG)
    stats_w = [None] * grps
    for g in range(grps):
        s = add_band(raw_w[g][0], row_bias(posb_ref, near, KCH), g)
        stats_w[g] = flash_first(s, vt_w, g, near, acc_w)
    for g in range(grps):
        s = add_shared(raw_w[g][1], row_bias(posb_ref, mid_tile, KCH))
        stats_w[g] = flash_next(s, vt_w, g, mid_tile, acc_w, *stats_w[g])
    for g in range(grps):
        s = add_shared(raw_w[g][2], row_bias(posb_ref, far_tile, QT) + edge)
        stats_w[g] = flash_next(s, vt_w, g, far_tile, acc_w, *stats_w[g])

    stats_s = []
    for g in range(grps):
        s = add_band(raw_s[g], row_bias(selb_ref.at[g], near, KCH), g)
        stats_s.extend(flash_first(s, vt_s, g, near, acc_s))

    for g in range(grps):
        acc_s2[g] = jnp.zeros(acc_s2.shape[1:], F32)
        stats_s.extend((jnp.full((1, hpg * QT), NEG, F32), jnp.zeros((1, hpg * QT), F32)))
    n_far = (qi + 2) // 2 - 1

    def sel_body(it, carry):
        tiles = (qi + 1 - 2 * (2 * it + 2), qi + 1 - 2 * (2 * it + 3))
        raw = [[scores(kpad_s, g, t, KCH) for g in range(grps)] for t in tiles]
        out = list(carry)
        for par, (acc_ref, t) in enumerate(zip((acc_s, acc_s2), tiles)):
            for g in range(grps):
                k = 2 * (par * grps + g)
                s = add_shared(raw[par][g], row_bias(selb_ref.at[g], t, KCH))
                out[k], out[k + 1] = flash_next(s, vt_s, g, t, acc_ref, carry[k], carry[k + 1])
        return tuple(out)

    stats_s = lax.fori_loop(0, (n_far + 1) // 2, sel_body, tuple(stats_s))

    for g in range(grps):
        gate = _sigmoid(gate_ref[:, g * LANES:(g + 1) * LANES].astype(F32)).T
        m_a, l_a = stats_s[2 * g], stats_s[2 * g + 1]
        m_b, l_b = stats_s[2 * (grps + g)], stats_s[2 * (grps + g) + 1]
        m_all = jnp.maximum(m_a, m_b)
        w_a = jnp.exp2((m_a - m_all) * to_exp2)
        w_b = jnp.exp2((m_b - m_all) * to_exp2)
        inv_s = 1.0 / jnp.maximum(l_a * w_a + l_b * w_b, 1e-30)
        w_a, w_b = w_a * inv_s, w_b * inv_s
        inv_w = 1.0 / jnp.maximum(stats_w[g][1], 1e-30)
        for h in range(hpg):
            sl = slice(h * QT, (h + 1) * QT)
            o_t = (gate[h:h + 1] * o_cmp[g][:, sl]
                   + gate[hpg + h:hpg + h + 1] * (acc_s[g, :, sl] * w_a[:, sl] + acc_s2[g, :, sl] * w_b[:, sl])
                   + gate[2 * hpg + h:2 * hpg + h + 1] * (acc_w[g, :, sl] * inv_w[:, sl]))
            c0 = (g * hpg + h) * HEAD_DIM
            o_ref[:, c0:c0 + HEAD_DIM] = o_t.T.astype(o_ref.dtype)


def _nsa_attention(proj, cmp_kv, cmp_kv_t, gates, cmp_bias, band, *, batch, seq, kv_chunk0, gate_blk0):
    nq = seq // QT
    grps = NSA_GROUPS
    n_sel = seq // SEL_BLOCK
    rows_c = seq // CMP_STRIDE
    assert WINDOW // QT <= PAD_BLOCKS // (QT // SEL_BLOCK) and WINDOW == 4 * QT and KCH == 2 * QT
    assert kv_chunk0 % grps == 0 and gate_blk0 % grps == 0

    def kv_spec(branch, kv):
        blk = (kv_chunk0 + (branch * 2 + kv) * grps) // grps
        return pl.BlockSpec((grps, seq, HEAD_DIM), lambda b, i: (blk, b, 0))

    return pl.pallas_call(
        _nsa_kernel,
        out_shape=jax.ShapeDtypeStruct((batch * seq, NSA_HEADS * HEAD_DIM), BF16),
        grid=(batch, nq),
        in_specs=[
            pl.BlockSpec((NSA_HEADS, QT, HEAD_DIM), lambda b, i: (0, b * nq + i, 0)),
            pl.BlockSpec((1, grps, rows_c, HEAD_DIM), lambda b, i: (b, 0, 0, 0)),
            pl.BlockSpec((1, grps, HEAD_DIM, rows_c), lambda b, i: (b, 1, 0, 0)),
            kv_spec(1, 0), kv_spec(1, 1), kv_spec(2, 0), kv_spec(2, 1),
            pl.BlockSpec((QT, grps * LANES), lambda b, i: (b * nq + i, gate_blk0 // grps)),
            pl.BlockSpec((NSA_HEADS, LANES, QT), lambda b, i: (0, 0, i)),
            pl.BlockSpec((NSA_HEADS, KCH, QT), lambda b, i: (0, 0, 0)),
        ],
        out_specs=pl.BlockSpec((QT, NSA_HEADS * HEAD_DIM), lambda b, i: (b * nq + i, 0)),
        scratch_shapes=[
            pltpu.VMEM((grps, seq + QT, HEAD_DIM), BF16),
            pltpu.VMEM((grps, nq + 1, HEAD_DIM, QT), BF16),
            pltpu.VMEM((grps, seq + QT, HEAD_DIM), BF16),
            pltpu.VMEM((grps, nq + 1, HEAD_DIM, QT), BF16),
            pltpu.VMEM((grps, HEAD_DIM, NSA_HPG * QT), F32),
            pltpu.VMEM((grps, HEAD_DIM, NSA_HPG * QT), F32),
            pltpu.VMEM((grps, HEAD_DIM, NSA_HPG * QT), F32),
            pltpu.VMEM((grps, PAD_BLOCKS + n_sel, QT), F32),
            pltpu.VMEM((PAD_BLOCKS + n_sel, QT), F32),
        ],
        compiler_params=_cparams(("arbitrary", "arbitrary")),
        name="nsa_attention",
    )(proj, cmp_kv, cmp_kv_t, proj, proj, proj, proj, gates, cmp_bias, band)


def _retention_kernel(q_ref, k_ref, v_ref, g_ref, cos_ref, sin_ref, dmat_ref, xi_ref, zeta_ref,
                      o_ref, state_ref, *, g_chunk):
    @pl.when(pl.program_id(1) == 0)
    def _():
        state_ref[...] = jnp.zeros(state_ref.shape, F32)

    cos, sin = cos_ref[...], sin_ref[...]
    half = RET_QK_DIM // 2
    heads = range(RET_HEADS)
    qb, kr, v, inner, cross = [], [], [], [], []
    for h in heads:
        q = q_ref[h].astype(F32)
        k = k_ref[h].astype(F32)
        qb.append((q * cos + pltpu.roll(q, half, 1) * sin).astype(BF16))
        kr.append((k * cos + pltpu.roll(k, half, 1) * sin) * (RET_QK_DIM ** -0.5))
        v.append(jnp.concatenate([v_ref[2 * h], v_ref[2 * h + 1]], axis=1))
    for h in heads:
        inner.append(_dot_nt(qb[h], kr[h].astype(BF16)))
        cross.append(_dot(qb[h], state_ref[h].astype(BF16)))
    for h in heads:
        kz = (kr[h] * zeta_ref[h]).T.astype(BF16)
        state_ref[h] = state_ref[h] * g_chunk[h] + _dot(kz, v[h])
    for h in heads:
        o = _dot((inner[h] * dmat_ref[h]).astype(BF16), v[h]) + cross[h] * xi_ref[h][:, 0:1]
        mu = jnp.mean(o, axis=-1, keepdims=True)
        var = jnp.mean(jnp.square(o - mu), axis=-1, keepdims=True)
        on = (o - mu) * lax.rsqrt(var + EPS)
        gate = g_ref[:, h * RET_V_DIM:(h + 1) * RET_V_DIM].astype(F32)
        gate = gate * (1.0 / (1.0 + jnp.exp(-gate)))
        o_ref[:, h * RET_V_DIM:(h + 1) * RET_V_DIM] = (on * gate).astype(o_ref.dtype)


def _retention(proj, gates, *, batch, seq, q_blk, k_blk, v_blk):
    c = RET_CHUNK
    n = seq // c
    heads = RET_HEADS
    dk = RET_QK_DIM
    theta = 1.0 / (10000.0 ** np.linspace(0.0, 1.0, dk // 2))
    ang = np.arange(seq)[:, None] * theta[None, :]
    cos = np.concatenate([np.cos(ang), np.cos(ang)], axis=1)
    sin = np.concatenate([-np.sin(ang), np.sin(ang)], axis=1)
    log_g = np.log(1.0 - np.exp2(-5.0 - np.arange(heads)))
    j = np.arange(c)
    diff = j[:, None] - j[None, :]
    dmat = np.where(diff >= 0, np.exp(np.maximum(diff, 0)[None] * log_g[:, None, None]), 0.0)
    xi = np.broadcast_to(np.exp((j + 1.0)[None, :] * log_g[:, None])[:, :, None], (heads, c, LANES))
    zeta = np.broadcast_to(np.exp((c - 1.0 - j)[None, :] * log_g[:, None])[:, :, None], (heads, c, LANES))
    g_chunk = tuple(float(v) for v in np.exp(c * log_g))
    consts = [jnp.asarray(a, F32) for a in (cos, sin, dmat, xi, zeta)]

    full3 = lambda shape: pl.BlockSpec(shape, lambda b, i: (0, 0, 0))
    return pl.pallas_call(
        functools.partial(_retention_kernel, g_chunk=g_chunk),
        out_shape=jax.ShapeDtypeStruct((batch * seq, heads * RET_V_DIM), BF16),
        grid=(batch, n),
        in_specs=[
            pl.BlockSpec((heads, c, LANES), lambda b, i: (q_blk, b * n + i, 0)),
            pl.BlockSpec((heads, c, LANES), lambda b, i: (k_blk, b * n + i, 0)),
            pl.BlockSpec((2 * heads, c, LANES), lambda b, i: (v_blk, b * n + i, 0)),
            pl.BlockSpec((c, heads * RET_V_DIM), lambda b, i: (b * n + i, 0)),
            pl.BlockSpec((c, dk), lambda b, i: (i, 0)),
            pl.BlockSpec((c, dk), lambda b, i: (i, 0)),
            full3((heads, c, c)), full3((heads, c, LANES)), full3((heads, c, LANES)),
        ],
        out_specs=pl.BlockSpec((c, heads * RET_V_DIM), lambda b, i: (b * n + i, 0)),
        scratch_shapes=[pltpu.VMEM((heads, dk, RET_V_DIM), F32)],
        compiler_params=_cparams(("parallel", "arbitrary")),
        name="retention",
    )(proj, proj, proj, gates, *consts)


def _mem_attn_kernel(q_ref, k_ref, v_ref, o_ref):
    scale = MEM_HEAD_DIM ** -0.5
    scores = [(_dot_nt(q_ref[2 * h], k_ref[2 * h]) + _dot_nt(q_ref[2 * h + 1], k_ref[2 * h + 1])) * scale
              for h in range(MEM_HEADS)]
    for h in range(MEM_HEADS):
        s = scores[h]
        mx = jnp.max(s, axis=-1, keepdims=True)
        p = jnp.exp(s - mx)
        p = (p / jnp.sum(p, axis=-1, keepdims=True)).astype(BF16)
        for half in range(2):
            c0 = h * MEM_HEAD_DIM + half * LANES
            o_ref[:, c0:c0 + LANES] = _dot(p, v_ref[2 * h + half]).astype(o_ref.dtype)


def _mem_attention(proj, mkv, *, batch, seq, mem_len, q_blk, tq):
    nq = seq // tq
    chunks = MEM_HEADS * MEM_HEAD_DIM // LANES
    return pl.pallas_call(
        _mem_attn_kernel,
        out_shape=jax.ShapeDtypeStruct((batch * seq, MEM_HEADS * MEM_HEAD_DIM), BF16),
        grid=(batch, nq),
        in_specs=[
            pl.BlockSpec((chunks, tq, LANES), lambda b, i: (q_blk, b * nq + i, 0)),
            pl.BlockSpec((chunks, mem_len, LANES), lambda b, i: (0, b, 0)),
            pl.BlockSpec((chunks, mem_len, LANES), lambda b, i: (1, b, 0)),
        ],
        out_specs=pl.BlockSpec((tq, MEM_HEADS * MEM_HEAD_DIM), lambda b, i: (b * nq + i, 0)),
        compiler_params=_cparams(("parallel", "parallel")),
        name="mem_attention",
    )(proj, mkv, mkv)


def _sigmoid(x):
    return 1.0 / (1.0 + jnp.exp(-x))


def _merge_kernel(ya_ref, yb_ref, yc_ref, wa_ref, wb_ref, wc_ref, ga_ref, gb_ref, gc_ref, o_ref):
    out = _sigmoid(ga_ref[...].astype(F32)) * _dot(ya_ref[...], wa_ref[...])
    out = out + _sigmoid(gb_ref[...].astype(F32)) * _dot(yb_ref[...], wb_ref[...])
    out = out + _sigmoid(gc_ref[...].astype(F32)) * _dot(yc_ref[...], wc_ref[...])
    o_ref[...] = out.astype(o_ref.dtype)


def _merge(ya, yb, yc, wa, wb, wc, gates, *, d_model, gate_col0, tm, tn):
    m, kdim = ya.shape
    y_spec = pl.BlockSpec((tm, kdim), lambda i, j: (i, 0))
    w_spec = pl.BlockSpec((kdim, tn), lambda i, j: (0, j))

    def g_spec(branch):
        base = (gate_col0 + branch * d_model) // tn
        return pl.BlockSpec((tm, tn), lambda i, j: (i, base + j))

    return pl.pallas_call(
        _merge_kernel,
        out_shape=jax.ShapeDtypeStruct((m, d_model), BF16),
        grid=(m // tm, d_model // tn),
        in_specs=[y_spec, y_spec, y_spec, w_spec, w_spec, w_spec, g_spec(0), g_spec(1), g_spec(2)],
        out_specs=pl.BlockSpec((tm, tn), lambda i, j: (i, j)),
        compiler_params=_cparams(("parallel", "parallel")),
        name="branch_merge",
    )(ya, yb, yc, wa, wb, wc, gates, gates, gates)


def _out_proj_kernel(y_ref, w_ref, g_ref, x_ref, o_ref):
    o_ref[...] = x_ref[...] + _rms(_dot(y_ref[...], w_ref[...]), g_ref[...])


def _out_proj(y, w, g, x2d, *, tm):
    m, d = x2d.shape
    return pl.pallas_call(
        _out_proj_kernel,
        out_shape=jax.ShapeDtypeStruct((m, d), F32),
        grid=(m // tm,),
        in_specs=[
            pl.BlockSpec((tm, d), lambda i: (i, 0)),
            pl.BlockSpec((d, d), lambda i: (0, 0)),
            pl.BlockSpec((1, d), lambda i: (0, 0)),
            pl.BlockSpec((tm, d), lambda i: (i, 0)),
        ],
        out_specs=pl.BlockSpec((tm, d), lambda i: (i, 0)),
        compiler_params=_cparams(("parallel",)),
        name="out_proj",
    )(y, w, g, x2d)


def _gelu_tanh(x):
    return 0.5 * x * (1.0 + jnp.tanh(math.sqrt(2.0 / math.pi) * (x + 0.044715 * (x * x * x))))


def _ffn_kernel(x_ref, halo_ref, gpre_ref, wg_ref, wv_ref, cwg_ref, cwv_ref, cbg_ref, cbv_ref,
                wd_ref, gpost_ref, o_ref, h_ref, acc_ref, *, seq):
    i, j = pl.program_id(0), pl.program_id(1)
    tm = x_ref.shape[0]

    @pl.when(j == 0)
    def _():
        h_ref[HALO:, :] = _rms(x_ref[...], gpre_ref[...]).astype(BF16)
        keep = jnp.where((i * tm) % seq == 0, 0.0, 1.0)
        h_ref[:HALO, :] = (_rms(halo_ref[...], gpre_ref[...]) * keep).astype(BF16)
        acc_ref[...] = jnp.zeros(acc_ref.shape, F32)

    h = h_ref[...]

    def conv(u, w_ref, b_ref):
        w = w_ref[...]
        y = b_ref[...] + w[2:3] * u[HALO:]
        y = y + w[1:2] * pltpu.roll(u, 1, 0)[HALO:]
        return y + w[0:1] * pltpu.roll(u, 2, 0)[HALO:]

    yg = conv(_dot(h, wg_ref[...]), cwg_ref, cbg_ref)
    yv = conv(_dot(h, wv_ref[...]), cwv_ref, cbv_ref)
    act = (_gelu_tanh(yg) * yv).astype(BF16)
    acc_ref[...] += _dot(act, wd_ref[...])

    @pl.when(j == pl.num_programs(1) - 1)
    def _():
        o_ref[...] = x_ref[...] + _rms(acc_ref[...], gpost_ref[...])


def _ffn(x2d, gpre, w_up, conv_w, conv_b, w_down, gpost, *, seq, tm, tn):
    m, d = x2d.shape
    d_ff = w_down.shape[0]
    nj = d_ff // tn
    halo_blocks = tm // HALO
    return pl.pallas_call(
        functools.partial(_ffn_kernel, seq=seq),
        out_shape=jax.ShapeDtypeStruct((m, d), F32),
        grid=(m // tm, nj),
        in_specs=[
            pl.BlockSpec((tm, d), lambda i, j: (i, 0)),
            pl.BlockSpec((HALO, d), lambda i, j: (jnp.maximum(i * halo_blocks - 1, 0), 0)),
            pl.BlockSpec((1, d), lambda i, j: (0, 0)),
            pl.BlockSpec((d, tn), lambda i, j: (0, j)),
            pl.BlockSpec((d, tn), lambda i, j: (0, nj + j)),
            pl.BlockSpec((CONV_WIDTH, tn), lambda i, j: (0, j)),
            pl.BlockSpec((CONV_WIDTH, tn), lambda i, j: (0, nj + j)),
            pl.BlockSpec((1, tn), lambda i, j: (0, j)),
            pl.BlockSpec((1, tn), lambda i, j: (0, nj + j)),
            pl.BlockSpec((tn, d), lambda i, j: (j, 0)),
            pl.BlockSpec((1, d), lambda i, j: (0, 0)),
        ],
        out_specs=pl.BlockSpec((tm, d), lambda i, j: (i, 0)),
        scratch_shapes=[pltpu.VMEM((tm + HALO, d), BF16), pltpu.VMEM((tm, d), F32)],
        compiler_params=_cparams(("parallel", "arbitrary")),
        name="conv_ffn",
    )(x2d, x2d, gpre, w_up, w_up, conv_w, conv_w, conv_b, conv_b, w_down, gpost)


def _pick(n, prefs):
    for t in prefs:
        if n % t == 0:
            return t
    raise ValueError(f"no tile in {prefs} divides {n}")


class _Tiles(NamedTuple):
    proj_rows: int
    proj_cols: int
    heads_rows: int
    heads_cols: int
    gate_cols: int
    merge_cols: int
    out_rows: int
    ffn_rows: int
    ffn_cols: int
    mem_q_rows: int


def _tiles(m):
    return _Tiles(proj_rows=_pick(m, (1024, 512, 256, 128)), proj_cols=2 * MXU_WIDTH,
                  heads_rows=_pick(m, (512, 256, 128)), heads_cols=11 * MXU_WIDTH, gate_cols=10 * MXU_WIDTH,
                  merge_cols=4 * MXU_WIDTH,
                  out_rows=_pick(m, (512, 256, 128)), ffn_rows=_pick(m, (512, 256, 128)), ffn_cols=2 * MXU_WIDTH,
                  mem_q_rows=512)


def kernel(x, mem, w_in, cmp_pe_k, cmp_w1_k, cmp_w2_k, cmp_pe_v, cmp_w1_v, cmp_w2_v, rel_bias, w_mem_kv,
           w_br_nsa, w_br_ret, w_br_mem, w_o, w_up, conv_w, conv_b, w_down, g_pre_mix, g_post_mix, g_mem,
           g_pre_ffn, g_post_ffn):
    batch, seq, d_model = x.shape
    mem_len = mem.shape[1]
    depth = w_in.shape[0]
    m = batch * seq
    grps, hpg, dh = NSA_GROUPS, NSA_HPG, HEAD_DIM

    nsa_q = NSA_HEADS * dh
    nsa_kv = 3 * 2 * grps * dh
    nsa_gates = 3 * NSA_HEADS
    ret_qk = RET_HEADS * RET_QK_DIM
    ret_v = RET_HEADS * RET_V_DIM
    mem_q = MEM_HEADS * MEM_HEAD_DIM
    splits = (nsa_q, nsa_kv, nsa_gates, ret_qk, ret_qk, ret_v, ret_v, mem_q, 3 * d_model)
    offs = np.concatenate([[0], np.cumsum(splits)])
    o_nq, o_nkv, o_ng, o_rq, o_rk, o_rv, o_rg, o_mq, o_bg = (int(v) for v in offs[:-1])

    q_chunk0 = 0
    rv_chunk0 = q_chunk0 + nsa_q // LANES
    mq_chunk0 = rv_chunk0 + ret_v // LANES
    rq_chunk0 = mq_chunk0 + mem_q // LANES
    rk_chunk0 = rq_chunk0 + ret_qk // LANES
    kv_chunk0 = rk_chunk0 + ret_qk // LANES
    bg_col0 = ret_v
    ng_col0 = bg_col0 + 3 * d_model
    tiles = _tiles(m)
    gate_cols = -(-(ng_col0 + grps * LANES) // tiles.gate_cols) * tiles.gate_cols

    tbl_flat = rel_bias.reshape(-1)
    cmp_bias, band = _bias_tables(tbl_flat, seq)
    x2d = x.reshape(m, d_model)

    for l in range(depth):
        w = w_in[l]
        cols = lambda o, n: w[:, o:o + n]
        w_attn = jnp.concatenate([cols(o_nq, nsa_q), cols(o_rv, ret_v), cols(o_mq, mem_q), cols(o_rq, ret_qk),
                                  cols(o_rk, ret_qk), cols(o_nkv, nsa_kv)], axis=1).astype(BF16)
        ng = cols(o_ng, nsa_gates).reshape(d_model, 3, grps, hpg).transpose(0, 2, 1, 3).reshape(d_model, grps, 3 * hpg)
        ng = jnp.pad(ng, ((0, 0), (0, 0), (0, LANES - 3 * hpg))).reshape(d_model, grps * LANES)
        w_gate = jnp.concatenate([cols(o_rg, ret_v), cols(o_bg, 3 * d_model), ng], axis=1)
        w_gate = jnp.pad(w_gate, ((0, 0), (0, gate_cols - w_gate.shape[1]))).astype(BF16)

        g_pre = g_pre_mix[l].reshape(1, d_model)
        proj, h_mix = _norm_proj(x2d, g_pre, w_attn, tm=tiles.heads_rows, tn=tiles.heads_cols,
                                 name="in_proj_heads")
        gates = _proj(h_mix, w_gate, tm=tiles.proj_rows, tn=tiles.gate_cols, name="in_proj_gates")

        pe = jnp.stack([cmp_pe_k[l], cmp_pe_v[l]]).reshape(2, 2, CMP_STRIDE * dh)
        w1 = jnp.stack([cmp_w1_k[l], cmp_w1_v[l]]).astype(BF16)
        w2 = jnp.stack([cmp_w2_k[l], cmp_w2_v[l]]).astype(BF16)
        cmp_kv, cmp_kv_t = _compress(proj, kv_chunk0, pe, w1, w2, batch=batch, seq=seq)
        y_a = _nsa_attention(proj, cmp_kv, cmp_kv_t, gates, cmp_bias, band, batch=batch, seq=seq,
                             kv_chunk0=kv_chunk0, gate_blk0=ng_col0 // LANES)

        y_b = _retention(proj, gates, batch=batch, seq=seq, q_blk=rq_chunk0 // RET_HEADS,
                         k_blk=rk_chunk0 // RET_HEADS, v_blk=rv_chunk0 // (2 * RET_HEADS))

        mem2d = mem.reshape(batch * mem_len, d_model)
        mkv, _ = _norm_proj(mem2d, g_mem[l].reshape(1, d_model), w_mem_kv[l].astype(BF16),
                            tm=_tiles(batch * mem_len).proj_rows, tn=tiles.proj_cols, name="mem_kv_proj")
        y_c = _mem_attention(proj, mkv, batch=batch, seq=seq, mem_len=mem_len,
                             q_blk=mq_chunk0 // (mem_q // LANES), tq=tiles.mem_q_rows)

        merged = _merge(y_a, y_b, y_c, w_br_nsa[l].astype(BF16), w_br_ret[l].astype(BF16),
                        w_br_mem[l].astype(BF16), gates, d_model=d_model, gate_col0=bg_col0,
                        tm=tiles.proj_rows, tn=tiles.merge_cols)
        x2d = _out_proj(merged, w_o[l].astype(BF16), g_post_mix[l].reshape(1, d_model), x2d, tm=tiles.out_rows)

        x2d = _ffn(x2d, g_pre_ffn[l].reshape(1, d_model), w_up[l].astype(BF16), conv_w[l],
                   conv_b[l].reshape(1, -1), w_down[l].astype(BF16), g_post_ffn[l].reshape(1, d_model),
                   seq=seq, tm=tiles.ffn_rows, tn=tiles.ffn_cols)
    return x2d.reshape(batch, seq, d_model)
```
